```python
import math
import jax, jax.numpy as jnp
from jax import lax
import numpy as np

D_MODEL = 1024
BATCH = 2
SEQ = 8192
DEPTH = 2
DEC_BATCH = 128
DEC_SEQ = 1
PAST_LEN = 2048
PAGE_SIZE = 128

CHUNK = 128
D_A = 512
G_A = 8
H_B = 8
N_B = 64
D_B = H_B * N_B
LORA_W = 64
LORA_A = 64
LORA_G = 160
GN_EPS_B = 64e-5
H_C = 8
D_HEAD = 64
D_C = H_C * 2 * D_HEAD
ROPE_DIM = D_HEAD // 4
ROPE_THETA = 500000.0
Q_BLOCK = 128
SUBLN_EPS = 1e-5
D_FF = 4 * D_MODEL
EPS = 1e-6
N_A_COLS = 2 * D_A
N_B_COLS = 3 * D_B + LORA_W + LORA_A + LORA_G
N_GATE_COLS = 3 * D_MODEL
N_IN = N_A_COLS + N_B_COLS + 3 * D_C + N_GATE_COLS
IN_SPLITS = (N_A_COLS, N_A_COLS + N_B_COLS, N_A_COLS + N_B_COLS + D_C,
             N_A_COLS + N_B_COLS + 2 * D_C, N_A_COLS + N_B_COLS + 3 * D_C)
B_SPLITS = (D_B, 2 * D_B, 3 * D_B, 3 * D_B + LORA_W, 3 * D_B + LORA_W + LORA_A)

kernel_name = "hybrid_sgu_rwkv7_diffattn_step"


def rms_norm(x, g, eps=EPS):
    xf = x.astype(jnp.float32)
    return (xf * lax.rsqrt(jnp.mean(xf * xf, -1, keepdims=True) + eps) * g).astype(x.dtype)


def layer_norm(x, g, b, eps):
    xf = x.astype(jnp.float32)
    mu = jnp.mean(xf, -1, keepdims=True)
    var = jnp.mean(jnp.square(xf - mu), -1, keepdims=True)
    return ((xf - mu) * lax.rsqrt(var + eps) * g + b).astype(x.dtype)


def rope(x, pos):
    half = ROPE_DIM // 2
    inv_freq = ROPE_THETA ** (-jnp.arange(half, dtype=jnp.float32) / half)
    ang = pos.astype(jnp.float32)[:, None] * inv_freq[None, :]
    cos = jnp.cos(ang)[:, None, None, :]
    sin = jnp.sin(ang)[:, None, None, :]
    xr = x[..., :ROPE_DIM].astype(jnp.float32)
    x1, x2 = xr[..., :half], xr[..., half:]
    rot = jnp.concatenate([x1 * cos - x2 * sin, x2 * cos + x1 * sin], -1).astype(x.dtype)
    return jnp.concatenate([rot, x[..., ROPE_DIM:]], -1)


def spatial_gating(u, v, w_s, b_s):
    B, T, _ = v.shape
    L = min(T, CHUNK)
    w = jnp.tril(w_s[:, :L, :L])
    vc = v.reshape(B, T // L, L, G_A, D_A // G_A)
    s = jnp.einsum("gts,bcsgd->bctgd", w, vc) + b_s[:, :L].T[None, None, :, :, None]
    return u * s.reshape(B, T, D_A)


def rwkv7_scan(s0, r, w, k, v, kk, a):
    def step(s, inp):
        r_t, w_t, k_t, v_t, kk_t, a_t = inp
        sa = jnp.einsum("bhij,bhj->bhi", s, -kk_t)
        s = (s * w_t[:, :, None, :] + sa[..., None] * (kk_t * a_t)[:, :, None, :]
             + v_t[..., None] * k_t[:, :, None, :])
        return s, jnp.einsum("bhij,bhj->bhi", s, r_t)
    seq = tuple(jnp.moveaxis(t, 1, 0) for t in (r, w, k, v, kk, a))
    s, ys = lax.scan(step, s0, seq)
    return s, jnp.moveaxis(ys, 0, 1)


def rwkv7_mix(cols, prev_row, s0, p):
    B, T, _ = cols.shape
    f32 = jnp.float32
    shifted = jnp.concatenate([prev_row[:, None, :], cols[:, :-1]], axis=1)
    xs = cols + (shifted - cols) * p["shift_mu"]
    r, k, v, wl, al, gl = jnp.split(xs, B_SPLITS, axis=-1)
    w_log = -jax.nn.softplus(-(p["w0"] + jnp.tanh(wl) @ p["w2"])) - 0.5
    decay = jnp.exp(-jnp.exp(w_log.astype(f32)))
    a = jax.nn.sigmoid(p["a0"] + al @ p["a2"])
    g = jax.nn.sigmoid(gl) @ p["g2"]
    heads = lambda t: t.reshape(B, T, H_B, N_B).astype(f32)
    kk = heads(k * p["k_k"])
    kk = kk / jnp.maximum(jnp.sqrt(jnp.sum(kk * kk, -1, keepdims=True)), 1e-12)
    k = k * (1.0 + (a - 1.0) * p["k_a"])
    rh, kh, vh, ah, wh = heads(r), heads(k), heads(v), heads(a), heads(decay)
    s_fin, y = rwkv7_scan(s0.astype(f32), rh, wh, kh, vh, kk, ah)
    y = layer_norm(y, p["lnx_g"].reshape(H_B, N_B), p["lnx_b"].reshape(H_B, N_B), GN_EPS_B)
    y = y + jnp.sum(rh * kh * p["r_k"], -1, keepdims=True) * vh
    y = (y.reshape(B, T, D_B) * g).astype(cols.dtype)
    return y, s_fin.astype(s0.dtype), cols[:, -1]


def diff_attn_core(q, k, v, mask, lam):
    s = jnp.einsum("bqhcd,bkhcd->bhcqk", q.astype(jnp.float32), k.astype(jnp.float32))
    s = jnp.where(mask, s / math.sqrt(D_HEAD), -jnp.inf)
    pr = jax.nn.softmax(s, axis=-1)
    pr = pr[:, :, 0] - lam * pr[:, :, 1]
    return jnp.einsum("bhqk,bkhe->bqhe", pr, v.astype(jnp.float32))


def diff_attn_prompt(q, k, v, lam):
    B, T = q.shape[:2]
    nb = T // Q_BLOCK
    qb = q.reshape(B, nb, Q_BLOCK, H_C, 2, D_HEAD).swapaxes(0, 1)
    kpos = jnp.arange(T)

    def one_block(args):
        i, q_blk = args
        qpos = i * Q_BLOCK + jnp.arange(Q_BLOCK)
        return diff_attn_core(q_blk, k, v, kpos[None, :] <= qpos[:, None], lam)

    o = lax.map(one_block, (jnp.arange(nb), qb))
    return o.swapaxes(0, 1).reshape(B, T, H_C, 2 * D_HEAD)


def diff_attn_sample(q, k, v, past_k, past_v, lam):
    B, T = q.shape[:2]
    P = past_k.shape[1]
    k_all = jnp.concatenate([past_k.reshape(B, P, H_C, 2, D_HEAD), k], axis=1)
    v_all = jnp.concatenate([past_v, v], axis=1)
    mask = jnp.arange(P + T)[None, :] <= (P + jnp.arange(T))[:, None]
    return diff_attn_core(q, k_all, v_all, mask, lam)


def hybrid_layer(x, pos, lyr, past_k, past_v, prev_shift, s0, p):
    B, T, _ = x.shape
    h = rms_norm(x, p["norm_mix"])
    proj = h @ p["w_in"]
    c_a, c_b, c_q, c_k, c_v, c_g = jnp.split(proj, IN_SPLITS, axis=-1)
    u, va = jnp.split(jax.nn.gelu(c_a, approximate=False), 2, axis=-1)
    va = layer_norm(va, p["sgu_ln_g"], p["sgu_ln_b"], EPS)
    o_a = spatial_gating(u, va, p["sgu_w"], p["sgu_b"])
    o_b, s_new, shift_new = rwkv7_mix(c_b, prev_shift, s0, p)
    q = rope(c_q.reshape(B, T, H_C, 2, D_HEAD), pos)
    k = rope(c_k.reshape(B, T, H_C, 2, D_HEAD), pos)
    vc = c_v.reshape(B, T, H_C, 2 * D_HEAD)
    lam_init = 0.8 - 0.6 * math.exp(-0.3 * lyr)
    lam = (jnp.exp(jnp.sum(p["lam_q1"].astype(jnp.float32) * p["lam_k1"].astype(jnp.float32)))
           - jnp.exp(jnp.sum(p["lam_q2"].astype(jnp.float32) * p["lam_k2"].astype(jnp.float32)))
           + lam_init)
    if past_k is None:
        o = diff_attn_prompt(q, k, vc, lam)
    else:
        o = diff_attn_sample(q, k, vc, past_k, past_v, lam)
    o = rms_norm(o, p["subln_g"], SUBLN_EPS) * (1.0 - lam_init)
    o_c = o.reshape(B, T, D_C).astype(x.dtype)
    g_a, g_b, g_c = jnp.split(jax.nn.sigmoid(c_g), 3, axis=-1)
    merged = g_a * (o_a @ p["w_br_a"]) + g_b * (o_b @ p["w_br_b"]) + g_c * (o_c @ p["w_br_c"])
    x = x + merged @ p["w_out"]
    h2 = rms_norm(x, p["norm_ffn"])
    x = x + jnp.square(jax.nn.relu(h2 @ p["w_up"])) @ p["w_down"]
    k_rows = k.reshape(B, T, H_C, 2 * D_HEAD)
    return x, k_rows, vc, s_new, shift_new, va


def setup_inputs(seed: int = 0) -> dict:
    key = jax.random.key(seed)
    ks = iter(jax.random.split(key, 48))
    nrm = lambda shape, scale: scale * jax.random.normal(next(ks), shape, jnp.float32)
    uni = lambda shape, lo, hi: jax.random.uniform(next(ks), shape, jnp.float32, lo, hi)
    n_pages = PAST_LEN // PAGE_SIZE
    n_used = DEC_BATCH * n_pages
    n_pool = n_used + n_used // 4
    x_prompt = nrm((BATCH, SEQ, D_MODEL), 1.0)
    x_sample = nrm((DEC_BATCH, DEC_SEQ, D_MODEL), 1.0)
    cache_k = nrm((DEPTH, n_pool, PAGE_SIZE, H_C, 2 * D_HEAD), 1.0)
    cache_v = nrm((DEPTH, n_pool, PAGE_SIZE, H_C, 2 * D_HEAD), 1.0)
    state_rwkv = nrm((DEPTH, DEC_BATCH, H_B, N_B, N_B), 0.5)
    state_shift = nrm((DEPTH, DEC_BATCH, N_B_COLS), 1.0)
    perm = jax.random.permutation(next(ks), n_pool)
    page_table = perm[:n_used].reshape(DEC_BATCH, n_pages).astype(jnp.int32)
    return {
        "x_prompt": x_prompt, "x_sample": x_sample,
        "cache_k": cache_k, "cache_v": cache_v,
        "state_rwkv": state_rwkv, "state_shift": state_shift,
        "page_table": page_table,
        "norm_mix": 1.0 + nrm((DEPTH, D_MODEL), 0.1),
        "w_in": nrm((DEPTH, D_MODEL, N_IN), D_MODEL ** -0.5),
        "sgu_ln_g": 1.0 + nrm((DEPTH, D_A), 0.1),
        "sgu_ln_b": nrm((DEPTH, D_A), 0.1),
        "sgu_w": nrm((DEPTH, G_A, CHUNK, CHUNK), CHUNK ** -0.5),
        "sgu_b": 1.0 + nrm((DEPTH, G_A, CHUNK), 0.1),
        "shift_mu": uni((DEPTH, N_B_COLS), 0.0, 1.0),
        "w0": uni((DEPTH, D_B), -6.0, 0.0),
        "w2": nrm((DEPTH, LORA_W, D_B), 0.1),
        "a0": nrm((DEPTH, D_B), 0.1),
        "a2": nrm((DEPTH, LORA_A, D_B), 0.5 * LORA_A ** -0.5),
        "g2": nrm((DEPTH, LORA_G, D_B), LORA_G ** -0.5),
        "k_k": 0.85 + nrm((DEPTH, D_B), 0.1),
        "k_a": 1.0 + nrm((DEPTH, D_B), 0.1),
        "r_k": nrm((DEPTH, H_B, N_B), 0.1),
        "lnx_g": 1.0 + nrm((DEPTH, D_B), 0.1),
        "lnx_b": nrm((DEPTH, D_B), 0.1),
        "lam_q1": nrm((DEPTH, D_HEAD), 0.1),
        "lam_k1": nrm((DEPTH, D_HEAD), 0.1),
        "lam_q2": nrm((DEPTH, D_HEAD), 0.1),
        "lam_k2": nrm((DEPTH, D_HEAD), 0.1),
        "subln_g": 1.0 + nrm((DEPTH, 2 * D_HEAD), 0.1),
        "w_br_a": nrm((DEPTH, D_A, D_MODEL), D_A ** -0.5),
        "w_br_b": nrm((DEPTH, D_B, D_MODEL), D_B ** -0.5),
        "w_br_c": nrm((DEPTH, D_C, D_MODEL), D_C ** -0.5),
        "w_out": nrm((DEPTH, D_MODEL, D_MODEL), D_MODEL ** -0.5),
        "norm_ffn": 1.0 + nrm((DEPTH, D_MODEL), 0.1),
        "w_up": nrm((DEPTH, D_MODEL, D_FF), D_MODEL ** -0.5),
        "w_down": nrm((DEPTH, D_FF, D_MODEL), D_FF ** -0.5),
        "norm_final": 1.0 + nrm((D_MODEL,), 0.1),
    }


def reference(x_prompt, x_sample, cache_k, cache_v, state_rwkv, state_shift, page_table,
              norm_mix, w_in, sgu_ln_g, sgu_ln_b, sgu_w, sgu_b, shift_mu, w0, w2, a0, a2, g2,
              k_k, k_a, r_k, lnx_g, lnx_b, lam_q1, lam_k1, lam_q2, lam_k2, subln_g,
              w_br_a, w_br_b, w_br_c, w_out, norm_ffn, w_up, w_down, norm_final):
    Bp, Tp, _ = x_prompt.shape
    Bs, Ts, _ = x_sample.shape
    pos_p = jnp.arange(Tp)
    pos_s = PAST_LEN + jnp.arange(Ts)
    xp, xs = x_prompt, x_sample
    kp_l, vp_l, sp_l, shp_l = [], [], [], []
    ks_l, vs_l, ss_l, shs_l, sgu_l = [], [], [], [], []
    for l in range(DEPTH):
        p = dict(norm_mix=norm_mix[l], w_in=w_in[l], sgu_ln_g=sgu_ln_g[l], sgu_ln_b=sgu_ln_b[l],
                 sgu_w=sgu_w[l], sgu_b=sgu_b[l], shift_mu=shift_mu[l], w0=w0[l], w2=w2[l],
                 a0=a0[l], a2=a2[l], g2=g2[l], k_k=k_k[l], k_a=k_a[l], r_k=r_k[l],
                 lnx_g=lnx_g[l], lnx_b=lnx_b[l], lam_q1=lam_q1[l], lam_k1=lam_k1[l],
                 lam_q2=lam_q2[l], lam_k2=lam_k2[l], subln_g=subln_g[l], w_br_a=w_br_a[l],
                 w_br_b=w_br_b[l], w_br_c=w_br_c[l], w_out=w_out[l], norm_ffn=norm_ffn[l],
                 w_up=w_up[l], w_down=w_down[l])
        s0_p = jnp.zeros((Bp, H_B, N_B, N_B), x_prompt.dtype)
        sh0_p = jnp.zeros((Bp, N_B_COLS), x_prompt.dtype)
        xp, kp, vp, sp, shp, _ = hybrid_layer(xp, pos_p, l, None, None, sh0_p, s0_p, p)
        past_k = cache_k[l][page_table].reshape(Bs, -1, H_C, 2 * D_HEAD)
        past_v = cache_v[l][page_table].reshape(Bs, -1, H_C, 2 * D_HEAD)
        xs, kso, vso, sso, shso, vsgu = hybrid_layer(xs, pos_s, l, past_k, past_v,
                                                     state_shift[l], state_rwkv[l], p)
        kp_l.append(kp); vp_l.append(vp); sp_l.append(sp); shp_l.append(shp)
        ks_l.append(kso); vs_l.append(vso); ss_l.append(sso); shs_l.append(shso); sgu_l.append(vsgu)
    y_prompt = rms_norm(xp, norm_final)
    y_sample = rms_norm(xs, norm_final)
    return (y_prompt, y_sample,
            jnp.stack(kp_l), jnp.stack(vp_l), jnp.stack(sp_l), jnp.stack(shp_l),
            jnp.stack(ks_l), jnp.stack(vs_l), jnp.stack(ss_l), jnp.stack(shs_l), jnp.stack(sgu_l))
```

```python
import functools
import math

import numpy as np
import jax
import jax.numpy as jnp
from jax import lax
from jax.experimental import pallas as pl
from jax.experimental.pallas import tpu as pltpu

F32 = jnp.float32
BF16 = jnp.bfloat16

D_MODEL = 1024
PAST_LEN = 2048
PAGE_SIZE = 128
CHUNK = 128
D_A = 512
G_A = 8
H_B = 8
N_B = 64
D_B = H_B * N_B
LORA_W = 64
LORA_A = 64
LORA_G = 160
GN_EPS_B = 64e-5
H_C = 8
D_HEAD = 64
D_C = H_C * 2 * D_HEAD
ROPE_DIM = D_HEAD // 4
ROPE_THETA = 500000.0
SUBLN_EPS = 1e-5
D_FF = 4 * D_MODEL
EPS = 1e-6
N_A_COLS = 2 * D_A
N_B_COLS = 3 * D_B + LORA_W + LORA_A + LORA_G
LANES = 128
SUBLANES = 8
LW_PAD = 128
LA_PAD = 128
LG_PAD = 256
N_B_PAD = 3 * D_B + LW_PAD + LA_PAD + LG_PAD
VMEM_LIMIT = 52 * 1024 * 1024


def _cparams(*sem):
    return pltpu.CompilerParams(dimension_semantics=sem, vmem_limit_bytes=VMEM_LIMIT)


def _rms(x, g, eps):
    return x * lax.rsqrt(jnp.mean(x * x, axis=-1, keepdims=True) + eps) * g


def _proj_kernel(*refs, rope, scale, n_out):
    x_ref, g_ref, w_ref = refs[:3]
    pos = 3
    if rope:
        c_ref, s1_ref, s2_ref = refs[3:6]
        pos = 6
    out_refs = refs[pos:pos + n_out]
    h_ref = refs[pos + n_out]

    @pl.when(pl.program_id(1) == 0)
    def _():
        h_ref[...] = _rms(x_ref[...], g_ref[...], EPS).astype(BF16)

    acc = jnp.dot(h_ref[...], w_ref[...], preferred_element_type=F32)
    if rope:
        c, s1, s2 = c_ref[...], s1_ref[...], s2_ref[...]
        parts = []
        for hh in range(acc.shape[1] // LANES):
            a = acc[:, hh * LANES:(hh + 1) * LANES]
            parts.append(a * c + pltpu.roll(a, ROPE_DIM // 2, 1) * s1
                         + pltpu.roll(a, LANES - ROPE_DIM // 2, 1) * s2)
        acc = jnp.concatenate(parts, axis=1)
    if scale != 1.0:
        acc = acc * scale
    for o_ref in out_refs:
        o_ref[...] = acc.astype(o_ref.dtype)


def _proj(x, g, w, *, tm, tn, out_dtypes, rope_tabs=None, scale=1.0, name="proj"):
    m, d = x.shape
    n = w.shape[1]
    in_specs = [pl.BlockSpec((tm, d), lambda i, j: (i, 0)),
                pl.BlockSpec((1, d), lambda i, j: (0, 0)),
                pl.BlockSpec((d, tn), lambda i, j: (0, j))]
    args = [x, g, w]
    if rope_tabs is not None:
        nt = rope_tabs[0].shape[0] // tm
        in_specs += [pl.BlockSpec((tm, LANES), lambda i, j: (i % nt, 0))] * 3
        args += list(rope_tabs)
    outs = pl.pallas_call(
        functools.partial(_proj_kernel, rope=rope_tabs is not None, scale=scale, n_out=len(out_dtypes)),
        grid=(m // tm, n // tn),
        in_specs=in_specs,
        out_specs=[pl.BlockSpec((tm, tn), lambda i, j: (i, j)) for _ in out_dtypes],
        out_shape=[jax.ShapeDtypeStruct((m, n), dt) for dt in out_dtypes],
        scratch_shapes=[pltpu.VMEM((tm, d), BF16)],
        compiler_params=_cparams("parallel", "arbitrary"),
        name=name,
    )(*args)
    return outs


def _rope_kernel(pos_ref, invf_ref, c_ref, s1_ref, s2_ref):
    ang = pos_ref[...] * invf_ref[...]
    lane = lax.broadcasted_iota(jnp.int32, ang.shape, 1) % D_HEAD
    first = lane < ROPE_DIM // 2
    second = (lane >= ROPE_DIM // 2) & (lane < ROPE_DIM)
    cos, sin = jnp.cos(ang), jnp.sin(ang)
    c_ref[...] = jnp.where(first | second, cos, 1.0)
    s1_ref[...] = jnp.where(second, sin, 0.0)
    s2_ref[...] = jnp.where(first, -sin, 0.0)


def _rope_tables(pos):
    t = pos.shape[0]
    half = ROPE_DIM // 2
    inv_freq = ROPE_THETA ** (-jnp.arange(half, dtype=F32) / half)
    blk = jnp.concatenate([inv_freq, inv_freq, jnp.zeros((D_HEAD - ROPE_DIM,), F32)])
    invf = jnp.concatenate([blk, blk]).reshape(1, LANES)
    tm = min(t, 512)
    return pl.pallas_call(
        _rope_kernel,
        grid=(t // tm,),
        in_specs=[pl.BlockSpec((tm, 1), lambda i: (i, 0)), pl.BlockSpec((1, LANES), lambda i: (0, 0))],
        out_specs=[pl.BlockSpec((tm, LANES), lambda i: (i, 0))] * 3,
        out_shape=[jax.ShapeDtypeStruct((t, LANES), F32)] * 3,
        compiler_params=_cparams("parallel"),
        name="rope_tables",
    )(pos.astype(F32).reshape(t, 1), invf)


def _gelu_ln(ca, lng, lnb):
    gx = 0.5 * ca * (1.0 + lax.erf(ca * math.sqrt(0.5)))
    u, v = gx[:, :D_A], gx[:, D_A:]
    d = v - jnp.mean(v, axis=-1, keepdims=True)
    va = d * lax.rsqrt(jnp.mean(d * d, axis=-1, keepdims=True) + EPS) * lng + lnb
    return u, va


def _sgu_prompt_kernel(ca_ref, lng_ref, lnb_ref, w_ref, b_ref, oa_ref, *, nchunk):
    u, va = _gelu_ln(ca_ref[...], lng_ref[...], lnb_ref[...])
    vab = va.astype(BF16)
    row = lax.broadcasted_iota(jnp.int32, (CHUNK, CHUNK), 0)
    col = lax.broadcasted_iota(jnp.int32, (CHUNK, CHUNK), 1)
    wcat = jnp.concatenate([jnp.where(col <= row, w_ref[g], 0.0).astype(BF16) for g in range(G_A)], axis=1)
    grp = lax.broadcasted_iota(jnp.int32, (CHUNK, D_A), 1) // (D_A // G_A)
    for ci in range(nchunk):
        vc = vab[ci * CHUNK:(ci + 1) * CHUNK]
        vbig = jnp.concatenate([jnp.where(grp == g, vc, jnp.zeros_like(vc)) for g in range(G_A)], axis=0)
        s = jnp.dot(wcat, vbig, preferred_element_type=F32) + b_ref[...]
        oa_ref[ci * CHUNK:(ci + 1) * CHUNK, :] = (u[ci * CHUNK:(ci + 1) * CHUNK] * s).astype(oa_ref.dtype)


def _sgu_prompt(ca, lng, lnb, w, bias_td, *, tm):
    m = ca.shape[0]
    return pl.pallas_call(
        functools.partial(_sgu_prompt_kernel, nchunk=tm // CHUNK),
        grid=(m // tm,),
        in_specs=[pl.BlockSpec((tm, N_A_COLS), lambda i: (i, 0)),
                  pl.BlockSpec((1, D_A), lambda i: (0, 0)),
                  pl.BlockSpec((1, D_A), lambda i: (0, 0)),
                  pl.BlockSpec((G_A, CHUNK, CHUNK), lambda i: (0, 0, 0)),
                  pl.BlockSpec((CHUNK, D_A), lambda i: (0, 0))],
        out_specs=pl.BlockSpec((tm, D_A), lambda i: (i, 0)),
        out_shape=jax.ShapeDtypeStruct((m, D_A), BF16),
        compiler_params=_cparams("parallel"),
        name="sgu_prompt",
    )(ca, lng, lnb, w, bias_td)


def _sgu_sample_kernel(ca_ref, lng_ref, lnb_ref, w0_ref, b0_ref, oa_ref, va_ref):
    u, va = _gelu_ln(ca_ref[...], lng_ref[...], lnb_ref[...])
    va_ref[...] = va
    oa_ref[...] = (u * (va * w0_ref[...] + b0_ref[...])).astype(oa_ref.dtype)


def _sgu_sample(ca, lng, lnb, w0, b0):
    m = ca.shape[0]
    row = lambda n: pl.BlockSpec((1, n), lambda i: (0, 0))
    return pl.pallas_call(
        _sgu_sample_kernel,
        grid=(1,),
        in_specs=[pl.BlockSpec((m, N_A_COLS), lambda i: (0, 0)), row(D_A), row(D_A), row(D_A), row(D_A)],
        out_specs=[pl.BlockSpec((m, D_A), lambda i: (0, 0))] * 2,
        out_shape=[jax.ShapeDtypeStruct((m, D_A), BF16), jax.ShapeDtypeStruct((m, D_A), F32)],
        compiler_params=_cparams("arbitrary"),
        name="sgu_sample",
    )(ca, lng, lnb, w0, b0)


def _seg_sum(x, ones_bd):
    hi = x.astype(BF16)
    lo = (x - hi.astype(F32)).astype(BF16)
    return (jnp.dot(hi, ones_bd, preferred_element_type=F32)
            + jnp.dot(lo, ones_bd, preferred_element_type=F32))


def _rwkv_pre_kernel(*refs, seq):
    if seq:
        c_ref, prev_ref = refs[:2]
    else:
        c_ref, sh_ref = refs[:2]
    (mu_ref, w0_ref, w2_ref, a0_ref, a2_ref, g2_ref, kk_ref, ka_ref, rk_ref, bd_ref) = refs[2:12]
    (r_o, w_o, k_o, v_o, a_o, b_o, g_o, rkv_o) = refs[12:20]
    if seq:
        last_o, carry_ref = refs[20:22]
        cols = c_ref[0]
        tm = cols.shape[0]

        @pl.when(pl.program_id(1) == 0)
        def _():
            carry_ref[...] = prev_ref[0]

        rolled = pltpu.roll(cols, 1, 0)
        first = lax.broadcasted_iota(jnp.int32, cols.shape, 0) == 0
        shifted = jnp.where(first, carry_ref[...], rolled)
        carry_ref[...] = cols[tm - 1:tm, :]
        last_o[0] = cols[tm - 1:tm, :]
    else:
        cols = c_ref[...]
        shifted = sh_ref[...]
    xs = cols + (shifted - cols) * mu_ref[...]
    r = xs[:, 0:D_B]
    k = xs[:, D_B:2 * D_B]
    v = xs[:, 2 * D_B:3 * D_B]
    o = 3 * D_B
    wl = xs[:, o:o + LW_PAD]
    al = xs[:, o + LW_PAD:o + LW_PAD + LA_PAD]
    gl = xs[:, o + LW_PAD + LA_PAD:o + LW_PAD + LA_PAD + LG_PAD]
    z = -(w0_ref[...] + jnp.dot(jnp.tanh(wl).astype(BF16), w2_ref[...], preferred_element_type=F32))
    softplus = jnp.maximum(z, 0.0) + jnp.log1p(jnp.exp(-jnp.abs(z)))
    decay = jnp.exp(-jnp.exp(-softplus - 0.5))
    a = jax.nn.sigmoid(a0_ref[...] + jnp.dot(al.astype(BF16), a2_ref[...], preferred_element_type=F32))
    g = jnp.dot(jax.nn.sigmoid(gl).astype(BF16), g2_ref[...], preferred_element_type=F32)
    bd = bd_ref[...]
    kk = k * kk_ref[...]
    kk = kk / jnp.maximum(jnp.sqrt(_seg_sum(kk * kk, bd)), 1e-12)
    k2 = k * (1.0 + (a - 1.0) * ka_ref[...])
    if seq:
        r_o[0], w_o[0], k_o[0], v_o[0], a_o[0], b_o[0], g_o[0] = r, decay, k2, v, -kk, kk * a, g
        rkv_o[0] = _seg_sum(r * k2 * rk_ref[...], bd) * v
    else:
        r_o[...], w_o[...], k_o[...], v_o[...], a_o[...], b_o[...], g_o[...] = r, decay, k2, v, -kk, kk * a, g
        rkv_o[...] = _seg_sum(r * k2 * rk_ref[...], bd) * v


def _rwkv_pre(cols, shift_src, params, *, seq, tm):
    prm_specs = []
    for p in params:
        prm_specs.append(pl.BlockSpec(p.shape, (lambda b, i: (0, 0)) if seq else (lambda i: (0, 0))))
    if seq:
        nb, t, _ = cols.shape
        grid = (nb, t // tm)
        in_specs = [pl.BlockSpec((1, tm, N_B_PAD), lambda b, i: (b, i, 0)),
                    pl.BlockSpec((1, 1, N_B_PAD), lambda b, i: (b, 0, 0))] + prm_specs
        ospec = pl.BlockSpec((1, tm, D_B), lambda b, i: (b, i, 0))
        oshape = jax.ShapeDtypeStruct((nb, t, D_B), F32)
        out_specs = [ospec] * 8 + [pl.BlockSpec((1, 1, N_B_PAD), lambda b, i: (b, 0, 0))]
        out_shape = [oshape] * 8 + [jax.ShapeDtypeStruct((nb, 1, N_B_PAD), F32)]
        scratch = [pltpu.VMEM((1, N_B_PAD), F32)]
        sem = ("parallel", "arbitrary")
    else:
        m = cols.shape[0]
        grid = (m // tm,)
        in_specs = [pl.BlockSpec((tm, N_B_PAD), lambda i: (i, 0))] * 2 + prm_specs
        out_specs = [pl.BlockSpec((tm, D_B), lambda i: (i, 0))] * 8
        out_shape = [jax.ShapeDtypeStruct((m, D_B), F32)] * 8
        scratch = []
        sem = ("parallel",)
    return pl.pallas_call(
        functools.partial(_rwkv_pre_kernel, seq=seq),
        grid=grid, in_specs=in_specs, out_specs=out_specs, out_shape=out_shape,
        scratch_shapes=scratch, compiler_params=_cparams(*sem),
        name="rwkv_pre_seq" if seq else "rwkv_pre_rows",
    )(cols, shift_src, *params)


def _scan_pair_step(S, a, w, b, k, v, r):
    lane = lax.broadcasted_iota(jnp.int32, (N_B, LANES), 1)
    row = lax.broadcasted_iota(jnp.int32, (N_B, LANES), 0)
    lo = lane < N_B
    e1 = lane == row
    e2 = lane == row + N_B

    def half_sums(p):
        t1 = jnp.sum(jnp.where(lo, p, 0.0), axis=1, keepdims=True)
        t2 = jnp.sum(jnp.where(lo, 0.0, p), axis=1, keepdims=True)
        return t1, t2

    t1, t2 = half_sums(S * a)
    sa = jnp.where(lo, t1, t2)
    vb = jnp.where(lo, jnp.sum(jnp.where(e1, v, 0.0), axis=1, keepdims=True),
                   jnp.sum(jnp.where(e2, v, 0.0), axis=1, keepdims=True))
    S = S * w + sa * b + vb * k
    y1, y2 = half_sums(S * r)
    y = jnp.sum(jnp.where(e1, y1, 0.0) + jnp.where(e2, y2, 0.0), axis=0, keepdims=True)
    return S, y


def _rwkv_scan_seq_kernel(r_ref, w_ref, k_ref, v_ref, a_ref, b_ref, y_ref, sf_ref, s_ref, *, nb, tb):
    npair = H_B // 2
    ti = pl.program_id(0)

    @pl.when(ti == 0)
    def _():
        s_ref[...] = jnp.zeros_like(s_ref)

    def group(gi, carry):
        base = pl.multiple_of(gi * SUBLANES, SUBLANES)
        for bi in range(nb):
            for p in range(npair):
                sl = (bi, pl.ds(base, SUBLANES), slice(p * LANES, (p + 1) * LANES))
                a, w, b, k, v, r = (ref[sl] for ref in (a_ref, w_ref, b_ref, k_ref, v_ref, r_ref))
                S = s_ref[bi * npair + p]
                ys = []
                for j in range(SUBLANES):
                    row = slice(j, j + 1)
                    S, y = _scan_pair_step(S, a[row], w[row], b[row], k[row], v[row], r[row])
                    ys.append(y)
                s_ref[bi * npair + p] = S
                y_ref[sl] = jnp.concatenate(ys, axis=0)
        return carry

    lax.fori_loop(0, tb // SUBLANES, group, 0)

    @pl.when(ti == pl.num_programs(0) - 1)
    def _():
        for bi in range(nb):
            for p in range(npair):
                S = s_ref[bi * npair + p]
                sf_ref[bi, 2 * p] = S[:, :N_B]
                sf_ref[bi, 2 * p + 1] = S[:, N_B:]


def _rwkv_scan_seq(r, w, k, v, a, b, *, tb):
    nb, t, _ = r.shape
    spec = pl.BlockSpec((nb, tb, D_B), lambda i: (0, i, 0))
    return pl.pallas_call(
        functools.partial(_rwkv_scan_seq_kernel, nb=nb, tb=tb),
        grid=(t // tb,),
        in_specs=[spec] * 6,
        out_specs=[spec, pl.BlockSpec((nb, H_B, N_B, N_B), lambda i: (0, 0, 0, 0))],
        out_shape=[jax.ShapeDtypeStruct((nb, t, D_B), F32), jax.ShapeDtypeStruct((nb, H_B, N_B, N_B), F32)],
        scratch_shapes=[pltpu.VMEM((nb * H_B // 2, N_B, LANES), F32)],
        compiler_params=_cparams("arbitrary"),
        name="rwkv_scan_seq",
    )(r, w, k, v, a, b)


def _rwkv_scan_rows_kernel(r_ref, w_ref, k_ref, v_ref, a_ref, b_ref, s0_ref, y_ref, sf_ref, *, tb):
    def group(gi, carry):
        base = pl.multiple_of(gi * SUBLANES, SUBLANES)
        for p in range(H_B // 2):
            sl = (pl.ds(base, SUBLANES), slice(p * LANES, (p + 1) * LANES))
            a, w, b, k, v, r = (ref[sl] for ref in (a_ref, w_ref, b_ref, k_ref, v_ref, r_ref))
            ys = []
            for j in range(SUBLANES):
                row = slice(j, j + 1)
                S0 = jnp.concatenate([s0_ref[base + j, 2 * p], s0_ref[base + j, 2 * p + 1]], axis=1)
                S, y = _scan_pair_step(S0, a[row], w[row], b[row], k[row], v[row], r[row])
                sf_ref[base + j, 2 * p] = S[:, :N_B]
                sf_ref[base + j, 2 * p + 1] = S[:, N_B:]
                ys.append(y)
            y_ref[sl] = jnp.concatenate(ys, axis=0)
        return carry

    lax.fori_loop(0, tb // SUBLANES, group, 0)


def _rwkv_scan_rows(r, w, k, v, a, b, s0, *, tb):
    m = r.shape[0]
    spec = pl.BlockSpec((tb, D_B), lambda i: (i, 0))
    sspec = pl.BlockSpec((tb, H_B, N_B, N_B), lambda i: (i, 0, 0, 0))
    return pl.pallas_call(
        functools.partial(_rwkv_scan_rows_kernel, tb=tb),
        grid=(m // tb,),
        in_specs=[spec] * 6 + [sspec],
        out_specs=[spec, sspec],
        out_shape=[jax.ShapeDtypeStruct((m, D_B), F32), jax.ShapeDtypeStruct((m, H_B, N_B, N_B), F32)],
        compiler_params=_cparams("parallel"),
        name="rwkv_scan_rows",
    )(r, w, k, v, a, b, s0)


def _rwkv_post_kernel(y_ref, rkv_ref, g_ref, lg_ref, lb_ref, bd_ref, o_ref):
    y = y_ref[...]
    bd = bd_ref[...]
    d = y - _seg_sum(y, bd) * (1.0 / N_B)
    var = _seg_sum(d * d, bd) * (1.0 / N_B)
    yn = d * lax.rsqrt(var + GN_EPS_B) * lg_ref[...] + lb_ref[...]
    o_ref[...] = ((yn + rkv_ref[...]) * g_ref[...]).astype(o_ref.dtype)


def _rwkv_post(y, rkv, g, lg, lb, bd, *, tm):
    m = y.shape[0]
    spec = pl.BlockSpec((tm, D_B), lambda i: (i, 0))
    row = pl.BlockSpec((1, D_B), lambda i: (0, 0))
    return pl.pallas_call(
        _rwkv_post_kernel,
        grid=(m // tm,),
        in_specs=[spec, spec, spec, row, row, pl.BlockSpec((D_B, D_B), lambda i: (0, 0))],
        out_specs=spec,
        out_shape=jax.ShapeDtypeStruct((m, D_B), BF16),
        compiler_params=_cparams("parallel"),
        name="rwkv_post",
    )(y, rkv, g, lg, lb, bd)


def _lam_full(lq1, lk1, lq2, lk2, lam_init):
    return (jnp.exp(jnp.sum(lq1 * lk1, axis=-1, keepdims=True))
            - jnp.exp(jnp.sum(lq2 * lk2, axis=-1, keepdims=True)) + lam_init)


def _attn_prompt_kernel(qi_ref, ki_ref, q_ref, k_ref, v_ref, lq1, lk1, lq2, lk2, sg_ref, o_ref,
                        m1, l1, a1, m2, l2, a2, *, tq, tk, lam_init):
    s = pl.program_id(2)
    qi = qi_ref[s]
    ki = ki_ref[s]

    @pl.when(ki == 0)
    def _():
        for m_ref, l_ref, a_ref in ((m1, l1, a1), (m2, l2, a2)):
            m_ref[...] = jnp.full_like(m_ref, -jnp.inf)
            l_ref[...] = jnp.zeros_like(l_ref)
            a_ref[...] = jnp.zeros_like(a_ref)

    q = q_ref[0]
    k = k_ref[0]
    v = v_ref[0]
    lane = lax.broadcasted_iota(jnp.int32, q.shape, 1)
    zero = jnp.zeros_like(q)
    q_sub = (jnp.where(lane < D_HEAD, q, zero), jnp.where(lane < D_HEAD, zero, q))

    def update(masked):
        for qs, m_ref, l_ref, a_ref in ((q_sub[0], m1, l1, a1), (q_sub[1], m2, l2, a2)):
            sc = lax.dot_general(qs, k, (((1,), (1,)), ((), ())), preferred_element_type=F32)
            if masked:
                rr = lax.broadcasted_iota(jnp.int32, sc.shape, 0)
                cc = lax.broadcasted_iota(jnp.int32, sc.shape, 1)
                sc = jnp.where(cc <= rr, sc, -jnp.inf)
            m_old = m_ref[...]
            m_new = jnp.maximum(m_old, jnp.max(sc, axis=-1, keepdims=True))
            alpha = jnp.exp(m_old - m_new)
            p = jnp.exp(sc - m_new)
            l_ref[...] = alpha * l_ref[...] + jnp.sum(p, axis=-1, keepdims=True)
            a_ref[...] = alpha * a_ref[...] + jnp.dot(p.astype(BF16), v, preferred_element_type=F32)
            m_ref[...] = m_new

    @pl.when(ki < qi)
    def _():
        update(False)

    @pl.when(ki == qi)
    def _():
        update(True)
        lam = _lam_full(lq1[...], lk1[...], lq2[...], lk2[...], lam_init)
        o = a1[...] / l1[...] - lam * (a2[...] / l2[...])
        o_ref[0] = (_rms(o, sg_ref[...], SUBLN_EPS) * (1.0 - lam_init)).astype(o_ref.dtype)


def _attn_prompt(q, k, v, lams, subln_g, lam_init, *, tq):
    nb, t, _ = q.shape
    tk = tq
    nq = t // tq
    pairs = [(i, j) for i in range(nq) for j in range(i + 1)]
    qi_tab = jnp.asarray(np.array([p[0] for p in pairs], np.int32))
    ki_tab = jnp.asarray(np.array([p[1] for p in pairs], np.int32))
    qspec = pl.BlockSpec((1, tq, LANES), lambda b, h, s, qt, kt: (b, qt[s], h))
    kspec = pl.BlockSpec((1, tk, LANES), lambda b, h, s, qt, kt: (b, kt[s], h))
    row = lambda n: pl.BlockSpec((1, n), lambda b, h, s, qt, kt: (0, 0))
    grid_spec = pltpu.PrefetchScalarGridSpec(
        num_scalar_prefetch=2,
        grid=(nb, H_C, len(pairs)),
        in_specs=[qspec, kspec, kspec, row(D_HEAD), row(D_HEAD), row(D_HEAD), row(D_HEAD), row(LANES)],
        out_specs=qspec,
        scratch_shapes=[pltpu.VMEM((tq, 1), F32), pltpu.VMEM((tq, 1), F32), pltpu.VMEM((tq, LANES), F32),
                        pltpu.VMEM((tq, 1), F32), pltpu.VMEM((tq, 1), F32), pltpu.VMEM((tq, LANES), F32)],
    )
    return pl.pallas_call(
        functools.partial(_attn_prompt_kernel, tq=tq, tk=tk, lam_init=lam_init),
        grid_spec=grid_spec,
        out_shape=jax.ShapeDtypeStruct(q.shape, BF16),
        compiler_params=_cparams("parallel", "parallel", "arbitrary"),
        name="attn_prompt",
    )(qi_tab, ki_tab, q, k, v, *lams, subln_g)


def _attn_sample_kernel(pt_ref, q_ref, kn_ref, vn_ref, *refs, npg, lam_init):
    k_refs = refs[:npg]
    v_refs = refs[npg:2 * npg]
    bd_ref, lq1, lk1, lq2, lk2, sg_ref, o_ref, m_ref, l_ref, alo_ref, asw_ref = refs[2 * npg:]
    g = pl.program_id(1)
    q = q_ref[0]
    bd = bd_ref[...]

    def seg_scores(prod2d):
        return jnp.dot(prod2d.astype(BF16), bd, preferred_element_type=F32)

    @pl.when(g == 0)
    def _():
        vn = vn_ref[0]
        m_ref[...] = seg_scores(q * kn_ref[0])
        l_ref[...] = jnp.ones_like(l_ref)
        alo_ref[...] = vn
        asw_ref[...] = pltpu.roll(vn, D_HEAD, 1)

    for r in range(npg):
        kp = k_refs[r][...]
        vp = v_refs[r][...]
        sc = seg_scores((kp * q[None]).reshape(PAGE_SIZE * H_C, LANES)).reshape(PAGE_SIZE, H_C, LANES)
        m_old = m_ref[...]
        m_new = jnp.maximum(m_old, jnp.max(sc, axis=0))
        alpha = jnp.exp(m_old - m_new)
        p = jnp.exp(sc - m_new[None])
        l_ref[...] = alpha * l_ref[...] + jnp.sum(p, axis=0)
        alo_ref[...] = alpha * alo_ref[...] + jnp.sum(p * vp, axis=0)
        asw_ref[...] = alpha * asw_ref[...] + jnp.sum(p * pltpu.roll(vp, D_HEAD, 2), axis=0)
        m_ref[...] = m_new

    @pl.when(g == pl.num_programs(1) - 1)
    def _():
        lane = lax.broadcasted_iota(jnp.int32, (H_C, LANES), 1)
        lo = lane < D_HEAD
        a = alo_ref[...] / l_ref[...]
        bsw = pltpu.roll(asw_ref[...] / l_ref[...], D_HEAD, 1)
        o1 = jnp.where(lo, a, bsw)
        o2 = jnp.where(lo, bsw, a)
        lam = _lam_full(lq1[...], lk1[...], lq2[...], lk2[...], lam_init)
        o_ref[0] = (_rms(o1 - lam * o2, sg_ref[...], SUBLN_EPS) * (1.0 - lam_init)).astype(o_ref.dtype)


def _attn_sample(q, k_new, v_new, cache_k, cache_v, page_table, layer, bd, lams, subln_g, lam_init, *, npg):
    nb = q.shape[0]
    n_pages = page_table.shape[1]
    hspec = pl.BlockSpec((1, H_C, LANES), lambda b, g, pt: (b, 0, 0))
    row = lambda n: pl.BlockSpec((1, n), lambda b, g, pt: (0, 0))

    def page_spec(r):
        return pl.BlockSpec((None, None, PAGE_SIZE, H_C, LANES),
                            lambda b, g, pt: (layer, pt[b, g * npg + r], 0, 0, 0))

    grid_spec = pltpu.PrefetchScalarGridSpec(
        num_scalar_prefetch=1,
        grid=(nb, n_pages // npg),
        in_specs=[hspec, hspec, hspec] + [page_spec(r) for r in range(npg)] * 2
                 + [pl.BlockSpec((LANES, LANES), lambda b, g, pt: (0, 0)),
                    row(D_HEAD), row(D_HEAD), row(D_HEAD), row(D_HEAD), row(LANES)],
        out_specs=hspec,
        scratch_shapes=[pltpu.VMEM((H_C, LANES), F32)] * 4,
    )
    return pl.pallas_call(
        functools.partial(_attn_sample_kernel, npg=npg, lam_init=lam_init),
        grid_spec=grid_spec,
        out_shape=jax.ShapeDtypeStruct((nb, H_C, LANES), BF16),
        compiler_params=_cparams("parallel", "arbitrary"),
        name="attn_sample",
    )(page_table, q, k_new, v_new, *([cache_k] * npg), *([cache_v] * npg), bd, *lams, subln_g)


def _merge_kernel(x_ref, oa_ref, ob_ref, oc_ref, cg_ref, wa_ref, wb_ref, wc_ref, wo_ref, xo_ref):
    cg = cg_ref[...]
    dot = lambda a, w: jnp.dot(a, w, preferred_element_type=F32)
    m = (jax.nn.sigmoid(cg[:, 0:D_MODEL]) * dot(oa_ref[...], wa_ref[...])
         + jax.nn.sigmoid(cg[:, D_MODEL:2 * D_MODEL]) * dot(ob_ref[...], wb_ref[...])
         + jax.nn.sigmoid(cg[:, 2 * D_MODEL:3 * D_MODEL]) * dot(oc_ref[...], wc_ref[...]))
    xo_ref[...] = x_ref[...] + dot(m.astype(BF16), wo_ref[...])


def _merge(x, oa, ob, oc, cg, wa, wb, wc, wo, *, tm):
    m = x.shape[0]
    rows = lambda n: pl.BlockSpec((tm, n), lambda i: (i, 0))
    full = lambda a: pl.BlockSpec(a.shape, lambda i: (0, 0))
    return pl.pallas_call(
        _merge_kernel,
        grid=(m // tm,),
        in_specs=[rows(D_MODEL), rows(D_A), rows(D_B), rows(D_C), rows(3 * D_MODEL),
                  full(wa), full(wb), full(wc), full(wo)],
        out_specs=rows(D_MODEL),
        out_shape=jax.ShapeDtypeStruct((m, D_MODEL), F32),
        compiler_params=_cparams("parallel"),
        name="merge",
    )(x, oa, ob, oc, cg, wa, wb, wc, wo)


def _ffn_kernel(x_ref, g_ref, wu_ref, wd_ref, gf_ref, o_ref, h_ref, acc_ref, *, final):
    j = pl.program_id(1)

    @pl.when(j == 0)
    def _():
        h_ref[...] = _rms(x_ref[...], g_ref[...], EPS).astype(BF16)
        acc_ref[...] = jnp.zeros_like(acc_ref)

    up = jnp.dot(h_ref[...], wu_ref[...], preferred_element_type=F32)
    act = jnp.square(jnp.maximum(up, 0.0)).astype(BF16)
    acc_ref[...] += jnp.dot(act, wd_ref[...], preferred_element_type=F32)

    @pl.when(j == pl.num_programs(1) - 1)
    def _():
        xn = x_ref[...] + acc_ref[...]
        o_ref[...] = _rms(xn, gf_ref[...], EPS) if final else xn


def _ffn(x, g, wu, wd, gf, *, final, tm, tf):
    m = x.shape[0]
    row = pl.BlockSpec((1, D_MODEL), lambda i, j: (0, 0))
    return pl.pallas_call(
        functools.partial(_ffn_kernel, final=final),
        grid=(m // tm, D_FF // tf),
        in_specs=[pl.BlockSpec((tm, D_MODEL), lambda i, j: (i, 0)), row,
                  pl.BlockSpec((D_MODEL, tf), lambda i, j: (0, j)),
                  pl.BlockSpec((tf, D_MODEL), lambda i, j: (j, 0)), row],
        out_specs=pl.BlockSpec((tm, D_MODEL), lambda i, j: (i, 0)),
        out_shape=jax.ShapeDtypeStruct((m, D_MODEL), F32),
        scratch_shapes=[pltpu.VMEM((tm, D_MODEL), BF16), pltpu.VMEM((tm, D_MODEL), F32)],
        compiler_params=_cparams("parallel", "arbitrary"),
        name="ffn",
    )(x, g, wu, wd, gf)


def _pad_b_cols(a):
    o = 3 * D_B
    z = lambda n: jnp.zeros(a.shape[:-1] + (n,), a.dtype)
    return jnp.concatenate([a[..., :o + LORA_W], z(LW_PAD - LORA_W),
                            a[..., o + LORA_W:o + LORA_W + LORA_A], z(LA_PAD - LORA_A),
                            a[..., o + LORA_W + LORA_A:], z(LG_PAD - LORA_G)], axis=-1)


def _unpad_b_cols(a):
    o = 3 * D_B
    return jnp.concatenate([a[..., :o + LORA_W], a[..., o + LW_PAD:o + LW_PAD + LORA_A],
                            a[..., o + LW_PAD + LA_PAD:o + LW_PAD + LA_PAD + LORA_G]], axis=-1)


def _pad_rows(a, n):
    return jnp.concatenate([a, jnp.zeros((n - a.shape[0],) + a.shape[1:], a.dtype)], axis=0)


def _ones_blockdiag(n, seg):
    i = np.arange(n) // seg
    return jnp.asarray((i[:, None] == i[None, :]).astype(np.float32), dtype=BF16)


def _layer_weights(l, p):
    w_in = p["w_in"][l]
    o_b = N_A_COLS
    o_q = o_b + N_B_COLS
    w = dict(
        norm_mix=p["norm_mix"][l].reshape(1, D_MODEL),
        w_a=w_in[:, :o_b].astype(BF16),
        w_b=_pad_b_cols(w_in[:, o_b:o_q]).astype(BF16),
        w_q=w_in[:, o_q:o_q + D_C].astype(BF16),
        w_k=w_in[:, o_q + D_C:o_q + 2 * D_C].astype(BF16),
        w_v=w_in[:, o_q + 2 * D_C:o_q + 3 * D_C].astype(BF16),
        w_g=w_in[:, o_q + 3 * D_C:].astype(BF16),
        lng=p["sgu_ln_g"][l].reshape(1, D_A), lnb=p["sgu_ln_b"][l].reshape(1, D_A),
        sgu_w=p["sgu_w"][l],
        sgu_bias_td=jnp.repeat(p["sgu_b"][l].T, D_A // G_A, axis=1),
        sgu_w0=jnp.repeat(p["sgu_w"][l][:, 0, 0], D_A // G_A).reshape(1, D_A),
        sgu_b0=jnp.repeat(p["sgu_b"][l][:, 0], D_A // G_A).reshape(1, D_A),
        rwkv=[_pad_b_cols(p["shift_mu"][l]).reshape(1, N_B_PAD),
              p["w0"][l].reshape(1, D_B), _pad_rows(p["w2"][l], LW_PAD).astype(BF16),
              p["a0"][l].reshape(1, D_B), _pad_rows(p["a2"][l], LA_PAD).astype(BF16),
              _pad_rows(p["g2"][l], LG_PAD).astype(BF16),
              p["k_k"][l].reshape(1, D_B), p["k_a"][l].reshape(1, D_B), p["r_k"][l].reshape(1, D_B),
              _ones_blockdiag(D_B, N_B)],
        lnx_g=p["lnx_g"][l].reshape(1, D_B), lnx_b=p["lnx_b"][l].reshape(1, D_B),
        lams=[p[n][l].reshape(1, D_HEAD) for n in ("lam_q1", "lam_k1", "lam_q2", "lam_k2")],
        subln_g=p["subln_g"][l].reshape(1, 2 * D_HEAD),
        w_br_a=p["w_br_a"][l].astype(BF16), w_br_b=p["w_br_b"][l].astype(BF16),
        w_br_c=p["w_br_c"][l].astype(BF16), w_out=p["w_out"][l].astype(BF16),
        norm_ffn=p["norm_ffn"][l].reshape(1, D_MODEL),
        w_up=p["w_up"][l].astype(BF16), w_down=p["w_down"][l].astype(BF16),
        norm_final=p["norm_final"].reshape(1, D_MODEL),
    )
    return w


def _project_all(x, w, rope_tabs, tm, q_dtype):
    pj = functools.partial(_proj, x, w["norm_mix"], tm=tm, tn=D_MODEL)
    (ca,) = pj(w["w_a"], out_dtypes=(F32,), name="proj_a")
    (cb,) = pj(w["w_b"], out_dtypes=(F32,), name="proj_b")
    (qb,) = pj(w["w_q"], out_dtypes=(q_dtype,), rope_tabs=rope_tabs, scale=1.0 / math.sqrt(D_HEAD), name="proj_q")
    kf, kb = pj(w["w_k"], out_dtypes=(F32, BF16), rope_tabs=rope_tabs, name="proj_k")
    vf, vb = pj(w["w_v"], out_dtypes=(F32, BF16), name="proj_v")
    (cg,) = pj(w["w_g"], out_dtypes=(F32,), name="proj_g")
    return ca, cb, qb, kf, kb, vf, vb, cg


def _prompt_layer(x, l, w, rope_tabs, nb, t, final):
    lam_init = 0.8 - 0.6 * math.exp(-0.3 * l)
    ca, cb, qb, kf, kb, vf, vb, cg = _project_all(x, w, rope_tabs, 1024, BF16)
    o_a = _sgu_prompt(ca, w["lng"], w["lnb"], w["sgu_w"], w["sgu_bias_td"], tm=512)
    prev0 = jnp.zeros((nb, 1, N_B_PAD), F32)
    *rw, last = _rwkv_pre(cb.reshape(nb, t, N_B_PAD), prev0, w["rwkv"], seq=True, tm=256)
    r, dec, k2, v, na, bb, g, rkv = rw
    y, s_fin = _rwkv_scan_seq(r, dec, k2, v, na, bb, tb=256)
    flat = lambda a: a.reshape(nb * t, D_B)
    o_b = _rwkv_post(flat(y), flat(rkv), flat(g), w["lnx_g"], w["lnx_b"], w["rwkv"][-1], tm=1024)
    b3 = lambda a: a.reshape(nb, t, D_C)
    o_c = _attn_prompt(b3(qb), b3(kb), b3(vb), w["lams"], w["subln_g"], lam_init, tq=512)
    x = _merge(x, o_a, o_b, o_c.reshape(nb * t, D_C), cg, w["w_br_a"], w["w_br_b"], w["w_br_c"], w["w_out"], tm=256)
    x = _ffn(x, w["norm_ffn"], w["w_up"], w["w_down"], w["norm_final"], final=final, tm=1024, tf=1024)
    return x, kf, vf, s_fin, _unpad_b_cols(last[:, 0, :])


def _sample_layer(x, l, w, rope_tabs, cache_k, cache_v, page_table, prev_shift, s0, final):
    nb = x.shape[0]
    lam_init = 0.8 - 0.6 * math.exp(-0.3 * l)
    ca, cb, qb, kf, kb, vf, vb, cg = _project_all(x, w, rope_tabs, nb, F32)
    o_a, va = _sgu_sample(ca, w["lng"], w["lnb"], w["sgu_w0"], w["sgu_b0"])
    r, dec, k2, v, na, bb, g, rkv = _rwkv_pre(cb, _pad_b_cols(prev_shift), w["rwkv"], seq=False, tm=nb)
    y, s_new = _rwkv_scan_rows(r, dec, k2, v, na, bb, s0, tb=32)
    o_b = _rwkv_post(y, rkv, g, w["lnx_g"], w["lnx_b"], w["rwkv"][-1], tm=nb)
    h3 = lambda a: a.reshape(nb, H_C, LANES)
    o_c = _attn_sample(h3(qb), h3(kf), h3(vf), cache_k, cache_v, page_table, l,
                       _ones_blockdiag(LANES, D_HEAD), w["lams"], w["subln_g"], lam_init, npg=4)
    x = _merge(x, o_a, o_b, o_c.reshape(nb, D_C), cg, w["w_br_a"], w["w_br_b"], w["w_br_c"], w["w_out"], tm=nb)
    x = _ffn(x, w["norm_ffn"], w["w_up"], w["w_down"], w["norm_final"], final=final, tm=nb, tf=1024)
    return x, kf, vf, s_new, _unpad_b_cols(cb), va


def kernel(x_prompt, x_sample, cache_k, cache_v, state_rwkv, state_shift, page_table, norm_mix, w_in, sgu_ln_g, sgu_ln_b, sgu_w, sgu_b, shift_mu, w0, w2, a0, a2, g2, k_k, k_a, r_k, lnx_g, lnx_b, lam_q1, lam_k1, lam_q2, lam_k2, subln_g, w_br_a, w_br_b, w_br_c, w_out, norm_ffn, w_up, w_down, norm_final):
    p = dict(norm_mix=norm_mix, w_in=w_in, sgu_ln_g=sgu_ln_g, sgu_ln_b=sgu_ln_b, sgu_w=sgu_w, sgu_b=sgu_b,
             shift_mu=shift_mu, w0=w0, w2=w2, a0=a0, a2=a2, g2=g2, k_k=k_k, k_a=k_a, r_k=r_k, lnx_g=lnx_g,
             lnx_b=lnx_b, lam_q1=lam_q1, lam_k1=lam_k1, lam_q2=lam_q2, lam_k2=lam_k2, subln_g=subln_g,
             w_br_a=w_br_a, w_br_b=w_br_b, w_br_c=w_br_c, w_out=w_out, norm_ffn=norm_ffn, w_up=w_up,
             w_down=w_down, norm_final=norm_final)
    depth = w_in.shape[0]
    bp, tp, _ = x_prompt.shape
    bs, ts, _ = x_sample.shape
    tabs_p = _rope_tables(jnp.arange(tp))
    tabs_s = _rope_tables(jnp.full((bs * ts,), PAST_LEN, jnp.int32))
    xp = x_prompt.reshape(bp * tp, D_MODEL)
    xs = x_sample.reshape(bs * ts, D_MODEL)
    outs_p, outs_s = [], []
    for l in range(depth):
        w = _layer_weights(l, p)
        final = l == depth - 1
        xp, kp, vp, sp, shp = _prompt_layer(xp, l, w, tabs_p, bp, tp, final)
        xs, ks, vs, ss, shs, va = _sample_layer(xs, l, w, tabs_s, cache_k, cache_v, page_table,
                                                state_shift[l], state_rwkv[l], final)
        outs_p.append((kp.reshape(bp, tp, H_C, 2 * D_HEAD), vp.reshape(bp, tp, H_C, 2 * D_HEAD), sp, shp))
        outs_s.append((ks.reshape(bs, ts, H_C, 2 * D_HEAD), vs.reshape(bs, ts, H_C, 2 * D_HEAD), ss, shs,
                       va.reshape(bs, ts, D_A)))
    stack = lambda outs, i: jnp.stack([o[i] for o in outs])
    return (xp.reshape(bp, tp, D_MODEL), xs.reshape(bs, ts, D_MODEL),
            stack(outs_p, 0), stack(outs_p, 1), stack(outs_p, 2), stack(outs_p, 3),
            stack(outs_s, 0), stack(outs_s, 1), stack(outs_s, 2), stack(outs_s, 3), stack(outs_s, 4))
```

```python
import functools
import math

import numpy as np
import jax
import jax.numpy as jnp
from jax import lax
from jax.experimental import pallas as pl
from jax.experimental.pallas import tpu as pltpu

F32 = jnp.float32
BF16 = jnp.bfloat16

D_MODEL = 1024
PAST_LEN = 2048
PAGE_SIZE = 128
CHUNK = 128
D_A = 512
G_A = 8
H_B = 8
N_B = 64
D_B = H_B * N_B
LORA_W = 64
LORA_A = 64
LORA_G = 160
GN_EPS_B = 64e-5
H_C = 8
D_HEAD = 64
D_C = H_C * 2 * D_HEAD
ROPE_DIM = D_HEAD // 4
ROPE_THETA = 500000.0
SUBLN_EPS = 1e-5
D_FF = 4 * D_MODEL
EPS = 1e-6
N_A_COLS = 2 * D_A
N_B_COLS = 3 * D_B + LORA_W + LORA_A + LORA_G
LANES = 128
SUBLANES = 8
LW_PAD = 128
LA_PAD = 128
LG_PAD = 256
N_B_PAD = 3 * D_B + LW_PAD + LA_PAD + LG_PAD
VMEM_LIMIT = 52 * 1024 * 1024


def _cparams(*sem):
    return pltpu.CompilerParams(dimension_semantics=sem, vmem_limit_bytes=VMEM_LIMIT)


def _rms(x, g, eps):
    return x * lax.rsqrt(jnp.mean(x * x, axis=-1, keepdims=True) + eps) * g


def _proj_kernel(*refs, rope, scale, n_out):
    x_ref, g_ref, w_ref = refs[:3]
    pos = 3
    if rope:
        c_ref, s1_ref, s2_ref = refs[3:6]
        pos = 6
    out_refs = refs[pos:pos + n_out]
    h_ref = refs[pos + n_out]

    @pl.when(pl.program_id(1) == 0)
    def _():
        h_ref[...] = _rms(x_ref[...], g_ref[...], EPS).astype(BF16)

    acc = jnp.dot(h_ref[...], w_ref[...], preferred_element_type=F32)
    if rope:
        c, s1, s2 = c_ref[...], s1_ref[...], s2_ref[...]
        parts = []
        for hh in range(acc.shape[1] // LANES):
            a = acc[:, hh * LANES:(hh + 1) * LANES]
            parts.append(a * c + pltpu.roll(a, ROPE_DIM // 2, 1) * s1
                         + pltpu.roll(a, LANES - ROPE_DIM // 2, 1) * s2)
        acc = jnp.concatenate(parts, axis=1)
    if scale != 1.0:
        acc = acc * scale
    for o_ref in out_refs:
        if len(o_ref.shape) == 3:
            o_ref[0] = acc.T.astype(o_ref.dtype)
        else:
            o_ref[...] = acc.astype(o_ref.dtype)


def _proj(x, g, w, *, tm, tn, out_dtypes, rope_tabs=None, scale=1.0, t_seq=None, name="proj"):
    m, d = x.shape
    n = w.shape[1]
    in_specs = [pl.BlockSpec((tm, d), lambda i, j: (i, 0)),
                pl.BlockSpec((1, d), lambda i, j: (0, 0)),
                pl.BlockSpec((d, tn), lambda i, j: (0, j))]
    args = [x, g, w]
    if rope_tabs is not None:
        nt = rope_tabs[0].shape[0] // tm
        in_specs += [pl.BlockSpec((tm, LANES), lambda i, j: (i % nt, 0))] * 3
        args += list(rope_tabs)
    out_specs = [pl.BlockSpec((tm, tn), lambda i, j: (i, j)) for _ in out_dtypes]
    out_shape = [jax.ShapeDtypeStruct((m, n), dt) for dt in out_dtypes]
    if t_seq is not None:
        nts = t_seq // tm
        out_specs[-1] = pl.BlockSpec((1, tn, tm), lambda i, j: (i // nts, j, i % nts))
        out_shape[-1] = jax.ShapeDtypeStruct((m // t_seq, n, t_seq), out_dtypes[-1])
    outs = pl.pallas_call(
        functools.partial(_proj_kernel, rope=rope_tabs is not None, scale=scale, n_out=len(out_dtypes)),
        grid=(m // tm, n // tn),
        in_specs=in_specs,
        out_specs=out_specs,
        out_shape=out_shape,
        scratch_shapes=[pltpu.VMEM((tm, d), BF16)],
        compiler_params=_cparams("parallel", "arbitrary"),
        name=name,
    )(*args)
    return outs


def _rope_kernel(pos_ref, invf_ref, c_ref, s1_ref, s2_ref):
    ang = pos_ref[...] * invf_ref[...]
    lane = lax.broadcasted_iota(jnp.int32, ang.shape, 1) % D_HEAD
    first = lane < ROPE_DIM // 2
    second = (lane >= ROPE_DIM // 2) & (lane < ROPE_DIM)
    cos, sin = jnp.cos(ang), jnp.sin(ang)
    c_ref[...] = jnp.where(first | second, cos, 1.0)
    s1_ref[...] = jnp.where(second, sin, 0.0)
    s2_ref[...] = jnp.where(first, -sin, 0.0)


def _rope_tables(pos):
    t = pos.shape[0]
    half = ROPE_DIM // 2
    inv_freq = ROPE_THETA ** (-jnp.arange(half, dtype=F32) / half)
    blk = jnp.concatenate([inv_freq, inv_freq, jnp.zeros((D_HEAD - ROPE_DIM,), F32)])
    invf = jnp.concatenate([blk, blk]).reshape(1, LANES)
    tm = min(t, 512)
    return pl.pallas_call(
        _rope_kernel,
        grid=(t // tm,),
        in_specs=[pl.BlockSpec((tm, 1), lambda i: (i, 0)), pl.BlockSpec((1, LANES), lambda i: (0, 0))],
        out_specs=[pl.BlockSpec((tm, LANES), lambda i: (i, 0))] * 3,
        out_shape=[jax.ShapeDtypeStruct((t, LANES), F32)] * 3,
        compiler_params=_cparams("parallel"),
        name="rope_tables",
    )(pos.astype(F32).reshape(t, 1), invf)


def _gelu_ln(ca, lng, lnb):
    gx = 0.5 * ca * (1.0 + lax.erf(ca * math.sqrt(0.5)))
    u, v = gx[:, :D_A], gx[:, D_A:]
    d = v - jnp.mean(v, axis=-1, keepdims=True)
    va = d * lax.rsqrt(jnp.mean(d * d, axis=-1, keepdims=True) + EPS) * lng + lnb
    return u, va


def _sgu_prompt_kernel(ca_ref, lng_ref, lnb_ref, w_ref, b_ref, oa_ref, *, nchunk):
    u, va = _gelu_ln(ca_ref[...], lng_ref[...], lnb_ref[...])
    vab = va.astype(BF16)
    row = lax.broadcasted_iota(jnp.int32, (CHUNK, CHUNK), 0)
    col = lax.broadcasted_iota(jnp.int32, (CHUNK, CHUNK), 1)
    wcat = jnp.concatenate([jnp.where(col <= row, w_ref[g], 0.0).astype(BF16) for g in range(G_A)], axis=1)
    grp = lax.broadcasted_iota(jnp.int32, (CHUNK, D_A), 1) // (D_A // G_A)
    for ci in range(nchunk):
        vc = vab[ci * CHUNK:(ci + 1) * CHUNK]
        vbig = jnp.concatenate([jnp.where(grp == g, vc, jnp.zeros_like(vc)) for g in range(G_A)], axis=0)
        s = jnp.dot(wcat, vbig, preferred_element_type=F32) + b_ref[...]
        oa_ref[ci * CHUNK:(ci + 1) * CHUNK, :] = (u[ci * CHUNK:(ci + 1) * CHUNK] * s).astype(oa_ref.dtype)


def _sgu_prompt(ca, lng, lnb, w, bias_td, *, tm):
    m = ca.shape[0]
    return pl.pallas_call(
        functools.partial(_sgu_prompt_kernel, nchunk=tm // CHUNK),
        grid=(m // tm,),
        in_specs=[pl.BlockSpec((tm, N_A_COLS), lambda i: (i, 0)),
                  pl.BlockSpec((1, D_A), lambda i: (0, 0)),
                  pl.BlockSpec((1, D_A), lambda i: (0, 0)),
                  pl.BlockSpec((G_A, CHUNK, CHUNK), lambda i: (0, 0, 0)),
                  pl.BlockSpec((CHUNK, D_A), lambda i: (0, 0))],
        out_specs=pl.BlockSpec((tm, D_A), lambda i: (i, 0)),
        out_shape=jax.ShapeDtypeStruct((m, D_A), BF16),
        compiler_params=_cparams("parallel"),
        name="sgu_prompt",
    )(ca, lng, lnb, w, bias_td)


def _sgu_sample_kernel(ca_ref, lng_ref, lnb_ref, w0_ref, b0_ref, oa_ref, va_ref):
    u, va = _gelu_ln(ca_ref[...], lng_ref[...], lnb_ref[...])
    va_ref[...] = va
    oa_ref[...] = (u * (va * w0_ref[...] + b0_ref[...])).astype(oa_ref.dtype)


def _sgu_sample(ca, lng, lnb, w0, b0):
    m = ca.shape[0]
    row = lambda n: pl.BlockSpec((1, n), lambda i: (0, 0))
    return pl.pallas_call(
        _sgu_sample_kernel,
        grid=(1,),
        in_specs=[pl.BlockSpec((m, N_A_COLS), lambda i: (0, 0)), row(D_A), row(D_A), row(D_A), row(D_A)],
        out_specs=[pl.BlockSpec((m, D_A), lambda i: (0, 0))] * 2,
        out_shape=[jax.ShapeDtypeStruct((m, D_A), BF16), jax.ShapeDtypeStruct((m, D_A), F32)],
        compiler_params=_cparams("arbitrary"),
        name="sgu_sample",
    )(ca, lng, lnb, w0, b0)


def _seg_sum(x, ones_bd):
    hi = x.astype(BF16)
    lo = (x - hi.astype(F32)).astype(BF16)
    return (jnp.dot(hi, ones_bd, preferred_element_type=F32)
            + jnp.dot(lo, ones_bd, preferred_element_type=F32))


def _rwkv_pre_kernel(*refs, seq):
    if seq:
        c_ref, prev_ref = refs[:2]
    else:
        c_ref, sh_ref = refs[:2]
    (mu_ref, w0_ref, w2_ref, a0_ref, a2_ref, g2_ref, kk_ref, ka_ref, rk_ref, bd_ref) = refs[2:12]
    (r_o, w_o, k_o, v_o, a_o, b_o, g_o, rkv_o) = refs[12:20]
    if seq:
        last_o, carry_ref = refs[20:22]
        cols = c_ref[0]
        tm = cols.shape[0]

        @pl.when(pl.program_id(1) == 0)
        def _():
            carry_ref[...] = prev_ref[0]

        rolled = pltpu.roll(cols, 1, 0)
        first = lax.broadcasted_iota(jnp.int32, cols.shape, 0) == 0
        shifted = jnp.where(first, carry_ref[...], rolled)
        carry_ref[...] = cols[tm - 1:tm, :]
        last_o[0] = cols[tm - 1:tm, :]
    else:
        cols = c_ref[...]
        shifted = sh_ref[...]
    xs = cols + (shifted - cols) * mu_ref[...]
    r = xs[:, 0:D_B]
    k = xs[:, D_B:2 * D_B]
    v = xs[:, 2 * D_B:3 * D_B]
    o = 3 * D_B
    wl = xs[:, o:o + LW_PAD]
    al = xs[:, o + LW_PAD:o + LW_PAD + LA_PAD]
    gl = xs[:, o + LW_PAD + LA_PAD:o + LW_PAD + LA_PAD + LG_PAD]
    z = -(w0_ref[...] + jnp.dot(jnp.tanh(wl).astype(BF16), w2_ref[...], preferred_element_type=F32))
    softplus = jnp.maximum(z, 0.0) + jnp.log1p(jnp.exp(-jnp.abs(z)))
    log_decay = -jnp.exp(-softplus - 0.5)
    decay = log_decay if seq else jnp.exp(log_decay)
    a = jax.nn.sigmoid(a0_ref[...] + jnp.dot(al.astype(BF16), a2_ref[...], preferred_element_type=F32))
    g = jnp.dot(jax.nn.sigmoid(gl).astype(BF16), g2_ref[...], preferred_element_type=F32)
    bd = bd_ref[...]
    kk = k * kk_ref[...]
    kk = kk / jnp.maximum(jnp.sqrt(_seg_sum(kk * kk, bd)), 1e-12)
    k2 = k * (1.0 + (a - 1.0) * ka_ref[...])
    if seq:
        r_o[0], w_o[0], k_o[0], v_o[0], a_o[0], b_o[0], g_o[0] = r, decay, k2, v, -kk, kk * a, g
        rkv_o[0] = _seg_sum(r * k2 * rk_ref[...], bd) * v
    else:
        r_o[...], w_o[...], k_o[...], v_o[...], a_o[...], b_o[...], g_o[...] = r, decay, k2, v, -kk, kk * a, g
        rkv_o[...] = _seg_sum(r * k2 * rk_ref[...], bd) * v


def _rwkv_pre(cols, shift_src, params, *, seq, tm):
    prm_specs = []
    for p in params:
        prm_specs.append(pl.BlockSpec(p.shape, (lambda b, i: (0, 0)) if seq else (lambda i: (0, 0))))
    if seq:
        nb, t, _ = cols.shape
        grid = (nb, t // tm)
        in_specs = [pl.BlockSpec((1, tm, N_B_PAD), lambda b, i: (b, i, 0)),
                    pl.BlockSpec((1, 1, N_B_PAD), lambda b, i: (b, 0, 0))] + prm_specs
        ospec = pl.BlockSpec((1, tm, D_B), lambda b, i: (b, i, 0))
        oshape = jax.ShapeDtypeStruct((nb, t, D_B), F32)
        out_specs = [ospec] * 8 + [pl.BlockSpec((1, 1, N_B_PAD), lambda b, i: (b, 0, 0))]
        out_shape = [oshape] * 8 + [jax.ShapeDtypeStruct((nb, 1, N_B_PAD), F32)]
        scratch = [pltpu.VMEM((1, N_B_PAD), F32)]
        sem = ("parallel", "arbitrary")
    else:
        m = cols.shape[0]
        grid = (m // tm,)
        in_specs = [pl.BlockSpec((tm, N_B_PAD), lambda i: (i, 0))] * 2 + prm_specs
        out_specs = [pl.BlockSpec((tm, D_B), lambda i: (i, 0))] * 8
        out_shape = [jax.ShapeDtypeStruct((m, D_B), F32)] * 8
        scratch = []
        sem = ("parallel",)
    return pl.pallas_call(
        functools.partial(_rwkv_pre_kernel, seq=seq),
        grid=grid, in_specs=in_specs, out_specs=out_specs, out_shape=out_shape,
        scratch_shapes=scratch, compiler_params=_cparams(*sem),
        name="rwkv_pre_seq" if seq else "rwkv_pre_rows",
    )(cols, shift_src, *params)


def _scan_pair_step(S, a, w, b, k, v, r):
    lane = lax.broadcasted_iota(jnp.int32, (N_B, LANES), 1)
    row = lax.broadcasted_iota(jnp.int32, (N_B, LANES), 0)
    lo = lane < N_B
    e1 = lane == row
    e2 = lane == row + N_B

    def half_sums(p):
        t1 = jnp.sum(jnp.where(lo, p, 0.0), axis=1, keepdims=True)
        t2 = jnp.sum(jnp.where(lo, 0.0, p), axis=1, keepdims=True)
        return t1, t2

    t1, t2 = half_sums(S * a)
    sa = jnp.where(lo, t1, t2)
    vb = jnp.where(lo, jnp.sum(jnp.where(e1, v, 0.0), axis=1, keepdims=True),
                   jnp.sum(jnp.where(e2, v, 0.0), axis=1, keepdims=True))
    S = S * w + sa * b + vb * k
    y1, y2 = half_sums(S * r)
    y = jnp.sum(jnp.where(e1, y1, 0.0) + jnp.where(e2, y2, 0.0), axis=0, keepdims=True)
    return S, y


def _split(x):
    hi = x.astype(BF16)
    return hi, (x - hi.astype(F32)).astype(BF16)


def _mm3(xs, ys, nt=False):
    lhs = jnp.concatenate([xs[0], xs[0], xs[1]], axis=1)
    if nt:
        rhs = jnp.concatenate([ys[0], ys[1], ys[0]], axis=1)
        return lax.dot_general(lhs, rhs, (((1,), (1,)), ((), ())), preferred_element_type=F32)
    rhs = jnp.concatenate([ys[0], ys[1], ys[0]], axis=0)
    return jnp.dot(lhs, rhs, preferred_element_type=F32)


def _block_diag(x):
    lo = lax.broadcasted_iota(jnp.int32, x.shape, 1) < N_B
    zero = jnp.zeros_like(x)
    return jnp.concatenate([jnp.where(lo, x, zero), jnp.where(lo, zero, x)], axis=0)


def _rwkv_chunk_pair(z, lw, a, b, k, v, r):
    c = lw.shape[0]
    tri = (lax.broadcasted_iota(jnp.int32, (c, c), 0) >= lax.broadcasted_iota(jnp.int32, (c, c), 1)).astype(BF16)
    l1 = lw.astype(BF16)
    r1 = lw - l1.astype(F32)
    l2 = r1.astype(BF16)
    l3 = (r1 - l2.astype(F32)).astype(BF16)
    lam = jnp.dot(jnp.concatenate([tri, tri, tri], axis=1), jnp.concatenate([l1, l2, l3], axis=0),
                  preferred_element_type=F32)
    lam_c = lam[c - 1:c, :]
    w_inv = jnp.exp(-lam)
    w_rem = jnp.exp(lam_c - lam)
    a_h = _block_diag(a * jnp.exp(lam - lw))
    r_h = _block_diag(r * jnp.exp(lam))
    s_a, s_r = _split(a_h), _split(r_h)
    s_b, s_k = _split(_block_diag(b * w_inv)), _split(_block_diag(k * w_inv))
    s_v = _split(_block_diag(v))
    n = 2 * c
    ri = lax.broadcasted_iota(jnp.int32, (n, n), 0)
    ci = lax.broadcasted_iota(jnp.int32, (n, n), 1)
    strict, incl, eye = ci < ri, ci <= ri, ci == ri
    lab = jnp.where(strict, _mm3(s_a, s_b, nt=True), 0.0)
    lak = jnp.where(strict, _mm3(s_a, s_k, nt=True), 0.0)
    lrb = jnp.where(incl, _mm3(s_r, s_b, nt=True), 0.0)
    lrk = jnp.where(incl, _mm3(s_r, s_k, nt=True), 0.0)
    t = jnp.where(eye, 1.0, 0.0) + lab
    s_p = _split(lab)
    for _ in range(int(math.log2(c)) - 1):
        s_p = _split(_mm3(s_p, s_p))
        t = t + _mm3(_split(t), s_p)
    s_t = _split(t)
    s_at = _split(_mm3(s_t, s_a))
    s_u0 = _split(_mm3(s_t, _split(_mm3(_split(lak), s_v))))
    s_lrb = _split(lrb)
    rt = r_h + _mm3(s_lrb, s_at)
    y0 = _mm3(s_lrb, s_u0) + _mm3(_split(lrk), s_v)
    s_bt = _split(_block_diag(b * w_rem).T)
    s_kt = _split(_block_diag(k * w_rem).T)
    m = jnp.where(eye, jnp.exp(lam_c), 0.0) + _mm3(s_bt, s_at)
    nn = _mm3(s_bt, s_u0) + _mm3(s_kt, s_v)
    s_z = _split(z)
    ybd = _mm3(_split(rt), s_z) + y0
    return _mm3(_split(m), s_z) + nn, ybd[:c] + ybd[c:]


def _rwkv_scan_seq_kernel(r_ref, w_ref, k_ref, v_ref, a_ref, b_ref, y_ref, sf_ref, z_ref):
    npair = H_B // 2
    ci = pl.program_id(0)
    bi = pl.program_id(1)

    @pl.when(ci == 0)
    def _():
        for p in range(npair):
            z_ref[bi * npair + p] = jnp.zeros((LANES, LANES), F32)

    for p in range(npair):
        sl = (0, slice(None), slice(p * LANES, (p + 1) * LANES))
        z, y = _rwkv_chunk_pair(z_ref[bi * npair + p], w_ref[sl], a_ref[sl], b_ref[sl], k_ref[sl], v_ref[sl], r_ref[sl])
        z_ref[bi * npair + p] = z
        y_ref[sl] = y

    @pl.when(ci == pl.num_programs(0) - 1)
    def _():
        for p in range(npair):
            st = z_ref[bi * npair + p].T
            sf_ref[bi, 2 * p] = st[:N_B, :N_B]
            sf_ref[bi, 2 * p + 1] = st[N_B:, N_B:]


def _rwkv_scan_seq(r, lw, k, v, a, b, *, tc):
    nb, t, _ = r.shape
    spec = pl.BlockSpec((1, tc, D_B), lambda c, bb: (bb, c, 0))
    return pl.pallas_call(
        _rwkv_scan_seq_kernel,
        grid=(t // tc, nb),
        in_specs=[spec] * 6,
        out_specs=[spec, pl.BlockSpec((nb, H_B, N_B, N_B), lambda c, bb: (0, 0, 0, 0))],
        out_shape=[jax.ShapeDtypeStruct((nb, t, D_B), F32), jax.ShapeDtypeStruct((nb, H_B, N_B, N_B), F32)],
        scratch_shapes=[pltpu.VMEM((nb * H_B // 2, LANES, LANES), F32)],
        compiler_params=_cparams("arbitrary", "arbitrary"),
        name="rwkv_scan_seq",
    )(r, lw, k, v, a, b)


def _rwkv_scan_rows_kernel(r_ref, w_ref, k_ref, v_ref, a_ref, b_ref, s0_ref, y_ref, sf_ref, *, tb):
    def group(gi, carry):
        base = pl.multiple_of(gi * SUBLANES, SUBLANES)
        for p in range(H_B // 2):
            sl = (pl.ds(base, SUBLANES), slice(p * LANES, (p + 1) * LANES))
            a, w, b, k, v, r = (ref[sl] for ref in (a_ref, w_ref, b_ref, k_ref, v_ref, r_ref))
            ys = []
            for j in range(SUBLANES):
                row = slice(j, j + 1)
                S0 = jnp.concatenate([s0_ref[base + j, 2 * p], s0_ref[base + j, 2 * p + 1]], axis=1)
                S, y = _scan_pair_step(S0, a[row], w[row], b[row], k[row], v[row], r[row])
                sf_ref[base + j, 2 * p] = S[:, :N_B]
                sf_ref[base + j, 2 * p + 1] = S[:, N_B:]
                ys.append(y)
            y_ref[sl] = jnp.concatenate(ys, axis=0)
        return carry

    lax.fori_loop(0, tb // SUBLANES, group, 0)


def _rwkv_scan_rows(r, w, k, v, a, b, s0, *, tb):
    m = r.shape[0]
    spec = pl.BlockSpec((tb, D_B), lambda i: (i, 0))
    sspec = pl.BlockSpec((tb, H_B, N_B, N_B), lambda i: (i, 0, 0, 0))
    return pl.pallas_call(
        functools.partial(_rwkv_scan_rows_kernel, tb=tb),
        grid=(m // tb,),
        in_specs=[spec] * 6 + [sspec],
        out_specs=[spec, sspec],
        out_shape=[jax.ShapeDtypeStruct((m, D_B), F32), jax.ShapeDtypeStruct((m, H_B, N_B, N_B), F32)],
        compiler_params=_cparams("parallel"),
        name="rwkv_scan_rows",
    )(r, w, k, v, a, b, s0)


def _rwkv_post_kernel(y_ref, rkv_ref, g_ref, lg_ref, lb_ref, bd_ref, o_ref):
    y = y_ref[...]
    bd = bd_ref[...]
    d = y - _seg_sum(y, bd) * (1.0 / N_B)
    var = _seg_sum(d * d, bd) * (1.0 / N_B)
    yn = d * lax.rsqrt(var + GN_EPS_B) * lg_ref[...] + lb_ref[...]
    o_ref[...] = ((yn + rkv_ref[...]) * g_ref[...]).astype(o_ref.dtype)


def _rwkv_post(y, rkv, g, lg, lb, bd, *, tm):
    m = y.shape[0]
    spec = pl.BlockSpec((tm, D_B), lambda i: (i, 0))
    row = pl.BlockSpec((1, D_B), lambda i: (0, 0))
    return pl.pallas_call(
        _rwkv_post_kernel,
        grid=(m // tm,),
        in_specs=[spec, spec, spec, row, row, pl.BlockSpec((D_B, D_B), lambda i: (0, 0))],
        out_specs=spec,
        out_shape=jax.ShapeDtypeStruct((m, D_B), BF16),
        compiler_params=_cparams("parallel"),
        name="rwkv_post",
    )(y, rkv, g, lg, lb, bd)


def _lam_full(lq1, lk1, lq2, lk2, lam_init):
    return (jnp.exp(jnp.sum(lq1 * lk1, axis=-1, keepdims=True))
            - jnp.exp(jnp.sum(lq2 * lk2, axis=-1, keepdims=True)) + lam_init)


def _attn_prompt_kernel(qi_ref, ki_ref, q_ref, k_ref, vt_ref, lq1, lk1, lq2, lk2, sg_ref, o_ref,
                        m1, l1, a1, m2, l2, a2, *, lam_init):
    s = pl.program_id(2)
    qi = qi_ref[s]
    ki = ki_ref[s]

    @pl.when(ki == 0)
    def _():
        for m_ref, l_ref, a_ref in ((m1, l1, a1), (m2, l2, a2)):
            m_ref[...] = jnp.full_like(m_ref, -jnp.inf)
            l_ref[...] = jnp.zeros_like(l_ref)
            a_ref[...] = jnp.zeros_like(a_ref)

    q = q_ref[0]
    k = k_ref[0]
    vt = vt_ref[0]
    lane = lax.broadcasted_iota(jnp.int32, q.shape, 1)
    zero = jnp.zeros_like(q)
    q_sub = (jnp.where(lane < D_HEAD, q, zero), jnp.where(lane < D_HEAD, zero, q))

    def update(masked):
        for qs, m_ref, l_ref, a_ref in ((q_sub[0], m1, l1, a1), (q_sub[1], m2, l2, a2)):
            st = lax.dot_general(k, qs, (((1,), (1,)), ((), ())), preferred_element_type=F32)
            if masked:
                kr = lax.broadcasted_iota(jnp.int32, st.shape, 0)
                qc = lax.broadcasted_iota(jnp.int32, st.shape, 1)
                st = jnp.where(kr <= qc, st, -jnp.inf)
            m_old = m_ref[...]
            m_new = jnp.maximum(m_old, jnp.max(st, axis=0, keepdims=True))
            alpha = jnp.exp(m_old - m_new)
            p = jnp.exp(st - m_new)
            l_ref[...] = alpha * l_ref[...] + jnp.sum(p, axis=0, keepdims=True)
            a_ref[...] = alpha * a_ref[...] + jnp.dot(vt, p.astype(BF16), preferred_element_type=F32)
            m_ref[...] = m_new

    @pl.when(ki < qi)
    def _():
        update(False)

    @pl.when(ki == qi)
    def _():
        update(True)
        lam = _lam_full(lq1[...], lk1[...], lq2[...], lk2[...], lam_init)
        ot = a1[...] / l1[...] - lam * (a2[...] / l2[...])
        on = ot * lax.rsqrt(jnp.mean(ot * ot, axis=0, keepdims=True) + SUBLN_EPS)
        o_ref[0] = (on.T * sg_ref[...] * (1.0 - lam_init)).astype(o_ref.dtype)


def _attn_prompt(q, k, vt, lams, subln_g, lam_init, *, tq):
    nb, t, _ = q.shape
    tk = tq
    nq = t // tq
    pairs = [(i, j) for i in range(nq) for j in range(i + 1)]
    qi_tab = jnp.asarray(np.array([p[0] for p in pairs], np.int32))
    ki_tab = jnp.asarray(np.array([p[1] for p in pairs], np.int32))
    qspec = pl.BlockSpec((1, tq, LANES), lambda b, h, s, qt, kt: (b, qt[s], h))
    kspec = pl.BlockSpec((1, tk, LANES), lambda b, h, s, qt, kt: (b, kt[s], h))
    vspec = pl.BlockSpec((1, LANES, tk), lambda b, h, s, qt, kt: (b, h, kt[s]))
    row = lambda n: pl.BlockSpec((1, n), lambda b, h, s, qt, kt: (0, 0))
    grid_spec = pltpu.PrefetchScalarGridSpec(
        num_scalar_prefetch=2,
        grid=(nb, H_C, len(pairs)),
        in_specs=[qspec, kspec, vspec, row(D_HEAD), row(D_HEAD), row(D_HEAD), row(D_HEAD), row(LANES)],
        out_specs=qspec,
        scratch_shapes=[pltpu.VMEM((1, tq), F32), pltpu.VMEM((1, tq), F32), pltpu.VMEM((LANES, tq), F32),
                        pltpu.VMEM((1, tq), F32), pltpu.VMEM((1, tq), F32), pltpu.VMEM((LANES, tq), F32)],
    )
    return pl.pallas_call(
        functools.partial(_attn_prompt_kernel, lam_init=lam_init),
        grid_spec=grid_spec,
        out_shape=jax.ShapeDtypeStruct(q.shape, BF16),
        compiler_params=_cparams("parallel", "parallel", "arbitrary"),
        name="attn_prompt",
    )(qi_tab, ki_tab, q, k, vt, *lams, subln_g)


def _attn_sample_kernel(pt_ref, q_ref, kn_ref, vn_ref, *refs, npg, lam_init):
    k_refs = refs[:npg]
    v_refs = refs[npg:2 * npg]
    bd_ref, lq1, lk1, lq2, lk2, sg_ref, o_ref, m_ref, l_ref, alo_ref, asw_ref = refs[2 * npg:]
    g = pl.program_id(1)
    q = q_ref[0]
    bd = bd_ref[...]

    def seg_scores(prod2d):
        return jnp.dot(prod2d.astype(BF16), bd, preferred_element_type=F32)

    @pl.when(g == 0)
    def _():
        vn = vn_ref[0]
        m_ref[...] = seg_scores(q * kn_ref[0])
        l_ref[...] = jnp.ones_like(l_ref)
        alo_ref[...] = vn
        asw_ref[...] = pltpu.roll(vn, D_HEAD, 1)

    for r in range(npg):
        kp = k_refs[r][...]
        vp = v_refs[r][...]
        sc = seg_scores((kp * q[None]).reshape(PAGE_SIZE * H_C, LANES)).reshape(PAGE_SIZE, H_C, LANES)
        m_old = m_ref[...]
        m_new = jnp.maximum(m_old, jnp.max(sc, axis=0))
        alpha = jnp.exp(m_old - m_new)
        p = jnp.exp(sc - m_new[None])
        l_ref[...] = alpha * l_ref[...] + jnp.sum(p, axis=0)
        alo_ref[...] = alpha * alo_ref[...] + jnp.sum(p * vp, axis=0)
        asw_ref[...] = alpha * asw_ref[...] + jnp.sum(p * pltpu.roll(vp, D_HEAD, 2), axis=0)
        m_ref[...] = m_new

    @pl.when(g == pl.num_programs(1) - 1)
    def _():
        lane = lax.broadcasted_iota(jnp.int32, (H_C, LANES), 1)
        lo = lane < D_HEAD
        a = alo_ref[...] / l_ref[...]
        bsw = pltpu.roll(asw_ref[...] / l_ref[...], D_HEAD, 1)
        o1 = jnp.where(lo, a, bsw)
        o2 = jnp.where(lo, bsw, a)
        lam = _lam_full(lq1[...], lk1[...], lq2[...], lk2[...], lam_init)
        o_ref[0] = (_rms(o1 - lam * o2, sg_ref[...], SUBLN_EPS) * (1.0 - lam_init)).astype(o_ref.dtype)


def _attn_sample(q, k_new, v_new, cache_k, cache_v, page_table, layer, bd, lams, subln_g, lam_init, *, npg):
    nb = q.shape[0]
    n_pages = page_table.shape[1]
    hspec = pl.BlockSpec((1, H_C, LANES), lambda b, g, pt: (b, 0, 0))
    row = lambda n: pl.BlockSpec((1, n), lambda b, g, pt: (0, 0))

    def page_spec(r):
        return pl.BlockSpec((None, None, PAGE_SIZE, H_C, LANES),
                            lambda b, g, pt: (layer, pt[b, g * npg + r], 0, 0, 0))

    grid_spec = pltpu.PrefetchScalarGridSpec(
        num_scalar_prefetch=1,
        grid=(nb, n_pages // npg),
        in_specs=[hspec, hspec, hspec] + [page_spec(r) for r in range(npg)] * 2
                 + [pl.BlockSpec((LANES, LANES), lambda b, g, pt: (0, 0)),
                    row(D_HEAD), row(D_HEAD), row(D_HEAD), row(D_HEAD), row(LANES)],
        out_specs=hspec,
        scratch_shapes=[pltpu.VMEM((H_C, LANES), F32)] * 4,
    )
    return pl.pallas_call(
        functools.partial(_attn_sample_kernel, npg=npg, lam_init=lam_init),
        grid_spec=grid_spec,
        out_shape=jax.ShapeDtypeStruct((nb, H_C, LANES), BF16),
        compiler_params=_cparams("parallel", "arbitrary"),
        name="attn_sample",
    )(page_table, q, k_new, v_new, *([cache_k] * npg), *([cache_v] * npg), bd, *lams, subln_g)


def _merge_kernel(x_ref, oa_ref, ob_ref, oc_ref, cg_ref, wa_ref, wb_ref, wc_ref, wo_ref, xo_ref):
    cg = cg_ref[...]
    dot = lambda a, w: jnp.dot(a, w, preferred_element_type=F32)
    m = (jax.nn.sigmoid(cg[:, 0:D_MODEL]) * dot(oa_ref[...], wa_ref[...])
         + jax.nn.sigmoid(cg[:, D_MODEL:2 * D_MODEL]) * dot(ob_ref[...], wb_ref[...])
         + jax.nn.sigmoid(cg[:, 2 * D_MODEL:3 * D_MODEL]) * dot(oc_ref[...], wc_ref[...]))
    xo_ref[...] = x_ref[...] + dot(m.astype(BF16), wo_ref[...])


def _merge(x, oa, ob, oc, cg, wa, wb, wc, wo, *, tm):
    m = x.shape[0]
    rows = lambda n: pl.BlockSpec((tm, n), lambda i: (i, 0))
    full = lambda a: pl.BlockSpec(a.shape, lambda i: (0, 0))
    return pl.pallas_call(
        _merge_kernel,
        grid=(m // tm,),
        in_specs=[rows(D_MODEL), rows(D_A), rows(D_B), rows(D_C), rows(3 * D_MODEL),
                  full(wa), full(wb), full(wc), full(wo)],
        out_specs=rows(D_MODEL),
        out_shape=jax.ShapeDtypeStruct((m, D_MODEL), F32),
        compiler_params=_cparams("parallel"),
        name="merge",
    )(x, oa, ob, oc, cg, wa, wb, wc, wo)


def _ffn_kernel(x_ref, g_ref, wu_ref, wd_ref, gf_ref, o_ref, h_ref, acc_ref, *, final):
    j = pl.program_id(1)

    @pl.when(j == 0)
    def _():
        h_ref[...] = _rms(x_ref[...], g_ref[...], EPS).astype(BF16)
        acc_ref[...] = jnp.zeros_like(acc_ref)

    up = jnp.dot(h_ref[...], wu_ref[...], preferred_element_type=F32)
    act = jnp.square(jnp.maximum(up, 0.0)).astype(BF16)
    acc_ref[...] += jnp.dot(act, wd_ref[...], preferred_element_type=F32)

    @pl.when(j == pl.num_programs(1) - 1)
    def _():
        xn = x_ref[...] + acc_ref[...]
        o_ref[...] = _rms(xn, gf_ref[...], EPS) if final else xn


def _ffn(x, g, wu, wd, gf, *, final, tm, tf):
    m = x.shape[0]
    row = pl.BlockSpec((1, D_MODEL), lambda i, j: (0, 0))
    return pl.pallas_call(
        functools.partial(_ffn_kernel, final=final),
        grid=(m // tm, D_FF // tf),
        in_specs=[pl.BlockSpec((tm, D_MODEL), lambda i, j: (i, 0)), row,
                  pl.BlockSpec((D_MODEL, tf), lambda i, j: (0, j)),
                  pl.BlockSpec((tf, D_MODEL), lambda i, j: (j, 0)), row],
        out_specs=pl.BlockSpec((tm, D_MODEL), lambda i, j: (i, 0)),
        out_shape=jax.ShapeDtypeStruct((m, D_MODEL), F32),
        scratch_shapes=[pltpu.VMEM((tm, D_MODEL), BF16), pltpu.VMEM((tm, D_MODEL), F32)],
        compiler_params=_cparams("parallel", "arbitrary"),
        name="ffn",
    )(x, g, wu, wd, gf)


def _pad_b_cols(a):
    o = 3 * D_B
    z = lambda n: jnp.zeros(a.shape[:-1] + (n,), a.dtype)
    return jnp.concatenate([a[..., :o + LORA_W], z(LW_PAD - LORA_W),
                            a[..., o + LORA_W:o + LORA_W + LORA_A], z(LA_PAD - LORA_A),
                            a[..., o + LORA_W + LORA_A:], z(LG_PAD - LORA_G)], axis=-1)


def _unpad_b_cols(a):
    o = 3 * D_B
    return jnp.concatenate([a[..., :o + LORA_W], a[..., o + LW_PAD:o + LW_PAD + LORA_A],
                            a[..., o + LW_PAD + LA_PAD:o + LW_PAD + LA_PAD + LORA_G]], axis=-1)


def _pad_rows(a, n):
    return jnp.concatenate([a, jnp.zeros((n - a.shape[0],) + a.shape[1:], a.dtype)], axis=0)


def _ones_blockdiag(n, seg):
    i = np.arange(n) // seg
    return jnp.asarray((i[:, None] == i[None, :]).astype(np.float32), dtype=BF16)


def _layer_weights(l, p):
    w_in = p["w_in"][l]
    o_b = N_A_COLS
    o_q = o_b + N_B_COLS
    w = dict(
        norm_mix=p["norm_mix"][l].reshape(1, D_MODEL),
        w_a=w_in[:, :o_b].astype(BF16),
        w_b=_pad_b_cols(w_in[:, o_b:o_q]).astype(BF16),
        w_q=w_in[:, o_q:o_q + D_C].astype(BF16),
        w_k=w_in[:, o_q + D_C:o_q + 2 * D_C].astype(BF16),
        w_v=w_in[:, o_q + 2 * D_C:o_q + 3 * D_C].astype(BF16),
        w_g=w_in[:, o_q + 3 * D_C:].astype(BF16),
        lng=p["sgu_ln_g"][l].reshape(1, D_A), lnb=p["sgu_ln_b"][l].reshape(1, D_A),
        sgu_w=p["sgu_w"][l],
        sgu_bias_td=jnp.repeat(p["sgu_b"][l].T, D_A // G_A, axis=1),
        sgu_w0=jnp.repeat(p["sgu_w"][l][:, 0, 0], D_A // G_A).reshape(1, D_A),
        sgu_b0=jnp.repeat(p["sgu_b"][l][:, 0], D_A // G_A).reshape(1, D_A),
        rwkv=[_pad_b_cols(p["shift_mu"][l]).reshape(1, N_B_PAD),
              p["w0"][l].reshape(1, D_B), _pad_rows(p["w2"][l], LW_PAD).astype(BF16),
              p["a0"][l].reshape(1, D_B), _pad_rows(p["a2"][l], LA_PAD).astype(BF16),
              _pad_rows(p["g2"][l], LG_PAD).astype(BF16),
              p["k_k"][l].reshape(1, D_B), p["k_a"][l].reshape(1, D_B), p["r_k"][l].reshape(1, D_B),
              _ones_blockdiag(D_B, N_B)],
        lnx_g=p["lnx_g"][l].reshape(1, D_B), lnx_b=p["lnx_b"][l].reshape(1, D_B),
        lams=[p[n][l].reshape(1, D_HEAD) for n in ("lam_q1", "lam_k1", "lam_q2", "lam_k2")],
        subln_g=p["subln_g"][l].reshape(1, 2 * D_HEAD),
        w_br_a=p["w_br_a"][l].astype(BF16), w_br_b=p["w_br_b"][l].astype(BF16),
        w_br_c=p["w_br_c"][l].astype(BF16), w_out=p["w_out"][l].astype(BF16),
        norm_ffn=p["norm_ffn"][l].reshape(1, D_MODEL),
        w_up=p["w_up"][l].astype(BF16), w_down=p["w_down"][l].astype(BF16),
        norm_final=p["norm_final"].reshape(1, D_MODEL),
    )
    return w


def _project_all(x, w, rope_tabs, tm, t_seq=None):
    pj = functools.partial(_proj, x, w["norm_mix"], tm=tm, tn=D_MODEL)
    qscale = 1.0 / math.sqrt(D_HEAD)
    (ca,) = pj(w["w_a"], out_dtypes=(F32,), name="proj_a")
    (cb,) = pj(w["w_b"], out_dtypes=(F32,), name="proj_b")
    (cg,) = pj(w["w_g"], out_dtypes=(F32,), name="proj_g")
    if t_seq is None:
        (q,) = pj(w["w_q"], out_dtypes=(F32,), rope_tabs=rope_tabs, scale=qscale, name="proj_q")
        (kf,) = pj(w["w_k"], out_dtypes=(F32,), rope_tabs=rope_tabs, name="proj_k")
        (vf,) = pj(w["w_v"], out_dtypes=(F32,), name="proj_v")
        return ca, cb, cg, q, kf, vf
    (q,) = pj(w["w_q"], out_dtypes=(BF16,), rope_tabs=rope_tabs, scale=qscale, name="proj_q")
    kf, kb = pj(w["w_k"], out_dtypes=(F32, BF16), rope_tabs=rope_tabs, name="proj_k")
    vf, vt = pj(w["w_v"], out_dtypes=(F32, BF16), t_seq=t_seq, name="proj_v")
    return ca, cb, cg, q, kf, vf, kb, vt


def _prompt_layer(x, l, w, rope_tabs, nb, t, final):
    lam_init = 0.8 - 0.6 * math.exp(-0.3 * l)
    ca, cb, cg, qb, kf, vf, kb, vt = _project_all(x, w, rope_tabs, 1024, t_seq=t)
    o_a = _sgu_prompt(ca, w["lng"], w["lnb"], w["sgu_w"], w["sgu_bias_td"], tm=512)
    prev0 = jnp.zeros((nb, 1, N_B_PAD), F32)
    *rw, last = _rwkv_pre(cb.reshape(nb, t, N_B_PAD), prev0, w["rwkv"], seq=True, tm=256)
    r, dec, k2, v, na, bb, g, rkv = rw
    y, s_fin = _rwkv_scan_seq(r, dec, k2, v, na, bb, tc=64)
    flat = lambda a: a.reshape(nb * t, D_B)
    o_b = _rwkv_post(flat(y), flat(rkv), flat(g), w["lnx_g"], w["lnx_b"], w["rwkv"][-1], tm=1024)
    b3 = lambda a: a.reshape(nb, t, D_C)
    o_c = _attn_prompt(b3(qb), b3(kb), vt, w["lams"], w["subln_g"], lam_init, tq=512)
    x = _merge(x, o_a, o_b, o_c.reshape(nb * t, D_C), cg, w["w_br_a"], w["w_br_b"], w["w_br_c"], w["w_out"], tm=256)
    x = _ffn(x, w["norm_ffn"], w["w_up"], w["w_down"], w["norm_final"], final=final, tm=1024, tf=1024)
    return x, kf, vf, s_fin, _unpad_b_cols(last[:, 0, :])


def _sample_layer(x, l, w, rope_tabs, cache_k, cache_v, page_table, prev_shift, s0, final):
    nb = x.shape[0]
    lam_init = 0.8 - 0.6 * math.exp(-0.3 * l)
    ca, cb, cg, qb, kf, vf = _project_all(x, w, rope_tabs, nb)
    o_a, va = _sgu_sample(ca, w["lng"], w["lnb"], w["sgu_w0"], w["sgu_b0"])
    r, dec, k2, v, na, bb, g, rkv = _rwkv_pre(cb, _pad_b_cols(prev_shift), w["rwkv"], seq=False, tm=nb)
    y, s_new = _rwkv_scan_rows(r, dec, k2, v, na, bb, s0, tb=32)
    o_b = _rwkv_post(y, rkv, g, w["lnx_g"], w["lnx_b"], w["rwkv"][-1], tm=nb)
    h3 = lambda a: a.reshape(nb, H_C, LANES)
    o_c = _attn_sample(h3(qb), h3(kf), h3(vf), cache_k, cache_v, page_table, l,
                       _ones_blockdiag(LANES, D_HEAD), w["lams"], w["subln_g"], lam_init, npg=4)
    x = _merge(x, o_a, o_b, o_c.reshape(nb, D_C), cg, w["w_br_a"], w["w_br_b"], w["w_br_c"], w["w_out"], tm=nb)
    x = _ffn(x, w["norm_ffn"], w["w_up"], w["w_down"], w["norm_final"], final=final, tm=nb, tf=1024)
    return x, kf, vf, s_new, _unpad_b_cols(cb), va


def kernel(x_prompt, x_sample, cache_k, cache_v, state_rwkv, state_shift, page_table, norm_mix, w_in, sgu_ln_g, sgu_ln_b, sgu_w, sgu_b, shift_mu, w0, w2, a0, a2, g2, k_k, k_a, r_k, lnx_g, lnx_b, lam_q1, lam_k1, lam_q2, lam_k2, subln_g, w_br_a, w_br_b, w_br_c, w_out, norm_ffn, w_up, w_down, norm_final):
    p = dict(norm_mix=norm_mix, w_in=w_in, sgu_ln_g=sgu_ln_g, sgu_ln_b=sgu_ln_b, sgu_w=sgu_w, sgu_b=sgu_b,
             shift_mu=shift_mu, w0=w0, w2=w2, a0=a0, a2=a2, g2=g2, k_k=k_k, k_a=k_a, r_k=r_k, lnx_g=lnx_g,
             lnx_b=lnx_b, lam_q1=lam_q1, lam_k1=lam_k1, lam_q2=lam_q2, lam_k2=lam_k2, subln_g=subln_g,
             w_br_a=w_br_a, w_br_b=w_br_b, w_br_c=w_br_c, w_out=w_out, norm_ffn=norm_ffn, w_up=w_up,
             w_down=w_down, norm_final=norm_final)
    depth = w_in.shape[0]
    bp, tp, _ = x_prompt.shape
    bs, ts, _ = x_sample.shape
    tabs_p = _rope_tables(jnp.arange(tp))
    tabs_s = _rope_tables(jnp.full((bs * ts,), PAST_LEN, jnp.int32))
    xp = x_prompt.reshape(bp * tp, D_MODEL)
    xs = x_sample.reshape(bs * ts, D_MODEL)
    outs_p, outs_s = [], []
    for l in range(depth):
        w = _layer_weights(l, p)
        final = l == depth - 1
        xp, kp, vp, sp, shp = _prompt_layer(xp, l, w, tabs_p, bp, tp, final)
        xs, ks, vs, ss, shs, va = _sample_layer(xs, l, w, tabs_s, cache_k, cache_v, page_table,
                                                state_shift[l], state_rwkv[l], final)
        outs_p.append((kp.reshape(bp, tp, H_C, 2 * D_HEAD), vp.reshape(bp, tp, H_C, 2 * D_HEAD), sp, shp))
        outs_s.append((ks.reshape(bs, ts, H_C, 2 * D_HEAD), vs.reshape(bs, ts, H_C, 2 * D_HEAD), ss, shs,
                       va.reshape(bs, ts, D_A)))
    stack = lambda outs, i: jnp.stack([o[i] for o in outs])
    return (xp.reshape(bp, tp, D_MODEL), xs.reshape(bs, ts, D_MODEL),
            stack(outs_p, 0), stack(outs_p, 1), stack(outs_p, 2), stack(outs_p, 3),
            stack(outs_s, 0), stack(outs_s, 1), stack(outs_s, 2), stack(outs_s, 3), stack(outs_s, 4))
```

```python
import functools
import math

import numpy as np
import jax
import jax.numpy as jnp
from jax import lax
from jax.experimental import pallas as pl
from jax.experimental.pallas import tpu as pltpu

F32 = jnp.float32
BF16 = jnp.bfloat16

D_MODEL = 1024
PAST_LEN = 2048
PAGE_SIZE = 128
CHUNK = 128
D_A = 512
G_A = 8
H_B = 8
N_B = 64
D_B = H_B * N_B
LORA_W = 64
LORA_A = 64
LORA_G = 160
GN_EPS_B = 64e-5
H_C = 8
D_HEAD = 64
D_C = H_C * 2 * D_HEAD
ROPE_DIM = D_HEAD // 4
ROPE_THETA = 500000.0
SUBLN_EPS = 1e-5
D_FF = 4 * D_MODEL
EPS = 1e-6
N_A_COLS = 2 * D_A
N_B_COLS = 3 * D_B + LORA_W + LORA_A + LORA_G
LANES = 128
SUBLANES = 8
LW_PAD = 128
LA_PAD = 128
LG_PAD = 256
N_B_PAD = 3 * D_B + LW_PAD + LA_PAD + LG_PAD
VMEM_LIMIT = 52 * 1024 * 1024
ATTN_BLOCK = 512


def _cparams(*sem):
    return pltpu.CompilerParams(dimension_semantics=sem, vmem_limit_bytes=VMEM_LIMIT)


def _rms(x, g, eps):
    return x * lax.rsqrt(jnp.mean(x * x, axis=-1, keepdims=True) + eps) * g


def _proj_kernel(*refs, rope, scale, n_out):
    x_ref, g_ref, w_ref = refs[:3]
    pos = 3
    if rope:
        c_ref, s1_ref, s2_ref = refs[3:6]
        pos = 6
    out_refs = refs[pos:pos + n_out]
    h_ref = refs[pos + n_out]

    @pl.when(pl.program_id(1) == 0)
    def _():
        h_ref[...] = _rms(x_ref[...], g_ref[...], EPS).astype(BF16)

    acc = jnp.dot(h_ref[...], w_ref[...], preferred_element_type=F32)
    if rope:
        c, s1, s2 = c_ref[...], s1_ref[...], s2_ref[...]
        parts = []
        for hh in range(acc.shape[1] // LANES):
            a = acc[:, hh * LANES:(hh + 1) * LANES]
            parts.append(a * c + pltpu.roll(a, ROPE_DIM // 2, 1) * s1
                         + pltpu.roll(a, LANES - ROPE_DIM // 2, 1) * s2)
        acc = jnp.concatenate(parts, axis=1)
    if scale != 1.0:
        acc = acc * scale
    for o_ref in out_refs:
        if len(o_ref.shape) == 4:
            o_ref[0, 0] = acc.T.astype(o_ref.dtype)
        else:
            o_ref[...] = acc.astype(o_ref.dtype)


def _proj(x, g, w, *, tm, tn, out_dtypes, rope_tabs=None, scale=1.0, t_seq=None, name="proj"):
    m, d = x.shape
    n = w.shape[1]
    in_specs = [pl.BlockSpec((tm, d), lambda i, j: (i, 0)),
                pl.BlockSpec((1, d), lambda i, j: (0, 0)),
                pl.BlockSpec((d, tn), lambda i, j: (0, j))]
    args = [x, g, w]
    if rope_tabs is not None:
        nt = rope_tabs[0].shape[0] // tm
        in_specs += [pl.BlockSpec((tm, LANES), lambda i, j: (i % nt, 0))] * 3
        args += list(rope_tabs)
    out_specs = [pl.BlockSpec((tm, tn), lambda i, j: (i, j)) for _ in out_dtypes]
    out_shape = [jax.ShapeDtypeStruct((m, n), dt) for dt in out_dtypes]
    if t_seq is not None:
        nts = t_seq // tm
        out_specs[-1] = pl.BlockSpec((1, 1, tn, tm), lambda i, j: (i // nts, i % nts, j, 0))
        out_shape[-1] = jax.ShapeDtypeStruct((m // t_seq, nts, n, tm), out_dtypes[-1])
    outs = pl.pallas_call(
        functools.partial(_proj_kernel, rope=rope_tabs is not None, scale=scale, n_out=len(out_dtypes)),
        grid=(m // tm, n // tn),
        in_specs=in_specs,
        out_specs=out_specs,
        out_shape=out_shape,
        scratch_shapes=[pltpu.VMEM((tm, d), BF16)],
        compiler_params=_cparams("parallel", "arbitrary"),
        name=name,
    )(*args)
    return outs


def _rope_kernel(pos_ref, invf_ref, c_ref, s1_ref, s2_ref):
    ang = pos_ref[...] * invf_ref[...]
    lane = lax.broadcasted_iota(jnp.int32, ang.shape, 1) % D_HEAD
    first = lane < ROPE_DIM // 2
    second = (lane >= ROPE_DIM // 2) & (lane < ROPE_DIM)
    cos, sin = jnp.cos(ang), jnp.sin(ang)
    c_ref[...] = jnp.where(first | second, cos, 1.0)
    s1_ref[...] = jnp.where(second, sin, 0.0)
    s2_ref[...] = jnp.where(first, -sin, 0.0)


def _rope_tables(pos):
    t = pos.shape[0]
    half = ROPE_DIM // 2
    inv_freq = ROPE_THETA ** (-jnp.arange(half, dtype=F32) / half)
    blk = jnp.concatenate([inv_freq, inv_freq, jnp.zeros((D_HEAD - ROPE_DIM,), F32)])
    invf = jnp.concatenate([blk, blk]).reshape(1, LANES)
    tm = min(t, 512)
    return pl.pallas_call(
        _rope_kernel,
        grid=(t // tm,),
        in_specs=[pl.BlockSpec((tm, 1), lambda i: (i, 0)), pl.BlockSpec((1, LANES), lambda i: (0, 0))],
        out_specs=[pl.BlockSpec((tm, LANES), lambda i: (i, 0))] * 3,
        out_shape=[jax.ShapeDtypeStruct((t, LANES), F32)] * 3,
        compiler_params=_cparams("parallel"),
        name="rope_tables",
    )(pos.astype(F32).reshape(t, 1), invf)


def _gelu_ln(ca, lng, lnb):
    gx = 0.5 * ca * (1.0 + lax.erf(ca * math.sqrt(0.5)))
    u, v = gx[:, :D_A], gx[:, D_A:]
    d = v - jnp.mean(v, axis=-1, keepdims=True)
    va = d * lax.rsqrt(jnp.mean(d * d, axis=-1, keepdims=True) + EPS) * lng + lnb
    return u, va


def _sgu_prompt_kernel(ca_ref, lng_ref, lnb_ref, w_ref, b_ref, oa_ref, *, nchunk):
    u, va = _gelu_ln(ca_ref[...], lng_ref[...], lnb_ref[...])
    vab = va.astype(BF16)
    row = lax.broadcasted_iota(jnp.int32, (CHUNK, CHUNK), 0)
    col = lax.broadcasted_iota(jnp.int32, (CHUNK, CHUNK), 1)
    wcat = jnp.concatenate([jnp.where(col <= row, w_ref[g], 0.0).astype(BF16) for g in range(G_A)], axis=1)
    grp = lax.broadcasted_iota(jnp.int32, (CHUNK, D_A), 1) // (D_A // G_A)
    for ci in range(nchunk):
        vc = vab[ci * CHUNK:(ci + 1) * CHUNK]
        vbig = jnp.concatenate([jnp.where(grp == g, vc, jnp.zeros_like(vc)) for g in range(G_A)], axis=0)
        s = jnp.dot(wcat, vbig, preferred_element_type=F32) + b_ref[...]
        oa_ref[ci * CHUNK:(ci + 1) * CHUNK, :] = (u[ci * CHUNK:(ci + 1) * CHUNK] * s).astype(oa_ref.dtype)


def _sgu_prompt(ca, lng, lnb, w, bias_td, *, tm):
    m = ca.shape[0]
    return pl.pallas_call(
        functools.partial(_sgu_prompt_kernel, nchunk=tm // CHUNK),
        grid=(m // tm,),
        in_specs=[pl.BlockSpec((tm, N_A_COLS), lambda i: (i, 0)),
                  pl.BlockSpec((1, D_A), lambda i: (0, 0)),
                  pl.BlockSpec((1, D_A), lambda i: (0, 0)),
                  pl.BlockSpec((G_A, CHUNK, CHUNK), lambda i: (0, 0, 0)),
                  pl.BlockSpec((CHUNK, D_A), lambda i: (0, 0))],
        out_specs=pl.BlockSpec((tm, D_A), lambda i: (i, 0)),
        out_shape=jax.ShapeDtypeStruct((m, D_A), BF16),
        compiler_params=_cparams("parallel"),
        name="sgu_prompt",
    )(ca, lng, lnb, w, bias_td)


def _sgu_sample_kernel(ca_ref, lng_ref, lnb_ref, w0_ref, b0_ref, oa_ref, va_ref):
    u, va = _gelu_ln(ca_ref[...], lng_ref[...], lnb_ref[...])
    va_ref[...] = va
    oa_ref[...] = (u * (va * w0_ref[...] + b0_ref[...])).astype(oa_ref.dtype)


def _sgu_sample(ca, lng, lnb, w0, b0):
    m = ca.shape[0]
    row = lambda n: pl.BlockSpec((1, n), lambda i: (0, 0))
    return pl.pallas_call(
        _sgu_sample_kernel,
        grid=(1,),
        in_specs=[pl.BlockSpec((m, N_A_COLS), lambda i: (0, 0)), row(D_A), row(D_A), row(D_A), row(D_A)],
        out_specs=[pl.BlockSpec((m, D_A), lambda i: (0, 0))] * 2,
        out_shape=[jax.ShapeDtypeStruct((m, D_A), BF16), jax.ShapeDtypeStruct((m, D_A), F32)],
        compiler_params=_cparams("arbitrary"),
        name="sgu_sample",
    )(ca, lng, lnb, w0, b0)


def _seg_sum(x, ones_bd):
    hi = x.astype(BF16)
    lo = (x - hi.astype(F32)).astype(BF16)
    return (jnp.dot(hi, ones_bd, preferred_element_type=F32)
            + jnp.dot(lo, ones_bd, preferred_element_type=F32))


def _rwkv_pre_kernel(*refs, seq):
    if seq:
        c_ref, prev_ref = refs[:2]
    else:
        c_ref, sh_ref = refs[:2]
    (mu_ref, w0_ref, w2_ref, a0_ref, a2_ref, g2_ref, kk_ref, ka_ref, rk_ref, bd_ref) = refs[2:12]
    (r_o, w_o, k_o, v_o, a_o, b_o, g_o, rkv_o) = refs[12:20]
    if seq:
        last_o, carry_ref = refs[20:22]
        cols = c_ref[0]
        tm = cols.shape[0]

        @pl.when(pl.program_id(1) == 0)
        def _():
            carry_ref[...] = prev_ref[0]

        rolled = pltpu.roll(cols, 1, 0)
        first = lax.broadcasted_iota(jnp.int32, cols.shape, 0) == 0
        shifted = jnp.where(first, carry_ref[...], rolled)
        carry_ref[...] = cols[tm - 1:tm, :]
        last_o[0] = cols[tm - 1:tm, :]
    else:
        cols = c_ref[...]
        shifted = sh_ref[...]
    xs = cols + (shifted - cols) * mu_ref[...]
    r = xs[:, 0:D_B]
    k = xs[:, D_B:2 * D_B]
    v = xs[:, 2 * D_B:3 * D_B]
    o = 3 * D_B
    wl = xs[:, o:o + LW_PAD]
    al = xs[:, o + LW_PAD:o + LW_PAD + LA_PAD]
    gl = xs[:, o + LW_PAD + LA_PAD:o + LW_PAD + LA_PAD + LG_PAD]
    z = -(w0_ref[...] + jnp.dot(jnp.tanh(wl).astype(BF16), w2_ref[...], preferred_element_type=F32))
    softplus = jnp.maximum(z, 0.0) + jnp.log1p(jnp.exp(-jnp.abs(z)))
    log_decay = -jnp.exp(-softplus - 0.5)
    decay = log_decay if seq else jnp.exp(log_decay)
    a = jax.nn.sigmoid(a0_ref[...] + jnp.dot(al.astype(BF16), a2_ref[...], preferred_element_type=F32))
    g = jnp.dot(jax.nn.sigmoid(gl).astype(BF16), g2_ref[...], preferred_element_type=F32)
    bd = bd_ref[...]
    kk = k * kk_ref[...]
    kk = kk / jnp.maximum(jnp.sqrt(_seg_sum(kk * kk, bd)), 1e-12)
    k2 = k * (1.0 + (a - 1.0) * ka_ref[...])
    if seq:
        r_o[0], w_o[0], k_o[0], v_o[0], a_o[0], b_o[0], g_o[0] = r, decay, k2, v, -kk, kk * a, g
        rkv_o[0] = _seg_sum(r * k2 * rk_ref[...], bd) * v
    else:
        r_o[...], w_o[...], k_o[...], v_o[...], a_o[...], b_o[...], g_o[...] = r, decay, k2, v, -kk, kk * a, g
        rkv_o[...] = _seg_sum(r * k2 * rk_ref[...], bd) * v


def _rwkv_pre(cols, shift_src, params, *, seq, tm):
    prm_specs = []
    for p in params:
        prm_specs.append(pl.BlockSpec(p.shape, (lambda b, i: (0, 0)) if seq else (lambda i: (0, 0))))
    if seq:
        nb, t, _ = cols.shape
        grid = (nb, t // tm)
        in_specs = [pl.BlockSpec((1, tm, N_B_PAD), lambda b, i: (b, i, 0)),
                    pl.BlockSpec((1, 1, N_B_PAD), lambda b, i: (b, 0, 0))] + prm_specs
        ospec = pl.BlockSpec((1, tm, D_B), lambda b, i: (b, i, 0))
        oshape = jax.ShapeDtypeStruct((nb, t, D_B), F32)
        out_specs = [ospec] * 8 + [pl.BlockSpec((1, 1, N_B_PAD), lambda b, i: (b, 0, 0))]
        out_shape = [oshape] * 8 + [jax.ShapeDtypeStruct((nb, 1, N_B_PAD), F32)]
        scratch = [pltpu.VMEM((1, N_B_PAD), F32)]
        sem = ("parallel", "arbitrary")
    else:
        m = cols.shape[0]
        grid = (m // tm,)
        in_specs = [pl.BlockSpec((tm, N_B_PAD), lambda i: (i, 0))] * 2 + prm_specs
        out_specs = [pl.BlockSpec((tm, D_B), lambda i: (i, 0))] * 8
        out_shape = [jax.ShapeDtypeStruct((m, D_B), F32)] * 8
        scratch = []
        sem = ("parallel",)
    return pl.pallas_call(
        functools.partial(_rwkv_pre_kernel, seq=seq),
        grid=grid, in_specs=in_specs, out_specs=out_specs, out_shape=out_shape,
        scratch_shapes=scratch, compiler_params=_cparams(*sem),
        name="rwkv_pre_seq" if seq else "rwkv_pre_rows",
    )(cols, shift_src, *params)


def _scan_pair_step(S, a, w, b, k, v, r):
    lane = lax.broadcasted_iota(jnp.int32, (N_B, LANES), 1)
    row = lax.broadcasted_iota(jnp.int32, (N_B, LANES), 0)
    lo = lane < N_B
    e1 = lane == row
    e2 = lane == row + N_B

    def half_sums(p):
        t1 = jnp.sum(jnp.where(lo, p, 0.0), axis=1, keepdims=True)
        t2 = jnp.sum(jnp.where(lo, 0.0, p), axis=1, keepdims=True)
        return t1, t2

    t1, t2 = half_sums(S * a)
    sa = jnp.where(lo, t1, t2)
    vb = jnp.where(lo, jnp.sum(jnp.where(e1, v, 0.0), axis=1, keepdims=True),
                   jnp.sum(jnp.where(e2, v, 0.0), axis=1, keepdims=True))
    S = S * w + sa * b + vb * k
    y1, y2 = half_sums(S * r)
    y = jnp.sum(jnp.where(e1, y1, 0.0) + jnp.where(e2, y2, 0.0), axis=0, keepdims=True)
    return S, y


def _split(x):
    hi = x.astype(BF16)
    return hi, (x - hi.astype(F32)).astype(BF16)


def _mm3(xs, ys, nt=False):
    lhs = jnp.concatenate([xs[0], xs[0], xs[1]], axis=1)
    if nt:
        rhs = jnp.concatenate([ys[0], ys[1], ys[0]], axis=1)
        return lax.dot_general(lhs, rhs, (((1,), (1,)), ((), ())), preferred_element_type=F32)
    rhs = jnp.concatenate([ys[0], ys[1], ys[0]], axis=0)
    return jnp.dot(lhs, rhs, preferred_element_type=F32)


def _block_diag(x):
    lo = lax.broadcasted_iota(jnp.int32, x.shape, 1) < N_B
    zero = jnp.zeros_like(x)
    return jnp.concatenate([jnp.where(lo, x, zero), jnp.where(lo, zero, x)], axis=0)


def _each(fn, *lists):
    return [fn(*args) for args in zip(*lists)]


def _rwkv_chunk_pairs(zs, lws, a_s, bs, ks, vs, rs):
    c = lws[0].shape[0]
    tri = (lax.broadcasted_iota(jnp.int32, (c, c), 0) >= lax.broadcasted_iota(jnp.int32, (c, c), 1)).astype(BF16)
    tri3 = jnp.concatenate([tri, tri, tri], axis=1)

    def cumsum(lw):
        l1 = lw.astype(BF16)
        r1 = lw - l1.astype(F32)
        l2 = r1.astype(BF16)
        l3 = (r1 - l2.astype(F32)).astype(BF16)
        return jnp.dot(tri3, jnp.concatenate([l1, l2, l3], axis=0), preferred_element_type=F32)

    lam = _each(cumsum, lws)
    lam_c = [x[c - 1:c, :] for x in lam]
    w_inv = [jnp.exp(-x) for x in lam]
    w_rem = _each(lambda lc, x: jnp.exp(lc - x), lam_c, lam)
    r_h = _each(lambda r, x: _block_diag(r * jnp.exp(x)), rs, lam)
    s_a = _each(lambda a, x, lw: _split(_block_diag(a * jnp.exp(x - lw))), a_s, lam, lws)
    s_r = _each(_split, r_h)
    s_b = _each(lambda b, wi: _split(_block_diag(b * wi)), bs, w_inv)
    s_k = _each(lambda k, wi: _split(_block_diag(k * wi)), ks, w_inv)
    s_v = [_split(_block_diag(v)) for v in vs]
    n = 2 * c
    ri = lax.broadcasted_iota(jnp.int32, (n, n), 0)
    ci = lax.broadcasted_iota(jnp.int32, (n, n), 1)
    strict, incl, eye = ci < ri, ci <= ri, ci == ri
    gram = lambda mask: (lambda x, y: jnp.where(mask, _mm3(x, y, nt=True), 0.0))
    lab = _each(gram(strict), s_a, s_b)
    lak = _each(gram(strict), s_a, s_k)
    lrb = _each(gram(incl), s_r, s_b)
    lrk = _each(gram(incl), s_r, s_k)
    t = [jnp.where(eye, 1.0, 0.0) + x for x in lab]
    s_p = _each(_split, lab)
    for _ in range(int(math.log2(c)) - 1):
        s_p = _each(lambda p: _split(_mm3(p, p)), s_p)
        t = _each(lambda tt, p: tt + _mm3(_split(tt), p), t, s_p)
    s_t = _each(_split, t)
    s_at = _each(lambda tt, a: _split(_mm3(tt, a)), s_t, s_a)
    x2 = _each(lambda l, v: _split(_mm3(_split(l), v)), lak, s_v)
    s_u0 = _each(lambda tt, x: _split(_mm3(tt, x)), s_t, x2)
    s_lrb = _each(_split, lrb)
    rt = _each(lambda rh, l, at: rh + _mm3(l, at), r_h, s_lrb, s_at)
    y0 = _each(lambda l, u, lk, v: _mm3(l, u) + _mm3(_split(lk), v), s_lrb, s_u0, lrk, s_v)
    s_bt = _each(lambda b, wr: _split(_block_diag(b * wr).T), bs, w_rem)
    s_kt = _each(lambda k, wr: _split(_block_diag(k * wr).T), ks, w_rem)
    m = _each(lambda lc, bt, at: jnp.where(eye, jnp.exp(lc), 0.0) + _mm3(bt, at), lam_c, s_bt, s_at)
    nn = _each(lambda bt, u, kt, v: _mm3(bt, u) + _mm3(kt, v), s_bt, s_u0, s_kt, s_v)
    s_z = _each(_split, zs)
    ybd = _each(lambda r, z, y: _mm3(_split(r), z) + y, rt, s_z, y0)
    z_new = _each(lambda mm, z, x: _mm3(_split(mm), z) + x, m, s_z, nn)
    return z_new, [y[:c] + y[c:] for y in ybd]


def _rwkv_scan_seq_kernel(r_ref, w_ref, k_ref, v_ref, a_ref, b_ref, y_ref, sf_ref, z_ref):
    npair = H_B // 2
    ci = pl.program_id(0)
    bi = pl.program_id(1)

    @pl.when(ci == 0)
    def _():
        for p in range(npair):
            z_ref[bi * npair + p] = jnp.zeros((LANES, LANES), F32)

    sls = [(0, slice(None), slice(p * LANES, (p + 1) * LANES)) for p in range(npair)]
    rows = lambda ref: [ref[sl] for sl in sls]
    zs, ys = _rwkv_chunk_pairs([z_ref[bi * npair + p] for p in range(npair)], rows(w_ref), rows(a_ref),
                               rows(b_ref), rows(k_ref), rows(v_ref), rows(r_ref))
    for p in range(npair):
        z_ref[bi * npair + p] = zs[p]
        y_ref[sls[p]] = ys[p]

    @pl.when(ci == pl.num_programs(0) - 1)
    def _():
        for p in range(npair):
            st = z_ref[bi * npair + p].T
            sf_ref[bi, 2 * p] = st[:N_B, :N_B]
            sf_ref[bi, 2 * p + 1] = st[N_B:, N_B:]


def _rwkv_scan_seq(r, lw, k, v, a, b, *, tc):
    nb, t, _ = r.shape
    spec = pl.BlockSpec((1, tc, D_B), lambda c, bb: (bb, c, 0))
    return pl.pallas_call(
        _rwkv_scan_seq_kernel,
        grid=(t // tc, nb),
        in_specs=[spec] * 6,
        out_specs=[spec, pl.BlockSpec((nb, H_B, N_B, N_B), lambda c, bb: (0, 0, 0, 0))],
        out_shape=[jax.ShapeDtypeStruct((nb, t, D_B), F32), jax.ShapeDtypeStruct((nb, H_B, N_B, N_B), F32)],
        scratch_shapes=[pltpu.VMEM((nb * H_B // 2, LANES, LANES), F32)],
        compiler_params=_cparams("arbitrary", "arbitrary"),
        name="rwkv_scan_seq",
    )(r, lw, k, v, a, b)


def _rwkv_scan_rows_kernel(r_ref, w_ref, k_ref, v_ref, a_ref, b_ref, s0_ref, y_ref, sf_ref, *, tb):
    def group(gi, carry):
        base = pl.multiple_of(gi * SUBLANES, SUBLANES)
        for p in range(H_B // 2):
            sl = (pl.ds(base, SUBLANES), slice(p * LANES, (p + 1) * LANES))
            a, w, b, k, v, r = (ref[sl] for ref in (a_ref, w_ref, b_ref, k_ref, v_ref, r_ref))
            ys = []
            for j in range(SUBLANES):
                row = slice(j, j + 1)
                S0 = jnp.concatenate([s0_ref[base + j, 2 * p], s0_ref[base + j, 2 * p + 1]], axis=1)
                S, y = _scan_pair_step(S0, a[row], w[row], b[row], k[row], v[row], r[row])
                sf_ref[base + j, 2 * p] = S[:, :N_B]
                sf_ref[base + j, 2 * p + 1] = S[:, N_B:]
                ys.append(y)
            y_ref[sl] = jnp.concatenate(ys, axis=0)
        return carry

    lax.fori_loop(0, tb // SUBLANES, group, 0)


def _rwkv_scan_rows(r, w, k, v, a, b, s0, *, tb):
    m = r.shape[0]
    spec = pl.BlockSpec((tb, D_B), lambda i: (i, 0))
    sspec = pl.BlockSpec((tb, H_B, N_B, N_B), lambda i: (i, 0, 0, 0))
    return pl.pallas_call(
        functools.partial(_rwkv_scan_rows_kernel, tb=tb),
        grid=(m // tb,),
        in_specs=[spec] * 6 + [sspec],
        out_specs=[spec, sspec],
        out_shape=[jax.ShapeDtypeStruct((m, D_B), F32), jax.ShapeDtypeStruct((m, H_B, N_B, N_B), F32)],
        compiler_params=_cparams("parallel"),
        name="rwkv_scan_rows",
    )(r, w, k, v, a, b, s0)


def _rwkv_post_kernel(y_ref, rkv_ref, g_ref, lg_ref, lb_ref, bd_ref, o_ref):
    y = y_ref[...]
    bd = bd_ref[...]
    d = y - _seg_sum(y, bd) * (1.0 / N_B)
    var = _seg_sum(d * d, bd) * (1.0 / N_B)
    yn = d * lax.rsqrt(var + GN_EPS_B) * lg_ref[...] + lb_ref[...]
    o_ref[...] = ((yn + rkv_ref[...]) * g_ref[...]).astype(o_ref.dtype)


def _rwkv_post(y, rkv, g, lg, lb, bd, *, tm):
    m = y.shape[0]
    spec = pl.BlockSpec((tm, D_B), lambda i: (i, 0))
    row = pl.BlockSpec((1, D_B), lambda i: (0, 0))
    return pl.pallas_call(
        _rwkv_post_kernel,
        grid=(m // tm,),
        in_specs=[spec, spec, spec, row, row, pl.BlockSpec((D_B, D_B), lambda i: (0, 0))],
        out_specs=spec,
        out_shape=jax.ShapeDtypeStruct((m, D_B), BF16),
        compiler_params=_cparams("parallel"),
        name="rwkv_post",
    )(y, rkv, g, lg, lb, bd)


def _lam_full(lq1, lk1, lq2, lk2, lam_init):
    return (jnp.exp(jnp.sum(lq1 * lk1, axis=-1, keepdims=True))
            - jnp.exp(jnp.sum(lq2 * lk2, axis=-1, keepdims=True)) + lam_init)


ONES_ROWS = 16


def _attn_prompt_kernel(q_ref, k_ref, vt_ref, lq1, lk1, lq2, lk2, sg_ref, o_ref, m1, a1, m2, a2, *, tk, lam_init):
    qi = pl.program_id(2)
    for m_ref, a_ref in ((m1, a1), (m2, a2)):
        m_ref[...] = jnp.full_like(m_ref, -jnp.inf)
        a_ref[...] = jnp.zeros_like(a_ref)

    q = q_ref[0]
    lane = lax.broadcasted_iota(jnp.int32, q.shape, 1)
    zero = jnp.zeros_like(q)
    q_sub = (jnp.where(lane < D_HEAD, q, zero), jnp.where(lane < D_HEAD, zero, q))
    ones = jnp.ones((ONES_ROWS, tk), BF16)

    def update(kis, masked):
        ks = [k_ref[0, pl.ds(pl.multiple_of(ki * tk, tk), tk), :] for ki in kis]
        vts = [jnp.concatenate([vt_ref[0, ki], ones], axis=0) for ki in kis]
        sts = [[lax.dot_general(k, qs, (((1,), (1,)), ((), ())), preferred_element_type=F32)
                for k in ks] for qs in q_sub]
        for st_blocks, m_ref, a_ref in ((sts[0], m1, a1), (sts[1], m2, a2)):
            if masked:
                kr = lax.broadcasted_iota(jnp.int32, st_blocks[0].shape, 0)
                qc = lax.broadcasted_iota(jnp.int32, st_blocks[0].shape, 1)
                st_blocks = [jnp.where(kr <= qc, st, -jnp.inf) for st in st_blocks]
            m_old = m_ref[...]
            m_new = m_old
            for st in st_blocks:
                m_new = jnp.maximum(m_new, jnp.max(st, axis=0, keepdims=True))
            acc = jnp.exp2(m_old - m_new) * a_ref[...]
            for st, vt1 in zip(st_blocks, vts):
                acc = acc + jnp.dot(vt1, jnp.exp2(st - m_new).astype(BF16), preferred_element_type=F32)
            a_ref[...] = acc
            m_ref[...] = m_new

    def body(j, carry):
        update([2 * j, 2 * j + 1], False)
        return carry

    lax.fori_loop(0, lax.shift_right_logical(qi, 1), body, 0)

    @pl.when((qi & 1) == 1)
    def _():
        update([qi - 1], False)

    update([qi], True)
    lam = _lam_full(lq1[...], lk1[...], lq2[...], lk2[...], lam_init)
    d = 2 * D_HEAD
    ot = a1[:d, :] / a1[d:d + 1, :] - lam * (a2[:d, :] / a2[d:d + 1, :])
    on = ot * lax.rsqrt(jnp.mean(ot * ot, axis=0, keepdims=True) + SUBLN_EPS)
    o_ref[0] = (on.T * sg_ref[...] * (1.0 - lam_init)).astype(o_ref.dtype)


def _attn_prompt(q, k, vt, lams, subln_g, lam_init, *, tq):
    nb, t, _ = q.shape
    tk = vt.shape[3]
    assert tk == tq
    qspec = pl.BlockSpec((1, tq, LANES), lambda b, h, i: (b, i, h))
    kspec = pl.BlockSpec((1, t, LANES), lambda b, h, i: (b, 0, h))
    vspec = pl.BlockSpec((1, t // tk, LANES, tk), lambda b, h, i: (b, 0, h, 0))
    row = lambda n: pl.BlockSpec((1, n), lambda b, h, i: (0, 0))
    acc = 2 * D_HEAD + ONES_ROWS
    return pl.pallas_call(
        functools.partial(_attn_prompt_kernel, tk=tk, lam_init=lam_init),
        grid=(nb, H_C, t // tq),
        in_specs=[qspec, kspec, vspec, row(D_HEAD), row(D_HEAD), row(D_HEAD), row(D_HEAD), row(LANES)],
        out_specs=qspec,
        out_shape=jax.ShapeDtypeStruct(q.shape, BF16),
        scratch_shapes=[pltpu.VMEM((1, tq), F32), pltpu.VMEM((acc, tq), F32),
                        pltpu.VMEM((1, tq), F32), pltpu.VMEM((acc, tq), F32)],
        compiler_params=_cparams("parallel", "parallel", "arbitrary"),
        name="attn_prompt",
    )(q, k, vt, *lams, subln_g)


def _attn_sample_kernel(pt_ref, q_ref, kn_ref, vn_ref, *refs, npg, lam_init):
    k_refs = refs[:npg]
    v_refs = refs[npg:2 * npg]
    bd_ref, lq1, lk1, lq2, lk2, sg_ref, o_ref, m_ref, l_ref, alo_ref, asw_ref = refs[2 * npg:]
    g = pl.program_id(1)
    q = q_ref[0]
    bd = bd_ref[...]

    def seg_scores(prod2d):
        return jnp.dot(prod2d.astype(BF16), bd, preferred_element_type=F32)

    @pl.when(g == 0)
    def _():
        vn = vn_ref[0]
        m_ref[...] = seg_scores(q * kn_ref[0])
        l_ref[...] = jnp.ones_like(l_ref)
        alo_ref[...] = vn
        asw_ref[...] = pltpu.roll(vn, D_HEAD, 1)

    for r in range(npg):
        kp = k_refs[r][...]
        vp = v_refs[r][...]
        sc = seg_scores((kp * q[None]).reshape(PAGE_SIZE * H_C, LANES)).reshape(PAGE_SIZE, H_C, LANES)
        m_old = m_ref[...]
        m_new = jnp.maximum(m_old, jnp.max(sc, axis=0))
        alpha = jnp.exp(m_old - m_new)
        p = jnp.exp(sc - m_new[None])
        l_ref[...] = alpha * l_ref[...] + jnp.sum(p, axis=0)
        alo_ref[...] = alpha * alo_ref[...] + jnp.sum(p * vp, axis=0)
        asw_ref[...] = alpha * asw_ref[...] + jnp.sum(p * pltpu.roll(vp, D_HEAD, 2), axis=0)
        m_ref[...] = m_new

    @pl.when(g == pl.num_programs(1) - 1)
    def _():
        lane = lax.broadcasted_iota(jnp.int32, (H_C, LANES), 1)
        lo = lane < D_HEAD
        a = alo_ref[...] / l_ref[...]
        bsw = pltpu.roll(asw_ref[...] / l_ref[...], D_HEAD, 1)
        o1 = jnp.where(lo, a, bsw)
        o2 = jnp.where(lo, bsw, a)
        lam = _lam_full(lq1[...], lk1[...], lq2[...], lk2[...], lam_init)
        o_ref[0] = (_rms(o1 - lam * o2, sg_ref[...], SUBLN_EPS) * (1.0 - lam_init)).astype(o_ref.dtype)


def _attn_sample(q, k_new, v_new, cache_k, cache_v, page_table, layer, bd, lams, subln_g, lam_init, *, npg):
    nb = q.shape[0]
    n_pages = page_table.shape[1]
    hspec = pl.BlockSpec((1, H_C, LANES), lambda b, g, pt: (b, 0, 0))
    row = lambda n: pl.BlockSpec((1, n), lambda b, g, pt: (0, 0))

    def page_spec(r):
        return pl.BlockSpec((None, None, PAGE_SIZE, H_C, LANES),
                            lambda b, g, pt: (layer, pt[b, g * npg + r], 0, 0, 0))

    grid_spec = pltpu.PrefetchScalarGridSpec(
        num_scalar_prefetch=1,
        grid=(nb, n_pages // npg),
        in_specs=[hspec, hspec, hspec] + [page_spec(r) for r in range(npg)] * 2
                 + [pl.BlockSpec((LANES, LANES), lambda b, g, pt: (0, 0)),
                    row(D_HEAD), row(D_HEAD), row(D_HEAD), row(D_HEAD), row(LANES)],
        out_specs=hspec,
        scratch_shapes=[pltpu.VMEM((H_C, LANES), F32)] * 4,
    )
    return pl.pallas_call(
        functools.partial(_attn_sample_kernel, npg=npg, lam_init=lam_init),
        grid_spec=grid_spec,
        out_shape=jax.ShapeDtypeStruct((nb, H_C, LANES), BF16),
        compiler_params=_cparams("parallel", "arbitrary"),
        name="attn_sample",
    )(page_table, q, k_new, v_new, *([cache_k] * npg), *([cache_v] * npg), bd, *lams, subln_g)


def _merge_kernel(x_ref, oa_ref, ob_ref, oc_ref, cg_ref, wa_ref, wb_ref, wc_ref, wo_ref, xo_ref):
    cg = cg_ref[...]
    dot = lambda a, w: jnp.dot(a, w, preferred_element_type=F32)
    m = (jax.nn.sigmoid(cg[:, 0:D_MODEL]) * dot(oa_ref[...], wa_ref[...])
         + jax.nn.sigmoid(cg[:, D_MODEL:2 * D_MODEL]) * dot(ob_ref[...], wb_ref[...])
         + jax.nn.sigmoid(cg[:, 2 * D_MODEL:3 * D_MODEL]) * dot(oc_ref[...], wc_ref[...]))
    xo_ref[...] = x_ref[...] + dot(m.astype(BF16), wo_ref[...])


def _merge(x, oa, ob, oc, cg, wa, wb, wc, wo, *, tm):
    m = x.shape[0]
    rows = lambda n: pl.BlockSpec((tm, n), lambda i: (i, 0))
    full = lambda a: pl.BlockSpec(a.shape, lambda i: (0, 0))
    return pl.pallas_call(
        _merge_kernel,
        grid=(m // tm,),
        in_specs=[rows(D_MODEL), rows(D_A), rows(D_B), rows(D_C), rows(3 * D_MODEL),
                  full(wa), full(wb), full(wc), full(wo)],
        out_specs=rows(D_MODEL),
        out_shape=jax.ShapeDtypeStruct((m, D_MODEL), F32),
        compiler_params=_cparams("parallel"),
        name="merge",
    )(x, oa, ob, oc, cg, wa, wb, wc, wo)


def _ffn_kernel(x_ref, g_ref, wu_ref, wd_ref, gf_ref, o_ref, h_ref, acc_ref, *, final):
    j = pl.program_id(1)

    @pl.when(j == 0)
    def _():
        h_ref[...] = _rms(x_ref[...], g_ref[...], EPS).astype(BF16)
        acc_ref[...] = jnp.zeros_like(acc_ref)

    up = jnp.dot(h_ref[...], wu_ref[...], preferred_element_type=F32)
    act = jnp.square(jnp.maximum(up, 0.0)).astype(BF16)
    acc_ref[...] += jnp.dot(act, wd_ref[...], preferred_element_type=F32)

    @pl.when(j == pl.num_programs(1) - 1)
    def _():
        xn = x_ref[...] + acc_ref[...]
        o_ref[...] = _rms(xn, gf_ref[...], EPS) if final else xn


def _ffn(x, g, wu, wd, gf, *, final, tm, tf):
    m = x.shape[0]
    row = pl.BlockSpec((1, D_MODEL), lambda i, j: (0, 0))
    return pl.pallas_call(
        functools.partial(_ffn_kernel, final=final),
        grid=(m // tm, D_FF // tf),
        in_specs=[pl.BlockSpec((tm, D_MODEL), lambda i, j: (i, 0)), row,
                  pl.BlockSpec((D_MODEL, tf), lambda i, j: (0, j)),
                  pl.BlockSpec((tf, D_MODEL), lambda i, j: (j, 0)), row],
        out_specs=pl.BlockSpec((tm, D_MODEL), lambda i, j: (i, 0)),
        out_shape=jax.ShapeDtypeStruct((m, D_MODEL), F32),
        scratch_shapes=[pltpu.VMEM((tm, D_MODEL), BF16), pltpu.VMEM((tm, D_MODEL), F32)],
        compiler_params=_cparams("parallel", "arbitrary"),
        name="ffn",
    )(x, g, wu, wd, gf)


def _pad_b_cols(a):
    o = 3 * D_B
    z = lambda n: jnp.zeros(a.shape[:-1] + (n,), a.dtype)
    return jnp.concatenate([a[..., :o + LORA_W], z(LW_PAD - LORA_W),
                            a[..., o + LORA_W:o + LORA_W + LORA_A], z(LA_PAD - LORA_A),
                            a[..., o + LORA_W + LORA_A:], z(LG_PAD - LORA_G)], axis=-1)


def _unpad_b_cols(a):
    o = 3 * D_B
    return jnp.concatenate([a[..., :o + LORA_W], a[..., o + LW_PAD:o + LW_PAD + LORA_A],
                            a[..., o + LW_PAD + LA_PAD:o + LW_PAD + LA_PAD + LORA_G]], axis=-1)


def _pad_rows(a, n):
    return jnp.concatenate([a, jnp.zeros((n - a.shape[0],) + a.shape[1:], a.dtype)], axis=0)


def _ones_blockdiag(n, seg):
    i = np.arange(n) // seg
    return jnp.asarray((i[:, None] == i[None, :]).astype(np.float32), dtype=BF16)


def _layer_weights(l, p):
    w_in = p["w_in"][l]
    o_b = N_A_COLS
    o_q = o_b + N_B_COLS
    w = dict(
        norm_mix=p["norm_mix"][l].reshape(1, D_MODEL),
        w_a=w_in[:, :o_b].astype(BF16),
        w_b=_pad_b_cols(w_in[:, o_b:o_q]).astype(BF16),
        w_q=w_in[:, o_q:o_q + D_C].astype(BF16),
        w_k=w_in[:, o_q + D_C:o_q + 2 * D_C].astype(BF16),
        w_v=w_in[:, o_q + 2 * D_C:o_q + 3 * D_C].astype(BF16),
        w_g=w_in[:, o_q + 3 * D_C:].astype(BF16),
        lng=p["sgu_ln_g"][l].reshape(1, D_A), lnb=p["sgu_ln_b"][l].reshape(1, D_A),
        sgu_w=p["sgu_w"][l],
        sgu_bias_td=jnp.repeat(p["sgu_b"][l].T, D_A // G_A, axis=1),
        sgu_w0=jnp.repeat(p["sgu_w"][l][:, 0, 0], D_A // G_A).reshape(1, D_A),
        sgu_b0=jnp.repeat(p["sgu_b"][l][:, 0], D_A // G_A).reshape(1, D_A),
        rwkv=[_pad_b_cols(p["shift_mu"][l]).reshape(1, N_B_PAD),
              p["w0"][l].reshape(1, D_B), _pad_rows(p["w2"][l], LW_PAD).astype(BF16),
              p["a0"][l].reshape(1, D_B), _pad_rows(p["a2"][l], LA_PAD).astype(BF16),
              _pad_rows(p["g2"][l], LG_PAD).astype(BF16),
              p["k_k"][l].reshape(1, D_B), p["k_a"][l].reshape(1, D_B), p["r_k"][l].reshape(1, D_B),
              _ones_blockdiag(D_B, N_B)],
        lnx_g=p["lnx_g"][l].reshape(1, D_B), lnx_b=p["lnx_b"][l].reshape(1, D_B),
        lams=[p[n][l].reshape(1, D_HEAD) for n in ("lam_q1", "lam_k1", "lam_q2", "lam_k2")],
        subln_g=p["subln_g"][l].reshape(1, 2 * D_HEAD),
        w_br_a=p["w_br_a"][l].astype(BF16), w_br_b=p["w_br_b"][l].astype(BF16),
        w_br_c=p["w_br_c"][l].astype(BF16), w_out=p["w_out"][l].astype(BF16),
        norm_ffn=p["norm_ffn"][l].reshape(1, D_MODEL),
        w_up=p["w_up"][l].astype(BF16), w_down=p["w_down"][l].astype(BF16),
        norm_final=p["norm_final"].reshape(1, D_MODEL),
    )
    return w


def _project_all(x, w, rope_tabs, tm, t_seq=None):
    pj = functools.partial(_proj, x, w["norm_mix"], tm=tm, tn=D_MODEL)
    qscale = 1.0 / math.sqrt(D_HEAD)
    (ca,) = pj(w["w_a"], out_dtypes=(F32,), name="proj_a")
    (cb,) = pj(w["w_b"], out_dtypes=(F32,), name="proj_b")
    (cg,) = pj(w["w_g"], out_dtypes=(F32,), name="proj_g")
    if t_seq is None:
        (q,) = pj(w["w_q"], out_dtypes=(F32,), rope_tabs=rope_tabs, scale=qscale, name="proj_q")
        (kf,) = pj(w["w_k"], out_dtypes=(F32,), rope_tabs=rope_tabs, name="proj_k")
        (vf,) = pj(w["w_v"], out_dtypes=(F32,), name="proj_v")
        return ca, cb, cg, q, kf, vf
    (q,) = pj(w["w_q"], out_dtypes=(BF16,), rope_tabs=rope_tabs, scale=qscale * math.log2(math.e), name="proj_q")
    kf, kb = pj(w["w_k"], out_dtypes=(F32, BF16), rope_tabs=rope_tabs, name="proj_k")
    vf, vt = _proj(x, w["norm_mix"], w["w_v"], tm=ATTN_BLOCK, tn=D_MODEL, out_dtypes=(F32, BF16), t_seq=t_seq,
                   name="proj_v")
    return ca, cb, cg, q, kf, vf, kb, vt


def _prompt_layer(x, l, w, rope_tabs, nb, t, final):
    lam_init = 0.8 - 0.6 * math.exp(-0.3 * l)
    ca, cb, cg, qb, kf, vf, kb, vt = _project_all(x, w, rope_tabs, 1024, t_seq=t)
    o_a = _sgu_prompt(ca, w["lng"], w["lnb"], w["sgu_w"], w["sgu_bias_td"], tm=512)
    prev0 = jnp.zeros((nb, 1, N_B_PAD), F32)
    *rw, last = _rwkv_pre(cb.reshape(nb, t, N_B_PAD), prev0, w["rwkv"], seq=True, tm=256)
    r, dec, k2, v, na, bb, g, rkv = rw
    y, s_fin = _rwkv_scan_seq(r, dec, k2, v, na, bb, tc=64)
    flat = lambda a: a.reshape(nb * t, D_B)
    o_b = _rwkv_post(flat(y), flat(rkv), flat(g), w["lnx_g"], w["lnx_b"], w["rwkv"][-1], tm=1024)
    b3 = lambda a: a.reshape(nb, t, D_C)
    o_c = _attn_prompt(b3(qb), b3(kb), vt, w["lams"], w["subln_g"], lam_init, tq=ATTN_BLOCK)
    x = _merge(x, o_a, o_b, o_c.reshape(nb * t, D_C), cg, w["w_br_a"], w["w_br_b"], w["w_br_c"], w["w_out"], tm=256)
    x = _ffn(x, w["norm_ffn"], w["w_up"], w["w_down"], w["norm_final"], final=final, tm=1024, tf=1024)
    return x, kf, vf, s_fin, _unpad_b_cols(last[:, 0, :])


def _sample_layer(x, l, w, rope_tabs, cache_k, cache_v, page_table, prev_shift, s0, final):
    nb = x.shape[0]
    lam_init = 0.8 - 0.6 * math.exp(-0.3 * l)
    ca, cb, cg, qb, kf, vf = _project_all(x, w, rope_tabs, nb)
    o_a, va = _sgu_sample(ca, w["lng"], w["lnb"], w["sgu_w0"], w["sgu_b0"])
    r, dec, k2, v, na, bb, g, rkv = _rwkv_pre(cb, _pad_b_cols(prev_shift), w["rwkv"], seq=False, tm=nb)
    y, s_new = _rwkv_scan_rows(r, dec, k2, v, na, bb, s0, tb=32)
    o_b = _rwkv_post(y, rkv, g, w["lnx_g"], w["lnx_b"], w["rwkv"][-1], tm=nb)
    h3 = lambda a: a.reshape(nb, H_C, LANES)
    o_c = _attn_sample(h3(qb), h3(kf), h3(vf), cache_k, cache_v, page_table, l,
                       _ones_blockdiag(LANES, D_HEAD), w["lams"], w["subln_g"], lam_init, npg=4)
    x = _merge(x, o_a, o_b, o_c.reshape(nb, D_C), cg, w["w_br_a"], w["w_br_b"], w["w_br_c"], w["w_out"], tm=nb)
    x = _ffn(x, w["norm_ffn"], w["w_up"], w["w_down"], w["norm_final"], final=final, tm=nb, tf=1024)
    return x, kf, vf, s_new, _unpad_b_cols(cb), va


def kernel(x_prompt, x_sample, cache_k, cache_v, state_rwkv, state_shift, page_table, norm_mix, w_in, sgu_ln_g, sgu_ln_b, sgu_w, sgu_b, shift_mu, w0, w2, a0, a2, g2, k_k, k_a, r_k, lnx_g, lnx_b, lam_q1, lam_k1, lam_q2, lam_k2, subln_g, w_br_a, w_br_b, w_br_c, w_out, norm_ffn, w_up, w_down, norm_final):
    p = dict(norm_mix=norm_mix, w_in=w_in, sgu_ln_g=sgu_ln_g, sgu_ln_b=sgu_ln_b, sgu_w=sgu_w, sgu_b=sgu_b,
             shift_mu=shift_mu, w0=w0, w2=w2, a0=a0, a2=a2, g2=g2, k_k=k_k, k_a=k_a, r_k=r_k, lnx_g=lnx_g,
             lnx_b=lnx_b, lam_q1=lam_q1, lam_k1=lam_k1, lam_q2=lam_q2, lam_k2=lam_k2, subln_g=subln_g,
             w_br_a=w_br_a, w_br_b=w_br_b, w_br_c=w_br_c, w_out=w_out, norm_ffn=norm_ffn, w_up=w_up,
             w_down=w_down, norm_final=norm_final)
    depth = w_in.shape[0]
    bp, tp, _ = x_prompt.shape
    bs, ts, _ = x_sample.shape
    tabs_p = _rope_tables(jnp.arange(tp))
    tabs_s = _rope_tables(jnp.full((bs * ts,), PAST_LEN, jnp.int32))
    xp = x_prompt.reshape(bp * tp, D_MODEL)
    xs = x_sample.reshape(bs * ts, D_MODEL)
    outs_p, outs_s = [], []
    for l in range(depth):
        w = _layer_weights(l, p)
        final = l == depth - 1
        xp, kp, vp, sp, shp = _prompt_layer(xp, l, w, tabs_p, bp, tp, final)
        xs, ks, vs, ss, shs, va = _sample_layer(xs, l, w, tabs_s, cache_k, cache_v, page_table,
                                                state_shift[l], state_rwkv[l], final)
        outs_p.append((kp.reshape(bp, tp, H_C, 2 * D_HEAD), vp.reshape(bp, tp, H_C, 2 * D_HEAD), sp, shp))
        outs_s.append((ks.reshape(bs, ts, H_C, 2 * D_HEAD), vs.reshape(bs, ts, H_C, 2 * D_HEAD), ss, shs,
                       va.reshape(bs, ts, D_A)))
    stack = lambda outs, i: jnp.stack([o[i] for o in outs])
    return (xp.reshape(bp, tp, D_MODEL), xs.reshape(bs, ts, D_MODEL),
            stack(outs_p, 0), stack(outs_p, 1), stack(outs_p, 2), stack(outs_p, 3),
            stack(outs_s, 0), stack(outs_s, 1), stack(outs_s, 2), stack(outs_s, 3), stack(outs_s, 4))
```

```python
import functools
import math

import numpy as np
import jax
import jax.numpy as jnp
from jax import lax
from jax.experimental import pallas as pl
from jax.experimental.pallas import tpu as pltpu

F32 = jnp.float32
BF16 = jnp.bfloat16

D_MODEL = 1024
PAST_LEN = 2048
PAGE_SIZE = 128
CHUNK = 128
D_A = 512
G_A = 8
H_B = 8
N_B = 64
D_B = H_B * N_B
LORA_W = 64
LORA_A = 64
LORA_G = 160
GN_EPS_B = 64e-5
H_C = 8
D_HEAD = 64
D_C = H_C * 2 * D_HEAD
ROPE_DIM = D_HEAD // 4
ROPE_THETA = 500000.0
SUBLN_EPS = 1e-5
D_FF = 4 * D_MODEL
EPS = 1e-6
N_A_COLS = 2 * D_A
N_B_COLS = 3 * D_B + LORA_W + LORA_A + LORA_G
LANES = 128
SUBLANES = 8
LW_PAD = 128
LA_PAD = 128
LG_PAD = 256
N_B_PAD = 3 * D_B + LW_PAD + LA_PAD + LG_PAD
VMEM_LIMIT = 52 * 1024 * 1024
ATTN_BLOCK = 512
ATTN_GROUP = 4


def _cparams(*sem):
    return pltpu.CompilerParams(dimension_semantics=sem, vmem_limit_bytes=VMEM_LIMIT)


def _rms(x, g, eps):
    return x * lax.rsqrt(jnp.mean(x * x, axis=-1, keepdims=True) + eps) * g


def _proj_kernel(*refs, rope, scale, n_out):
    x_ref, g_ref, w_ref = refs[:3]
    pos = 3
    if rope:
        c_ref, s1_ref, s2_ref = refs[3:6]
        pos = 6
    out_refs = refs[pos:pos + n_out]
    h_ref = refs[pos + n_out]

    @pl.when(pl.program_id(1) == 0)
    def _():
        h_ref[...] = _rms(x_ref[...], g_ref[...], EPS).astype(BF16)

    acc = jnp.dot(h_ref[...], w_ref[...], preferred_element_type=F32)
    if rope:
        c, s1, s2 = c_ref[...], s1_ref[...], s2_ref[...]
        parts = []
        for hh in range(acc.shape[1] // LANES):
            a = acc[:, hh * LANES:(hh + 1) * LANES]
            parts.append(a * c + pltpu.roll(a, ROPE_DIM // 2, 1) * s1
                         + pltpu.roll(a, LANES - ROPE_DIM // 2, 1) * s2)
        acc = jnp.concatenate(parts, axis=1)
    if scale != 1.0:
        acc = acc * scale
    for o_ref in out_refs:
        if len(o_ref.shape) == 4:
            o_ref[0, 0] = acc.T.astype(o_ref.dtype)
        else:
            o_ref[...] = acc.astype(o_ref.dtype)


def _proj(x, g, w, *, tm, tn, out_dtypes, rope_tabs=None, scale=1.0, t_seq=None, name="proj"):
    m, d = x.shape
    n = w.shape[1]
    in_specs = [pl.BlockSpec((tm, d), lambda i, j: (i, 0)),
                pl.BlockSpec((1, d), lambda i, j: (0, 0)),
                pl.BlockSpec((d, tn), lambda i, j: (0, j))]
    args = [x, g, w]
    if rope_tabs is not None:
        nt = rope_tabs[0].shape[0] // tm
        in_specs += [pl.BlockSpec((tm, LANES), lambda i, j: (i % nt, 0))] * 3
        args += list(rope_tabs)
    out_specs = [pl.BlockSpec((tm, tn), lambda i, j: (i, j)) for _ in out_dtypes]
    out_shape = [jax.ShapeDtypeStruct((m, n), dt) for dt in out_dtypes]
    if t_seq is not None:
        nts = t_seq // tm
        out_specs[-1] = pl.BlockSpec((1, 1, tn, tm), lambda i, j: (i // nts, i % nts, j, 0))
        out_shape[-1] = jax.ShapeDtypeStruct((m // t_seq, nts, n, tm), out_dtypes[-1])
    outs = pl.pallas_call(
        functools.partial(_proj_kernel, rope=rope_tabs is not None, scale=scale, n_out=len(out_dtypes)),
        grid=(m // tm, n // tn),
        in_specs=in_specs,
        out_specs=out_specs,
        out_shape=out_shape,
        scratch_shapes=[pltpu.VMEM((tm, d), BF16)],
        compiler_params=_cparams("parallel", "arbitrary"),
        name=name,
    )(*args)
    return outs


def _rope_kernel(pos_ref, invf_ref, c_ref, s1_ref, s2_ref):
    ang = pos_ref[...] * invf_ref[...]
    lane = lax.broadcasted_iota(jnp.int32, ang.shape, 1) % D_HEAD
    first = lane < ROPE_DIM // 2
    second = (lane >= ROPE_DIM // 2) & (lane < ROPE_DIM)
    cos, sin = jnp.cos(ang), jnp.sin(ang)
    c_ref[...] = jnp.where(first | second, cos, 1.0)
    s1_ref[...] = jnp.where(second, sin, 0.0)
    s2_ref[...] = jnp.where(first, -sin, 0.0)


def _rope_tables(pos):
    t = pos.shape[0]
    half = ROPE_DIM // 2
    inv_freq = ROPE_THETA ** (-jnp.arange(half, dtype=F32) / half)
    blk = jnp.concatenate([inv_freq, inv_freq, jnp.zeros((D_HEAD - ROPE_DIM,), F32)])
    invf = jnp.concatenate([blk, blk]).reshape(1, LANES)
    tm = min(t, 512)
    return pl.pallas_call(
        _rope_kernel,
        grid=(t // tm,),
        in_specs=[pl.BlockSpec((tm, 1), lambda i: (i, 0)), pl.BlockSpec((1, LANES), lambda i: (0, 0))],
        out_specs=[pl.BlockSpec((tm, LANES), lambda i: (i, 0))] * 3,
        out_shape=[jax.ShapeDtypeStruct((t, LANES), F32)] * 3,
        compiler_params=_cparams("parallel"),
        name="rope_tables",
    )(pos.astype(F32).reshape(t, 1), invf)


def _gelu_ln(ca, lng, lnb):
    gx = 0.5 * ca * (1.0 + lax.erf(ca * math.sqrt(0.5)))
    u, v = gx[:, :D_A], gx[:, D_A:]
    d = v - jnp.mean(v, axis=-1, keepdims=True)
    va = d * lax.rsqrt(jnp.mean(d * d, axis=-1, keepdims=True) + EPS) * lng + lnb
    return u, va


def _sgu_prompt_kernel(ca_ref, lng_ref, lnb_ref, w_ref, b_ref, oa_ref, *, nchunk):
    u, va = _gelu_ln(ca_ref[...], lng_ref[...], lnb_ref[...])
    vab = va.astype(BF16)
    row = lax.broadcasted_iota(jnp.int32, (CHUNK, CHUNK), 0)
    col = lax.broadcasted_iota(jnp.int32, (CHUNK, CHUNK), 1)
    wcat = jnp.concatenate([jnp.where(col <= row, w_ref[g], 0.0).astype(BF16) for g in range(G_A)], axis=1)
    grp = lax.broadcasted_iota(jnp.int32, (CHUNK, D_A), 1) // (D_A // G_A)
    for ci in range(nchunk):
        vc = vab[ci * CHUNK:(ci + 1) * CHUNK]
        vbig = jnp.concatenate([jnp.where(grp == g, vc, jnp.zeros_like(vc)) for g in range(G_A)], axis=0)
        s = jnp.dot(wcat, vbig, preferred_element_type=F32) + b_ref[...]
        oa_ref[ci * CHUNK:(ci + 1) * CHUNK, :] = (u[ci * CHUNK:(ci + 1) * CHUNK] * s).astype(oa_ref.dtype)


def _sgu_prompt(ca, lng, lnb, w, bias_td, *, tm):
    m = ca.shape[0]
    return pl.pallas_call(
        functools.partial(_sgu_prompt_kernel, nchunk=tm // CHUNK),
        grid=(m // tm,),
        in_specs=[pl.BlockSpec((tm, N_A_COLS), lambda i: (i, 0)),
                  pl.BlockSpec((1, D_A), lambda i: (0, 0)),
                  pl.BlockSpec((1, D_A), lambda i: (0, 0)),
                  pl.BlockSpec((G_A, CHUNK, CHUNK), lambda i: (0, 0, 0)),
                  pl.BlockSpec((CHUNK, D_A), lambda i: (0, 0))],
        out_specs=pl.BlockSpec((tm, D_A), lambda i: (i, 0)),
        out_shape=jax.ShapeDtypeStruct((m, D_A), BF16),
        compiler_params=_cparams("parallel"),
        name="sgu_prompt",
    )(ca, lng, lnb, w, bias_td)


def _sgu_sample_kernel(ca_ref, lng_ref, lnb_ref, w0_ref, b0_ref, oa_ref, va_ref):
    u, va = _gelu_ln(ca_ref[...], lng_ref[...], lnb_ref[...])
    va_ref[...] = va
    oa_ref[...] = (u * (va * w0_ref[...] + b0_ref[...])).astype(oa_ref.dtype)


def _sgu_sample(ca, lng, lnb, w0, b0):
    m = ca.shape[0]
    row = lambda n: pl.BlockSpec((1, n), lambda i: (0, 0))
    return pl.pallas_call(
        _sgu_sample_kernel,
        grid=(1,),
        in_specs=[pl.BlockSpec((m, N_A_COLS), lambda i: (0, 0)), row(D_A), row(D_A), row(D_A), row(D_A)],
        out_specs=[pl.BlockSpec((m, D_A), lambda i: (0, 0))] * 2,
        out_shape=[jax.ShapeDtypeStruct((m, D_A), BF16), jax.ShapeDtypeStruct((m, D_A), F32)],
        compiler_params=_cparams("arbitrary"),
        name="sgu_sample",
    )(ca, lng, lnb, w0, b0)


def _seg_sum(x, ones_bd):
    hi = x.astype(BF16)
    lo = (x - hi.astype(F32)).astype(BF16)
    return (jnp.dot(hi, ones_bd, preferred_element_type=F32)
            + jnp.dot(lo, ones_bd, preferred_element_type=F32))


def _rwkv_pre_kernel(*refs, seq):
    if seq:
        c_ref, prev_ref = refs[:2]
    else:
        c_ref, sh_ref = refs[:2]
    (mu_ref, w0_ref, w2_ref, a0_ref, a2_ref, g2_ref, kk_ref, ka_ref, rk_ref, bd_ref) = refs[2:12]
    (r_o, w_o, k_o, v_o, a_o, b_o, g_o, rkv_o) = refs[12:20]
    if seq:
        last_o, carry_ref = refs[20:22]
        cols = c_ref[0]
        tm = cols.shape[0]

        @pl.when(pl.program_id(1) == 0)
        def _():
            carry_ref[...] = prev_ref[0]

        rolled = pltpu.roll(cols, 1, 0)
        first = lax.broadcasted_iota(jnp.int32, cols.shape, 0) == 0
        shifted = jnp.where(first, carry_ref[...], rolled)
        carry_ref[...] = cols[tm - 1:tm, :]
        last_o[0] = cols[tm - 1:tm, :]
    else:
        cols = c_ref[...]
        shifted = sh_ref[...]
    xs = cols + (shifted - cols) * mu_ref[...]
    r = xs[:, 0:D_B]
    k = xs[:, D_B:2 * D_B]
    v = xs[:, 2 * D_B:3 * D_B]
    o = 3 * D_B
    wl = xs[:, o:o + LW_PAD]
    al = xs[:, o + LW_PAD:o + LW_PAD + LA_PAD]
    gl = xs[:, o + LW_PAD + LA_PAD:o + LW_PAD + LA_PAD + LG_PAD]
    z = -(w0_ref[...] + jnp.dot(jnp.tanh(wl).astype(BF16), w2_ref[...], preferred_element_type=F32))
    softplus = jnp.maximum(z, 0.0) + jnp.log1p(jnp.exp(-jnp.abs(z)))
    log_decay = -jnp.exp(-softplus - 0.5)
    decay = log_decay if seq else jnp.exp(log_decay)
    a = jax.nn.sigmoid(a0_ref[...] + jnp.dot(al.astype(BF16), a2_ref[...], preferred_element_type=F32))
    g = jnp.dot(jax.nn.sigmoid(gl).astype(BF16), g2_ref[...], preferred_element_type=F32)
    bd = bd_ref[...]
    kk = k * kk_ref[...]
    kk = kk / jnp.maximum(jnp.sqrt(_seg_sum(kk * kk, bd)), 1e-12)
    k2 = k * (1.0 + (a - 1.0) * ka_ref[...])
    if seq:
        r_o[0], w_o[0], k_o[0], v_o[0], a_o[0], b_o[0], g_o[0] = r, decay, k2, v, -kk, kk * a, g
        rkv_o[0] = _seg_sum(r * k2 * rk_ref[...], bd) * v
    else:
        r_o[...], w_o[...], k_o[...], v_o[...], a_o[...], b_o[...], g_o[...] = r, decay, k2, v, -kk, kk * a, g
        rkv_o[...] = _seg_sum(r * k2 * rk_ref[...], bd) * v


def _rwkv_pre(cols, shift_src, params, *, seq, tm):
    prm_specs = []
    for p in params:
        prm_specs.append(pl.BlockSpec(p.shape, (lambda b, i: (0, 0)) if seq else (lambda i: (0, 0))))
    if seq:
        nb, t, _ = cols.shape
        grid = (nb, t // tm)
        in_specs = [pl.BlockSpec((1, tm, N_B_PAD), lambda b, i: (b, i, 0)),
                    pl.BlockSpec((1, 1, N_B_PAD), lambda b, i: (b, 0, 0))] + prm_specs
        ospec = pl.BlockSpec((1, tm, D_B), lambda b, i: (b, i, 0))
        oshape = jax.ShapeDtypeStruct((nb, t, D_B), F32)
        out_specs = [ospec] * 8 + [pl.BlockSpec((1, 1, N_B_PAD), lambda b, i: (b, 0, 0))]
        out_shape = [oshape] * 8 + [jax.ShapeDtypeStruct((nb, 1, N_B_PAD), F32)]
        scratch = [pltpu.VMEM((1, N_B_PAD), F32)]
        sem = ("parallel", "arbitrary")
    else:
        m = cols.shape[0]
        grid = (m // tm,)
        in_specs = [pl.BlockSpec((tm, N_B_PAD), lambda i: (i, 0))] * 2 + prm_specs
        out_specs = [pl.BlockSpec((tm, D_B), lambda i: (i, 0))] * 8
        out_shape = [jax.ShapeDtypeStruct((m, D_B), F32)] * 8
        scratch = []
        sem = ("parallel",)
    return pl.pallas_call(
        functools.partial(_rwkv_pre_kernel, seq=seq),
        grid=grid, in_specs=in_specs, out_specs=out_specs, out_shape=out_shape,
        scratch_shapes=scratch, compiler_params=_cparams(*sem),
        name="rwkv_pre_seq" if seq else "rwkv_pre_rows",
    )(cols, shift_src, *params)


def _scan_pair_step(S, a, w, b, k, v, r):
    lane = lax.broadcasted_iota(jnp.int32, (N_B, LANES), 1)
    row = lax.broadcasted_iota(jnp.int32, (N_B, LANES), 0)
    lo = lane < N_B
    e1 = lane == row
    e2 = lane == row + N_B

    def half_sums(p):
        t1 = jnp.sum(jnp.where(lo, p, 0.0), axis=1, keepdims=True)
        t2 = jnp.sum(jnp.where(lo, 0.0, p), axis=1, keepdims=True)
        return t1, t2

    t1, t2 = half_sums(S * a)
    sa = jnp.where(lo, t1, t2)
    vb = jnp.where(lo, jnp.sum(jnp.where(e1, v, 0.0), axis=1, keepdims=True),
                   jnp.sum(jnp.where(e2, v, 0.0), axis=1, keepdims=True))
    S = S * w + sa * b + vb * k
    y1, y2 = half_sums(S * r)
    y = jnp.sum(jnp.where(e1, y1, 0.0) + jnp.where(e2, y2, 0.0), axis=0, keepdims=True)
    return S, y


def _split(x):
    hi = x.astype(BF16)
    return hi, (x - hi.astype(F32)).astype(BF16)


def _mm3(xs, ys, nt=False):
    lhs = jnp.concatenate([xs[0], xs[0], xs[1]], axis=1)
    if nt:
        rhs = jnp.concatenate([ys[0], ys[1], ys[0]], axis=1)
        return lax.dot_general(lhs, rhs, (((1,), (1,)), ((), ())), preferred_element_type=F32)
    rhs = jnp.concatenate([ys[0], ys[1], ys[0]], axis=0)
    return jnp.dot(lhs, rhs, preferred_element_type=F32)


def _block_diag(x):
    lo = lax.broadcasted_iota(jnp.int32, x.shape, 1) < N_B
    zero = jnp.zeros_like(x)
    return jnp.concatenate([jnp.where(lo, x, zero), jnp.where(lo, zero, x)], axis=0)


def _each(fn, *lists):
    return [fn(*args) for args in zip(*lists)]


def _rwkv_chunk_pairs(zs, lws, a_s, bs, ks, vs, rs):
    c = lws[0].shape[0]
    tri = (lax.broadcasted_iota(jnp.int32, (c, c), 0) >= lax.broadcasted_iota(jnp.int32, (c, c), 1)).astype(BF16)
    tri3 = jnp.concatenate([tri, tri, tri], axis=1)

    def cumsum(lw):
        l1 = lw.astype(BF16)
        r1 = lw - l1.astype(F32)
        l2 = r1.astype(BF16)
        l3 = (r1 - l2.astype(F32)).astype(BF16)
        return jnp.dot(tri3, jnp.concatenate([l1, l2, l3], axis=0), preferred_element_type=F32)

    lam = _each(cumsum, lws)
    lam_c = [x[c - 1:c, :] for x in lam]
    w_inv = [jnp.exp(-x) for x in lam]
    w_rem = _each(lambda lc, x: jnp.exp(lc - x), lam_c, lam)
    r_h = _each(lambda r, x: _block_diag(r * jnp.exp(x)), rs, lam)
    s_a = _each(lambda a, x, lw: _split(_block_diag(a * jnp.exp(x - lw))), a_s, lam, lws)
    s_r = _each(_split, r_h)
    s_b = _each(lambda b, wi: _split(_block_diag(b * wi)), bs, w_inv)
    s_k = _each(lambda k, wi: _split(_block_diag(k * wi)), ks, w_inv)
    s_v = [_split(_block_diag(v)) for v in vs]
    n = 2 * c
    ri = lax.broadcasted_iota(jnp.int32, (n, n), 0)
    ci = lax.broadcasted_iota(jnp.int32, (n, n), 1)
    strict, incl, eye = ci < ri, ci <= ri, ci == ri
    gram = lambda mask: (lambda x, y: jnp.where(mask, _mm3(x, y, nt=True), 0.0))
    lab = _each(gram(strict), s_a, s_b)
    lak = _each(gram(strict), s_a, s_k)
    lrb = _each(gram(incl), s_r, s_b)
    lrk = _each(gram(incl), s_r, s_k)
    t = [jnp.where(eye, 1.0, 0.0) + x for x in lab]
    s_p = _each(_split, lab)
    for _ in range(int(math.log2(c)) - 1):
        s_p = _each(lambda p: _split(_mm3(p, p)), s_p)
        t = _each(lambda tt, p: tt + _mm3(_split(tt), p), t, s_p)
    s_t = _each(_split, t)
    s_at = _each(lambda tt, a: _split(_mm3(tt, a)), s_t, s_a)
    x2 = _each(lambda l, v: _split(_mm3(_split(l), v)), lak, s_v)
    s_u0 = _each(lambda tt, x: _split(_mm3(tt, x)), s_t, x2)
    s_lrb = _each(_split, lrb)
    rt = _each(lambda rh, l, at: rh + _mm3(l, at), r_h, s_lrb, s_at)
    y0 = _each(lambda l, u, lk, v: _mm3(l, u) + _mm3(_split(lk), v), s_lrb, s_u0, lrk, s_v)
    s_bt = _each(lambda b, wr: _split(_block_diag(b * wr).T), bs, w_rem)
    s_kt = _each(lambda k, wr: _split(_block_diag(k * wr).T), ks, w_rem)
    m = _each(lambda lc, bt, at: jnp.where(eye, jnp.exp(lc), 0.0) + _mm3(bt, at), lam_c, s_bt, s_at)
    nn = _each(lambda bt, u, kt, v: _mm3(bt, u) + _mm3(kt, v), s_bt, s_u0, s_kt, s_v)
    s_z = _each(_split, zs)
    ybd = _each(lambda r, z, y: _mm3(_split(r), z) + y, rt, s_z, y0)
    z_new = _each(lambda mm, z, x: _mm3(_split(mm), z) + x, m, s_z, nn)
    return z_new, [y[:c] + y[c:] for y in ybd]


def _rwkv_scan_seq_kernel(r_ref, w_ref, k_ref, v_ref, a_ref, b_ref, y_ref, sf_ref, z_ref):
    npair = H_B // 2
    ci = pl.program_id(0)
    bi = pl.program_id(1)

    @pl.when(ci == 0)
    def _():
        for p in range(npair):
            z_ref[bi * npair + p] = jnp.zeros((LANES, LANES), F32)

    sls = [(0, slice(None), slice(p * LANES, (p + 1) * LANES)) for p in range(npair)]
    rows = lambda ref: [ref[sl] for sl in sls]
    zs, ys = _rwkv_chunk_pairs([z_ref[bi * npair + p] for p in range(npair)], rows(w_ref), rows(a_ref),
                               rows(b_ref), rows(k_ref), rows(v_ref), rows(r_ref))
    for p in range(npair):
        z_ref[bi * npair + p] = zs[p]
        y_ref[sls[p]] = ys[p]

    @pl.when(ci == pl.num_programs(0) - 1)
    def _():
        for p in range(npair):
            st = z_ref[bi * npair + p].T
            sf_ref[bi, 2 * p] = st[:N_B, :N_B]
            sf_ref[bi, 2 * p + 1] = st[N_B:, N_B:]


def _rwkv_scan_seq(r, lw, k, v, a, b, *, tc):
    nb, t, _ = r.shape
    spec = pl.BlockSpec((1, tc, D_B), lambda c, bb: (bb, c, 0))
    return pl.pallas_call(
        _rwkv_scan_seq_kernel,
        grid=(t // tc, nb),
        in_specs=[spec] * 6,
        out_specs=[spec, pl.BlockSpec((nb, H_B, N_B, N_B), lambda c, bb: (0, 0, 0, 0))],
        out_shape=[jax.ShapeDtypeStruct((nb, t, D_B), F32), jax.ShapeDtypeStruct((nb, H_B, N_B, N_B), F32)],
        scratch_shapes=[pltpu.VMEM((nb * H_B // 2, LANES, LANES), F32)],
        compiler_params=_cparams("arbitrary", "arbitrary"),
        name="rwkv_scan_seq",
    )(r, lw, k, v, a, b)


def _rwkv_scan_rows_kernel(r_ref, w_ref, k_ref, v_ref, a_ref, b_ref, s0_ref, y_ref, sf_ref, *, tb):
    def group(gi, carry):
        base = pl.multiple_of(gi * SUBLANES, SUBLANES)
        for p in range(H_B // 2):
            sl = (pl.ds(base, SUBLANES), slice(p * LANES, (p + 1) * LANES))
            a, w, b, k, v, r = (ref[sl] for ref in (a_ref, w_ref, b_ref, k_ref, v_ref, r_ref))
            ys = []
            for j in range(SUBLANES):
                row = slice(j, j + 1)
                S0 = jnp.concatenate([s0_ref[base + j, 2 * p], s0_ref[base + j, 2 * p + 1]], axis=1)
                S, y = _scan_pair_step(S0, a[row], w[row], b[row], k[row], v[row], r[row])
                sf_ref[base + j, 2 * p] = S[:, :N_B]
                sf_ref[base + j, 2 * p + 1] = S[:, N_B:]
                ys.append(y)
            y_ref[sl] = jnp.concatenate(ys, axis=0)
        return carry

    lax.fori_loop(0, tb // SUBLANES, group, 0)


def _rwkv_scan_rows(r, w, k, v, a, b, s0, *, tb):
    m = r.shape[0]
    spec = pl.BlockSpec((tb, D_B), lambda i: (i, 0))
    sspec = pl.BlockSpec((tb, H_B, N_B, N_B), lambda i: (i, 0, 0, 0))
    return pl.pallas_call(
        functools.partial(_rwkv_scan_rows_kernel, tb=tb),
        grid=(m // tb,),
        in_specs=[spec] * 6 + [sspec],
        out_specs=[spec, sspec],
        out_shape=[jax.ShapeDtypeStruct((m, D_B), F32), jax.ShapeDtypeStruct((m, H_B, N_B, N_B), F32)],
        compiler_params=_cparams("parallel"),
        name="rwkv_scan_rows",
    )(r, w, k, v, a, b, s0)


def _rwkv_post_kernel(y_ref, rkv_ref, g_ref, lg_ref, lb_ref, bd_ref, o_ref):
    y = y_ref[...]
    bd = bd_ref[...]
    d = y - _seg_sum(y, bd) * (1.0 / N_B)
    var = _seg_sum(d * d, bd) * (1.0 / N_B)
    yn = d * lax.rsqrt(var + GN_EPS_B) * lg_ref[...] + lb_ref[...]
    o_ref[...] = ((yn + rkv_ref[...]) * g_ref[...]).astype(o_ref.dtype)


def _rwkv_post(y, rkv, g, lg, lb, bd, *, tm):
    m = y.shape[0]
    spec = pl.BlockSpec((tm, D_B), lambda i: (i, 0))
    row = pl.BlockSpec((1, D_B), lambda i: (0, 0))
    return pl.pallas_call(
        _rwkv_post_kernel,
        grid=(m // tm,),
        in_specs=[spec, spec, spec, row, row, pl.BlockSpec((D_B, D_B), lambda i: (0, 0))],
        out_specs=spec,
        out_shape=jax.ShapeDtypeStruct((m, D_B), BF16),
        compiler_params=_cparams("parallel"),
        name="rwkv_post",
    )(y, rkv, g, lg, lb, bd)


def _lam_full(lq1, lk1, lq2, lk2, lam_init):
    return (jnp.exp(jnp.sum(lq1 * lk1, axis=-1, keepdims=True))
            - jnp.exp(jnp.sum(lq2 * lk2, axis=-1, keepdims=True)) + lam_init)


ONES_ROWS = 16


def _attn_prompt_kernel(q_ref, k_ref, vt_ref, lq1, lk1, lq2, lk2, sg_ref, o_ref, m1, a1, m2, a2, *, tk, lam_init):
    qi = pl.program_id(2)
    for m_ref, a_ref in ((m1, a1), (m2, a2)):
        m_ref[...] = jnp.full_like(m_ref, -jnp.inf)
        a_ref[...] = jnp.zeros_like(a_ref)

    q = q_ref[0]
    lane = lax.broadcasted_iota(jnp.int32, q.shape, 1)
    zero = jnp.zeros_like(q)
    q_sub = (jnp.where(lane < D_HEAD, q, zero), jnp.where(lane < D_HEAD, zero, q))
    ones = jnp.ones((ONES_ROWS, tk), BF16)

    def update(kis, masked):
        ks = [k_ref[0, pl.ds(pl.multiple_of(ki * tk, tk), tk), :] for ki in kis]
        vts = [jnp.concatenate([vt_ref[0, ki], ones], axis=0) for ki in kis]
        sts = [[lax.dot_general(k, qs, (((1,), (1,)), ((), ())), preferred_element_type=F32)
                for k in ks] for qs in q_sub]
        for st_blocks, m_ref, a_ref in ((sts[0], m1, a1), (sts[1], m2, a2)):
            if masked:
                kr = lax.broadcasted_iota(jnp.int32, st_blocks[0].shape, 0)
                qc = lax.broadcasted_iota(jnp.int32, st_blocks[0].shape, 1)
                st_blocks = [jnp.where(kr <= qc, st, -jnp.inf) for st in st_blocks]
            m_old = m_ref[...]
            m_new = m_old
            for st in st_blocks:
                m_new = jnp.maximum(m_new, jnp.max(st, axis=0, keepdims=True))
            acc = jnp.exp2(m_old - m_new) * a_ref[...]
            for st, vt1 in zip(st_blocks, vts):
                acc = acc + jnp.dot(vt1, jnp.exp2(st - m_new).astype(BF16), preferred_element_type=F32)
            a_ref[...] = acc
            m_ref[...] = m_new

    def group_body(j, carry):
        update([ATTN_GROUP * j + i for i in range(ATTN_GROUP)], False)
        return carry

    def single_body(ki, carry):
        update([ki], False)
        return carry

    n_groups = lax.shift_right_logical(qi, ATTN_GROUP.bit_length() - 1)
    lax.fori_loop(0, n_groups, group_body, 0)
    lax.fori_loop(n_groups * ATTN_GROUP, qi, single_body, 0)
    update([qi], True)
    lam = _lam_full(lq1[...], lk1[...], lq2[...], lk2[...], lam_init)
    d = 2 * D_HEAD
    ot = a1[:d, :] / a1[d:d + 1, :] - lam * (a2[:d, :] / a2[d:d + 1, :])
    on = ot * lax.rsqrt(jnp.mean(ot * ot, axis=0, keepdims=True) + SUBLN_EPS)
    o_ref[0] = (on.T * sg_ref[...] * (1.0 - lam_init)).astype(o_ref.dtype)


def _attn_prompt(q, k, vt, lams, subln_g, lam_init, *, tq):
    nb, t, _ = q.shape
    tk = vt.shape[3]
    assert tk == tq
    qspec = pl.BlockSpec((1, tq, LANES), lambda b, h, i: (b, i, h))
    kspec = pl.BlockSpec((1, t, LANES), lambda b, h, i: (b, 0, h))
    vspec = pl.BlockSpec((1, t // tk, LANES, tk), lambda b, h, i: (b, 0, h, 0))
    row = lambda n: pl.BlockSpec((1, n), lambda b, h, i: (0, 0))
    acc = 2 * D_HEAD + ONES_ROWS
    return pl.pallas_call(
        functools.partial(_attn_prompt_kernel, tk=tk, lam_init=lam_init),
        grid=(nb, H_C, t // tq),
        in_specs=[qspec, kspec, vspec, row(D_HEAD), row(D_HEAD), row(D_HEAD), row(D_HEAD), row(LANES)],
        out_specs=qspec,
        out_shape=jax.ShapeDtypeStruct(q.shape, BF16),
        scratch_shapes=[pltpu.VMEM((1, tq), F32), pltpu.VMEM((acc, tq), F32),
                        pltpu.VMEM((1, tq), F32), pltpu.VMEM((acc, tq), F32)],
        compiler_params=_cparams("parallel", "parallel", "arbitrary"),
        name="attn_prompt",
    )(q, k, vt, *lams, subln_g)


def _attn_sample_kernel(pt_ref, q_ref, kn_ref, vn_ref, *refs, npg, lam_init):
    k_refs = refs[:npg]
    v_refs = refs[npg:2 * npg]
    lq1, lk1, lq2, lk2, sg_ref, o_ref, m_ref, l_ref, acc_ref = refs[2 * npg:]
    g = pl.program_id(1)
    nrow, ncol = 2 * H_C, PAGE_SIZE * H_C
    q = q_ref[0]
    lo = lax.broadcasted_iota(jnp.int32, q.shape, 1) < D_HEAD
    qz = jnp.concatenate([jnp.where(lo, q, 0.0), jnp.where(lo, 0.0, q)], axis=0).astype(BF16)
    valid = ((lax.broadcasted_iota(jnp.int32, (nrow, ncol), 1) & (H_C - 1))
             == (lax.broadcasted_iota(jnp.int32, (nrow, ncol), 0) & (H_C - 1)))

    @pl.when(g == 0)
    def _():
        as_mxu = lambda ref: jnp.concatenate([ref[0], ref[0]], axis=0).astype(BF16).astype(F32)
        m_ref[...] = jnp.sum(qz.astype(F32) * as_mxu(kn_ref), axis=1, keepdims=True)
        l_ref[...] = jnp.ones_like(l_ref)
        acc_ref[...] = as_mxu(vn_ref)

    sts = []
    for r in range(npg):
        kmat = k_refs[r][...].reshape(ncol, LANES).astype(BF16)
        st = lax.dot_general(qz, kmat, (((1,), (1,)), ((), ())), preferred_element_type=F32)
        sts.append(jnp.where(valid, st, -jnp.inf))
    m_old = m_ref[...]
    m_new = m_old
    for st in sts:
        m_new = jnp.maximum(m_new, jnp.max(st, axis=1, keepdims=True))
    alpha = jnp.exp2(m_old - m_new)
    l = alpha * l_ref[...]
    acc = alpha * acc_ref[...]
    for st, v_ref in zip(sts, v_refs):
        p = jnp.exp2(st - m_new)
        l = l + jnp.sum(p, axis=1, keepdims=True)
        acc = acc + jnp.dot(p.astype(BF16), v_ref[...].reshape(ncol, LANES).astype(BF16),
                            preferred_element_type=F32)
    m_ref[...] = m_new
    l_ref[...] = l
    acc_ref[...] = acc

    @pl.when(g == pl.num_programs(1) - 1)
    def _():
        o = acc_ref[...] / l_ref[...]
        lam = _lam_full(lq1[...], lk1[...], lq2[...], lk2[...], lam_init)
        o_ref[0] = (_rms(o[:H_C] - lam * o[H_C:], sg_ref[...], SUBLN_EPS) * (1.0 - lam_init)).astype(o_ref.dtype)


def _attn_sample(q, k_new, v_new, cache_k, cache_v, page_table, layer, lams, subln_g, lam_init, *, npg):
    nb = q.shape[0]
    n_pages = page_table.shape[1]
    hspec = pl.BlockSpec((1, H_C, LANES), lambda b, g, pt: (b, 0, 0))
    row = lambda n: pl.BlockSpec((1, n), lambda b, g, pt: (0, 0))

    def page_spec(r):
        return pl.BlockSpec((None, None, PAGE_SIZE, H_C, LANES),
                            lambda b, g, pt: (layer, pt[b, g * npg + r], 0, 0, 0))

    grid_spec = pltpu.PrefetchScalarGridSpec(
        num_scalar_prefetch=1,
        grid=(nb, n_pages // npg),
        in_specs=[hspec, hspec, hspec] + [page_spec(r) for r in range(npg)] * 2
                 + [row(D_HEAD), row(D_HEAD), row(D_HEAD), row(D_HEAD), row(LANES)],
        out_specs=hspec,
        scratch_shapes=[pltpu.VMEM((2 * H_C, 1), F32), pltpu.VMEM((2 * H_C, 1), F32),
                        pltpu.VMEM((2 * H_C, LANES), F32)],
    )
    return pl.pallas_call(
        functools.partial(_attn_sample_kernel, npg=npg, lam_init=lam_init),
        grid_spec=grid_spec,
        out_shape=jax.ShapeDtypeStruct((nb, H_C, LANES), BF16),
        compiler_params=_cparams("parallel", "arbitrary"),
        name="attn_sample",
    )(page_table, q, k_new, v_new, *([cache_k] * npg), *([cache_v] * npg), *lams, subln_g)


def _merge_kernel(x_ref, oa_ref, ob_ref, oc_ref, cg_ref, wa_ref, wb_ref, wc_ref, wo_ref, xo_ref):
    cg = cg_ref[...]
    dot = lambda a, w: jnp.dot(a, w, preferred_element_type=F32)
    m = (jax.nn.sigmoid(cg[:, 0:D_MODEL]) * dot(oa_ref[...], wa_ref[...])
         + jax.nn.sigmoid(cg[:, D_MODEL:2 * D_MODEL]) * dot(ob_ref[...], wb_ref[...])
         + jax.nn.sigmoid(cg[:, 2 * D_MODEL:3 * D_MODEL]) * dot(oc_ref[...], wc_ref[...]))
    xo_ref[...] = x_ref[...] + dot(m.astype(BF16), wo_ref[...])


def _merge(x, oa, ob, oc, cg, wa, wb, wc, wo, *, tm):
    m = x.shape[0]
    rows = lambda n: pl.BlockSpec((tm, n), lambda i: (i, 0))
    full = lambda a: pl.BlockSpec(a.shape, lambda i: (0, 0))
    return pl.pallas_call(
        _merge_kernel,
        grid=(m // tm,),
        in_specs=[rows(D_MODEL), rows(D_A), rows(D_B), rows(D_C), rows(3 * D_MODEL),
                  full(wa), full(wb), full(wc), full(wo)],
        out_specs=rows(D_MODEL),
        out_shape=jax.ShapeDtypeStruct((m, D_MODEL), F32),
        compiler_params=_cparams("parallel"),
        name="merge",
    )(x, oa, ob, oc, cg, wa, wb, wc, wo)


def _ffn_kernel(x_ref, g_ref, wu_ref, wd_ref, gf_ref, o_ref, h_ref, acc_ref, *, final):
    j = pl.program_id(1)

    @pl.when(j == 0)
    def _():
        h_ref[...] = _rms(x_ref[...], g_ref[...], EPS).astype(BF16)
        acc_ref[...] = jnp.zeros_like(acc_ref)

    up = jnp.dot(h_ref[...], wu_ref[...], preferred_element_type=F32)
    act = jnp.square(jnp.maximum(up, 0.0)).astype(BF16)
    acc_ref[...] += jnp.dot(act, wd_ref[...], preferred_element_type=F32)

    @pl.when(j == pl.num_programs(1) - 1)
    def _():
        xn = x_ref[...] + acc_ref[...]
        o_ref[...] = _rms(xn, gf_ref[...], EPS) if final else xn


def _ffn(x, g, wu, wd, gf, *, final, tm, tf):
    m = x.shape[0]
    row = pl.BlockSpec((1, D_MODEL), lambda i, j: (0, 0))
    return pl.pallas_call(
        functools.partial(_ffn_kernel, final=final),
        grid=(m // tm, D_FF // tf),
        in_specs=[pl.BlockSpec((tm, D_MODEL), lambda i, j: (i, 0)), row,
                  pl.BlockSpec((D_MODEL, tf), lambda i, j: (0, j)),
                  pl.BlockSpec((tf, D_MODEL), lambda i, j: (j, 0)), row],
        out_specs=pl.BlockSpec((tm, D_MODEL), lambda i, j: (i, 0)),
        out_shape=jax.ShapeDtypeStruct((m, D_MODEL), F32),
        scratch_shapes=[pltpu.VMEM((tm, D_MODEL), BF16), pltpu.VMEM((tm, D_MODEL), F32)],
        compiler_params=_cparams("parallel", "arbitrary"),
        name="ffn",
    )(x, g, wu, wd, gf)


def _pad_b_cols(a):
    o = 3 * D_B
    z = lambda n: jnp.zeros(a.shape[:-1] + (n,), a.dtype)
    return jnp.concatenate([a[..., :o + LORA_W], z(LW_PAD - LORA_W),
                            a[..., o + LORA_W:o + LORA_W + LORA_A], z(LA_PAD - LORA_A),
                            a[..., o + LORA_W + LORA_A:], z(LG_PAD - LORA_G)], axis=-1)


def _unpad_b_cols(a):
    o = 3 * D_B
    return jnp.concatenate([a[..., :o + LORA_W], a[..., o + LW_PAD:o + LW_PAD + LORA_A],
                            a[..., o + LW_PAD + LA_PAD:o + LW_PAD + LA_PAD + LORA_G]], axis=-1)


def _pad_rows(a, n):
    return jnp.concatenate([a, jnp.zeros((n - a.shape[0],) + a.shape[1:], a.dtype)], axis=0)


def _ones_blockdiag(n, seg):
    i = np.arange(n) // seg
    return jnp.asarray((i[:, None] == i[None, :]).astype(np.float32), dtype=BF16)


def _layer_weights(l, p):
    w_in = p["w_in"][l]
    o_b = N_A_COLS
    o_q = o_b + N_B_COLS
    w = dict(
        norm_mix=p["norm_mix"][l].reshape(1, D_MODEL),
        w_a=w_in[:, :o_b].astype(BF16),
        w_b=_pad_b_cols(w_in[:, o_b:o_q]).astype(BF16),
        w_q=w_in[:, o_q:o_q + D_C].astype(BF16),
        w_k=w_in[:, o_q + D_C:o_q + 2 * D_C].astype(BF16),
        w_v=w_in[:, o_q + 2 * D_C:o_q + 3 * D_C].astype(BF16),
        w_g=w_in[:, o_q + 3 * D_C:].astype(BF16),
        lng=p["sgu_ln_g"][l].reshape(1, D_A), lnb=p["sgu_ln_b"][l].reshape(1, D_A),
        sgu_w=p["sgu_w"][l],
        sgu_bias_td=jnp.repeat(p["sgu_b"][l].T, D_A // G_A, axis=1),
        sgu_w0=jnp.repeat(p["sgu_w"][l][:, 0, 0], D_A // G_A).reshape(1, D_A),
        sgu_b0=jnp.repeat(p["sgu_b"][l][:, 0], D_A // G_A).reshape(1, D_A),
        rwkv=[_pad_b_cols(p["shift_mu"][l]).reshape(1, N_B_PAD),
              p["w0"][l].reshape(1, D_B), _pad_rows(p["w2"][l], LW_PAD).astype(BF16),
              p["a0"][l].reshape(1, D_B), _pad_rows(p["a2"][l], LA_PAD).astype(BF16),
              _pad_rows(p["g2"][l], LG_PAD).astype(BF16),
              p["k_k"][l].reshape(1, D_B), p["k_a"][l].reshape(1, D_B), p["r_k"][l].reshape(1, D_B),
              _ones_blockdiag(D_B, N_B)],
        lnx_g=p["lnx_g"][l].reshape(1, D_B), lnx_b=p["lnx_b"][l].reshape(1, D_B),
        lams=[p[n][l].reshape(1, D_HEAD) for n in ("lam_q1", "lam_k1", "lam_q2", "lam_k2")],
        subln_g=p["subln_g"][l].reshape(1, 2 * D_HEAD),
        w_br_a=p["w_br_a"][l].astype(BF16), w_br_b=p["w_br_b"][l].astype(BF16),
        w_br_c=p["w_br_c"][l].astype(BF16), w_out=p["w_out"][l].astype(BF16),
        norm_ffn=p["norm_ffn"][l].reshape(1, D_MODEL),
        w_up=p["w_up"][l].astype(BF16), w_down=p["w_down"][l].astype(BF16),
        norm_final=p["norm_final"].reshape(1, D_MODEL),
    )
    return w


def _project_all(x, w, rope_tabs, tm, t_seq=None):
    pj = functools.partial(_proj, x, w["norm_mix"], tm=tm, tn=D_MODEL)
    qscale = math.log2(math.e) / math.sqrt(D_HEAD)
    (ca,) = pj(w["w_a"], out_dtypes=(F32,), name="proj_a")
    (cb,) = pj(w["w_b"], out_dtypes=(F32,), name="proj_b")
    (cg,) = pj(w["w_g"], out_dtypes=(F32,), name="proj_g")
    if t_seq is None:
        (q,) = pj(w["w_q"], out_dtypes=(F32,), rope_tabs=rope_tabs, scale=qscale, name="proj_q")
        (kf,) = pj(w["w_k"], out_dtypes=(F32,), rope_tabs=rope_tabs, name="proj_k")
        (vf,) = pj(w["w_v"], out_dtypes=(F32,), name="proj_v")
        return ca, cb, cg, q, kf, vf
    (q,) = pj(w["w_q"], out_dtypes=(BF16,), rope_tabs=rope_tabs, scale=qscale, name="proj_q")
    kf, kb = pj(w["w_k"], out_dtypes=(F32, BF16), rope_tabs=rope_tabs, name="proj_k")
    vf, vt = _proj(x, w["norm_mix"], w["w_v"], tm=ATTN_BLOCK, tn=D_MODEL, out_dtypes=(F32, BF16), t_seq=t_seq,
                   name="proj_v")
    return ca, cb, cg, q, kf, vf, kb, vt


def _prompt_layer(x, l, w, rope_tabs, nb, t, final):
    lam_init = 0.8 - 0.6 * math.exp(-0.3 * l)
    ca, cb, cg, qb, kf, vf, kb, vt = _project_all(x, w, rope_tabs, 1024, t_seq=t)
    o_a = _sgu_prompt(ca, w["lng"], w["lnb"], w["sgu_w"], w["sgu_bias_td"], tm=512)
    prev0 = jnp.zeros((nb, 1, N_B_PAD), F32)
    *rw, last = _rwkv_pre(cb.reshape(nb, t, N_B_PAD), prev0, w["rwkv"], seq=True, tm=256)
    r, dec, k2, v, na, bb, g, rkv = rw
    y, s_fin = _rwkv_scan_seq(r, dec, k2, v, na, bb, tc=64)
    flat = lambda a: a.reshape(nb * t, D_B)
    o_b = _rwkv_post(flat(y), flat(rkv), flat(g), w["lnx_g"], w["lnx_b"], w["rwkv"][-1], tm=1024)
    b3 = lambda a: a.reshape(nb, t, D_C)
    o_c = _attn_prompt(b3(qb), b3(kb), vt, w["lams"], w["subln_g"], lam_init, tq=ATTN_BLOCK)
    x = _merge(x, o_a, o_b, o_c.reshape(nb * t, D_C), cg, w["w_br_a"], w["w_br_b"], w["w_br_c"], w["w_out"], tm=256)
    x = _ffn(x, w["norm_ffn"], w["w_up"], w["w_down"], w["norm_final"], final=final, tm=1024, tf=1024)
    return x, kf, vf, s_fin, _unpad_b_cols(last[:, 0, :])


def _sample_layer(x, l, w, rope_tabs, cache_k, cache_v, page_table, prev_shift, s0, final):
    nb = x.shape[0]
    lam_init = 0.8 - 0.6 * math.exp(-0.3 * l)
    ca, cb, cg, qb, kf, vf = _project_all(x, w, rope_tabs, nb)
    o_a, va = _sgu_sample(ca, w["lng"], w["lnb"], w["sgu_w0"], w["sgu_b0"])
    r, dec, k2, v, na, bb, g, rkv = _rwkv_pre(cb, _pad_b_cols(prev_shift), w["rwkv"], seq=False, tm=nb)
    y, s_new = _rwkv_scan_rows(r, dec, k2, v, na, bb, s0, tb=32)
    o_b = _rwkv_post(y, rkv, g, w["lnx_g"], w["lnx_b"], w["rwkv"][-1], tm=nb)
    h3 = lambda a: a.reshape(nb, H_C, LANES)
    o_c = _attn_sample(h3(qb), h3(kf), h3(vf), cache_k, cache_v, page_table, l,
                       w["lams"], w["subln_g"], lam_init, npg=8)
    x = _merge(x, o_a, o_b, o_c.reshape(nb, D_C), cg, w["w_br_a"], w["w_br_b"], w["w_br_c"], w["w_out"], tm=nb)
    x = _ffn(x, w["norm_ffn"], w["w_up"], w["w_down"], w["norm_final"], final=final, tm=nb, tf=1024)
    return x, kf, vf, s_new, _unpad_b_cols(cb), va


def kernel(x_prompt, x_sample, cache_k, cache_v, state_rwkv, state_shift, page_table, norm_mix, w_in, sgu_ln_g, sgu_ln_b, sgu_w, sgu_b, shift_mu, w0, w2, a0, a2, g2, k_k, k_a, r_k, lnx_g, lnx_b, lam_q1, lam_k1, lam_q2, lam_k2, subln_g, w_br_a, w_br_b, w_br_c, w_out, norm_ffn, w_up, w_down, norm_final):
    p = dict(norm_mix=norm_mix, w_in=w_in, sgu_ln_g=sgu_ln_g, sgu_ln_b=sgu_ln_b, sgu_w=sgu_w, sgu_b=sgu_b,
             shift_mu=shift_mu, w0=w0, w2=w2, a0=a0, a2=a2, g2=g2, k_k=k_k, k_a=k_a, r_k=r_k, lnx_g=lnx_g,
             lnx_b=lnx_b, lam_q1=lam_q1, lam_k1=lam_k1, lam_q2=lam_q2, lam_k2=lam_k2, subln_g=subln_g,
             w_br_a=w_br_a, w_br_b=w_br_b, w_br_c=w_br_c, w_out=w_out, norm_ffn=norm_ffn, w_up=w_up,
             w_down=w_down, norm_final=norm_final)
    depth = w_in.shape[0]
    bp, tp, _ = x_prompt.shape
    bs, ts, _ = x_sample.shape
    tabs_p = _rope_tables(jnp.arange(tp))
    tabs_s = _rope_tables(jnp.full((bs * ts,), PAST_LEN, jnp.int32))
    xp = x_prompt.reshape(bp * tp, D_MODEL)
    xs = x_sample.reshape(bs * ts, D_MODEL)
    outs_p, outs_s = [], []
    for l in range(depth):
        w = _layer_weights(l, p)
        final = l == depth - 1
        xp, kp, vp, sp, shp = _prompt_layer(xp, l, w, tabs_p, bp, tp, final)
        xs, ks, vs, ss, shs, va = _sample_layer(xs, l, w, tabs_s, cache_k, cache_v, page_table,
                                                state_shift[l], state_rwkv[l], final)
        outs_p.append((kp.reshape(bp, tp, H_C, 2 * D_HEAD), vp.reshape(bp, tp, H_C, 2 * D_HEAD), sp, shp))
        outs_s.append((ks.reshape(bs, ts, H_C, 2 * D_HEAD), vs.reshape(bs, ts, H_C, 2 * D_HEAD), ss, shs,
                       va.reshape(bs, ts, D_A)))
    stack = lambda outs, i: jnp.stack([o[i] for o in outs])
    return (xp.reshape(bp, tp, D_MODEL), xs.reshape(bs, ts, D_MODEL),
            stack(outs_p, 0), stack(outs_p, 1), stack(outs_p, 2), stack(outs_p, 3),
            stack(outs_s, 0), stack(outs_s, 1), stack(outs_s, 2), stack(outs_s, 3), stack(outs_s, 4))
```

```python
import functools
import math

import numpy as np
import jax
import jax.numpy as jnp
from jax import lax
from jax.experimental import pallas as pl
from jax.experimental.pallas import tpu as pltpu

F32 = jnp.float32
BF16 = jnp.bfloat16

D_MODEL = 1024
PAST_LEN = 2048
PAGE_SIZE = 128
CHUNK = 128
D_A = 512
G_A = 8
H_B = 8
N_B = 64
D_B = H_B * N_B
LORA_W = 64
LORA_A = 64
LORA_G = 160
GN_EPS_B = 64e-5
H_C = 8
D_HEAD = 64
D_C = H_C * 2 * D_HEAD
ROPE_DIM = D_HEAD // 4
ROPE_THETA = 500000.0
SUBLN_EPS = 1e-5
D_FF = 4 * D_MODEL
EPS = 1e-6
N_A_COLS = 2 * D_A
N_B_COLS = 3 * D_B + LORA_W + LORA_A + LORA_G
LANES = 128
SUBLANES = 8
LW_PAD = 128
LA_PAD = 128
LG_PAD = 256
N_B_PAD = 3 * D_B + LW_PAD + LA_PAD + LG_PAD
VMEM_LIMIT = 52 * 1024 * 1024
ATTN_BLOCK = 512
ATTN_GROUP = 4


def _cparams(*sem):
    return pltpu.CompilerParams(dimension_semantics=sem, vmem_limit_bytes=VMEM_LIMIT)


def _rms(x, g, eps):
    return x * lax.rsqrt(jnp.mean(x * x, axis=-1, keepdims=True) + eps) * g


def _proj_kernel(*refs, rope, scale, n_out, aliased):
    x_ref, g_ref, w_ref = refs[:3]
    pos = 3
    if rope:
        c_ref, s1_ref, s2_ref = refs[3:6]
        pos = 6
    pos += aliased
    out_refs = refs[pos:pos + n_out]
    h_ref = refs[pos + n_out]

    @pl.when(pl.program_id(1) == 0)
    def _():
        h_ref[...] = _rms(x_ref[...], g_ref[...], EPS).astype(BF16)

    acc = jnp.dot(h_ref[...], w_ref[...], preferred_element_type=F32)
    if rope:
        c, s1, s2 = c_ref[...], s1_ref[...], s2_ref[...]
        parts = []
        for hh in range(acc.shape[1] // LANES):
            a = acc[:, hh * LANES:(hh + 1) * LANES]
            parts.append(a * c + pltpu.roll(a, ROPE_DIM // 2, 1) * s1
                         + pltpu.roll(a, LANES - ROPE_DIM // 2, 1) * s2)
        acc = jnp.concatenate(parts, axis=1)
    if scale != 1.0:
        acc = acc * scale
    for o_ref in out_refs:
        if len(o_ref.shape) == 4:
            o_ref[0, 0] = acc.T.astype(o_ref.dtype)
        else:
            o_ref[...] = acc.astype(o_ref.dtype)


def _proj(x, g, w, *, tm, tn, out_dtypes, rope_tabs=None, scale=1.0, t_seq=None, stack=None, name="proj"):
    m, d = x.shape
    n = w.shape[1]
    in_specs = [pl.BlockSpec((tm, d), lambda i, j: (i, 0)),
                pl.BlockSpec((1, d), lambda i, j: (0, 0)),
                pl.BlockSpec((d, tn), lambda i, j: (0, j))]
    args = [x, g, w]
    if rope_tabs is not None:
        nt = rope_tabs[0].shape[0] // tm
        in_specs += [pl.BlockSpec((tm, LANES), lambda i, j: (i % nt, 0))] * 3
        args += list(rope_tabs)
    out_specs = [pl.BlockSpec((tm, tn), lambda i, j: (i, j)) for _ in out_dtypes]
    out_shape = [jax.ShapeDtypeStruct((m, n), dt) for dt in out_dtypes]
    if t_seq is not None:
        nts = t_seq // tm
        out_specs[-1] = pl.BlockSpec((1, 1, tn, tm), lambda i, j: (i // nts, i % nts, j, 0))
        out_shape[-1] = jax.ShapeDtypeStruct((m // t_seq, nts, n, tm), out_dtypes[-1])
    aliases = {}
    if stack is not None:
        depth, layer, prev = stack
        off = layer * (m // tm)
        out_specs[0] = pl.BlockSpec((tm, tn), lambda i, j: (i + off, j))
        out_shape[0] = jax.ShapeDtypeStruct((depth * m, n), out_dtypes[0])
        if prev is not None:
            aliases = {len(args): 0}
            in_specs.append(pl.BlockSpec(memory_space=pl.ANY))
            args.append(prev)
    outs = pl.pallas_call(
        functools.partial(_proj_kernel, rope=rope_tabs is not None, scale=scale, n_out=len(out_dtypes),
                          aliased=len(aliases)),
        grid=(m // tm, n // tn),
        in_specs=in_specs,
        out_specs=out_specs,
        out_shape=out_shape,
        input_output_aliases=aliases,
        scratch_shapes=[pltpu.VMEM((tm, d), BF16)],
        compiler_params=_cparams("parallel", "arbitrary"),
        name=name,
    )(*args)
    return outs


def _rope_kernel(pos_ref, invf_ref, c_ref, s1_ref, s2_ref):
    ang = pos_ref[...] * invf_ref[...]
    lane = lax.broadcasted_iota(jnp.int32, ang.shape, 1) % D_HEAD
    first = lane < ROPE_DIM // 2
    second = (lane >= ROPE_DIM // 2) & (lane < ROPE_DIM)
    cos, sin = jnp.cos(ang), jnp.sin(ang)
    c_ref[...] = jnp.where(first | second, cos, 1.0)
    s1_ref[...] = jnp.where(second, sin, 0.0)
    s2_ref[...] = jnp.where(first, -sin, 0.0)


def _rope_tables(pos):
    t = pos.shape[0]
    half = ROPE_DIM // 2
    inv_freq = ROPE_THETA ** (-jnp.arange(half, dtype=F32) / half)
    blk = jnp.concatenate([inv_freq, inv_freq, jnp.zeros((D_HEAD - ROPE_DIM,), F32)])
    invf = jnp.concatenate([blk, blk]).reshape(1, LANES)
    tm = min(t, 512)
    return pl.pallas_call(
        _rope_kernel,
        grid=(t // tm,),
        in_specs=[pl.BlockSpec((tm, 1), lambda i: (i, 0)), pl.BlockSpec((1, LANES), lambda i: (0, 0))],
        out_specs=[pl.BlockSpec((tm, LANES), lambda i: (i, 0))] * 3,
        out_shape=[jax.ShapeDtypeStruct((t, LANES), F32)] * 3,
        compiler_params=_cparams("parallel"),
        name="rope_tables",
    )(pos.astype(F32).reshape(t, 1), invf)


def _gelu_ln(ca, lng, lnb):
    gx = 0.5 * ca * (1.0 + lax.erf(ca * math.sqrt(0.5)))
    u, v = gx[:, :D_A], gx[:, D_A:]
    d = v - jnp.mean(v, axis=-1, keepdims=True)
    va = d * lax.rsqrt(jnp.mean(d * d, axis=-1, keepdims=True) + EPS) * lng + lnb
    return u, va


def _sgu_prompt_kernel(ca_ref, lng_ref, lnb_ref, w_ref, b_ref, oa_ref, *, nchunk):
    u, va = _gelu_ln(ca_ref[...], lng_ref[...], lnb_ref[...])
    vab = va.astype(BF16)
    row = lax.broadcasted_iota(jnp.int32, (CHUNK, CHUNK), 0)
    col = lax.broadcasted_iota(jnp.int32, (CHUNK, CHUNK), 1)
    wcat = jnp.concatenate([jnp.where(col <= row, w_ref[g], 0.0).astype(BF16) for g in range(G_A)], axis=1)
    grp = lax.broadcasted_iota(jnp.int32, (CHUNK, D_A), 1) // (D_A // G_A)
    for ci in range(nchunk):
        vc = vab[ci * CHUNK:(ci + 1) * CHUNK]
        vbig = jnp.concatenate([jnp.where(grp == g, vc, jnp.zeros_like(vc)) for g in range(G_A)], axis=0)
        s = jnp.dot(wcat, vbig, preferred_element_type=F32) + b_ref[...]
        oa_ref[ci * CHUNK:(ci + 1) * CHUNK, :] = (u[ci * CHUNK:(ci + 1) * CHUNK] * s).astype(oa_ref.dtype)


def _sgu_prompt(ca, lng, lnb, w, bias_td, *, tm):
    m = ca.shape[0]
    return pl.pallas_call(
        functools.partial(_sgu_prompt_kernel, nchunk=tm // CHUNK),
        grid=(m // tm,),
        in_specs=[pl.BlockSpec((tm, N_A_COLS), lambda i: (i, 0)),
                  pl.BlockSpec((1, D_A), lambda i: (0, 0)),
                  pl.BlockSpec((1, D_A), lambda i: (0, 0)),
                  pl.BlockSpec((G_A, CHUNK, CHUNK), lambda i: (0, 0, 0)),
                  pl.BlockSpec((CHUNK, D_A), lambda i: (0, 0))],
        out_specs=pl.BlockSpec((tm, D_A), lambda i: (i, 0)),
        out_shape=jax.ShapeDtypeStruct((m, D_A), BF16),
        compiler_params=_cparams("parallel"),
        name="sgu_prompt",
    )(ca, lng, lnb, w, bias_td)


def _sgu_sample_kernel(ca_ref, lng_ref, lnb_ref, w0_ref, b0_ref, oa_ref, va_ref):
    u, va = _gelu_ln(ca_ref[...], lng_ref[...], lnb_ref[...])
    va_ref[...] = va
    oa_ref[...] = (u * (va * w0_ref[...] + b0_ref[...])).astype(oa_ref.dtype)


def _sgu_sample(ca, lng, lnb, w0, b0):
    m = ca.shape[0]
    row = lambda n: pl.BlockSpec((1, n), lambda i: (0, 0))
    return pl.pallas_call(
        _sgu_sample_kernel,
        grid=(1,),
        in_specs=[pl.BlockSpec((m, N_A_COLS), lambda i: (0, 0)), row(D_A), row(D_A), row(D_A), row(D_A)],
        out_specs=[pl.BlockSpec((m, D_A), lambda i: (0, 0))] * 2,
        out_shape=[jax.ShapeDtypeStruct((m, D_A), BF16), jax.ShapeDtypeStruct((m, D_A), F32)],
        compiler_params=_cparams("arbitrary"),
        name="sgu_sample",
    )(ca, lng, lnb, w0, b0)


def _seg_sum(x, ones_bd):
    hi = x.astype(BF16)
    lo = (x - hi.astype(F32)).astype(BF16)
    return (jnp.dot(hi, ones_bd, preferred_element_type=F32)
            + jnp.dot(lo, ones_bd, preferred_element_type=F32))


def _rwkv_pre_kernel(*refs, seq):
    if seq:
        c_ref, prev_ref = refs[:2]
    else:
        c_ref, sh_ref = refs[:2]
    (mu_ref, w0_ref, w2_ref, a0_ref, a2_ref, g2_ref, kk_ref, ka_ref, rk_ref, bd_ref) = refs[2:12]
    (r_o, w_o, k_o, v_o, a_o, b_o, g_o, rkv_o) = refs[12:20]
    if seq:
        last_o, carry_ref = refs[20:22]
        cols = c_ref[0]
        tm = cols.shape[0]

        @pl.when(pl.program_id(1) == 0)
        def _():
            carry_ref[...] = prev_ref[0]

        rolled = pltpu.roll(cols, 1, 0)
        first = lax.broadcasted_iota(jnp.int32, cols.shape, 0) == 0
        shifted = jnp.where(first, carry_ref[...], rolled)
        carry_ref[...] = cols[tm - 1:tm, :]
        last_o[0] = cols[tm - 1:tm, :]
    else:
        cols = c_ref[...]
        shifted = sh_ref[...]
    xs = cols + (shifted - cols) * mu_ref[...]
    r = xs[:, 0:D_B]
    k = xs[:, D_B:2 * D_B]
    v = xs[:, 2 * D_B:3 * D_B]
    o = 3 * D_B
    wl = xs[:, o:o + LW_PAD]
    al = xs[:, o + LW_PAD:o + LW_PAD + LA_PAD]
    gl = xs[:, o + LW_PAD + LA_PAD:o + LW_PAD + LA_PAD + LG_PAD]
    z = -(w0_ref[...] + jnp.dot(jnp.tanh(wl).astype(BF16), w2_ref[...], preferred_element_type=F32))
    softplus = jnp.maximum(z, 0.0) + jnp.log1p(jnp.exp(-jnp.abs(z)))
    log_decay = -jnp.exp(-softplus - 0.5)
    decay = log_decay if seq else jnp.exp(log_decay)
    a = jax.nn.sigmoid(a0_ref[...] + jnp.dot(al.astype(BF16), a2_ref[...], preferred_element_type=F32))
    g = jnp.dot(jax.nn.sigmoid(gl).astype(BF16), g2_ref[...], preferred_element_type=F32)
    bd = bd_ref[...]
    kk = k * kk_ref[...]
    kk = kk / jnp.maximum(jnp.sqrt(_seg_sum(kk * kk, bd)), 1e-12)
    k2 = k * (1.0 + (a - 1.0) * ka_ref[...])
    if seq:
        r_o[0], w_o[0], k_o[0], v_o[0], a_o[0], b_o[0], g_o[0] = r, decay, k2, v, -kk, kk * a, g
        rkv_o[0] = _seg_sum(r * k2 * rk_ref[...], bd) * v
    else:
        r_o[...], w_o[...], k_o[...], v_o[...], a_o[...], b_o[...], g_o[...] = r, decay, k2, v, -kk, kk * a, g
        rkv_o[...] = _seg_sum(r * k2 * rk_ref[...], bd) * v


def _rwkv_pre(cols, shift_src, params, *, seq, tm):
    prm_specs = []
    for p in params:
        prm_specs.append(pl.BlockSpec(p.shape, (lambda b, i: (0, 0)) if seq else (lambda i: (0, 0))))
    if seq:
        nb, t, _ = cols.shape
        grid = (nb, t // tm)
        in_specs = [pl.BlockSpec((1, tm, N_B_PAD), lambda b, i: (b, i, 0)),
                    pl.BlockSpec((1, 1, N_B_PAD), lambda b, i: (b, 0, 0))] + prm_specs
        ospec = pl.BlockSpec((1, tm, D_B), lambda b, i: (b, i, 0))
        oshape = jax.ShapeDtypeStruct((nb, t, D_B), F32)
        out_specs = [ospec] * 8 + [pl.BlockSpec((1, 1, N_B_PAD), lambda b, i: (b, 0, 0))]
        out_shape = [oshape] * 8 + [jax.ShapeDtypeStruct((nb, 1, N_B_PAD), F32)]
        scratch = [pltpu.VMEM((1, N_B_PAD), F32)]
        sem = ("parallel", "arbitrary")
    else:
        m = cols.shape[0]
        grid = (m // tm,)
        in_specs = [pl.BlockSpec((tm, N_B_PAD), lambda i: (i, 0))] * 2 + prm_specs
        out_specs = [pl.BlockSpec((tm, D_B), lambda i: (i, 0))] * 8
        out_shape = [jax.ShapeDtypeStruct((m, D_B), F32)] * 8
        scratch = []
        sem = ("parallel",)
    return pl.pallas_call(
        functools.partial(_rwkv_pre_kernel, seq=seq),
        grid=grid, in_specs=in_specs, out_specs=out_specs, out_shape=out_shape,
        scratch_shapes=scratch, compiler_params=_cparams(*sem),
        name="rwkv_pre_seq" if seq else "rwkv_pre_rows",
    )(cols, shift_src, *params)


def _scan_pair_step(S, a, w, b, k, v, r):
    lane = lax.broadcasted_iota(jnp.int32, (N_B, LANES), 1)
    row = lax.broadcasted_iota(jnp.int32, (N_B, LANES), 0)
    lo = lane < N_B
    e1 = lane == row
    e2 = lane == row + N_B

    def half_sums(p):
        t1 = jnp.sum(jnp.where(lo, p, 0.0), axis=1, keepdims=True)
        t2 = jnp.sum(jnp.where(lo, 0.0, p), axis=1, keepdims=True)
        return t1, t2

    t1, t2 = half_sums(S * a)
    sa = jnp.where(lo, t1, t2)
    vb = jnp.where(lo, jnp.sum(jnp.where(e1, v, 0.0), axis=1, keepdims=True),
                   jnp.sum(jnp.where(e2, v, 0.0), axis=1, keepdims=True))
    S = S * w + sa * b + vb * k
    y1, y2 = half_sums(S * r)
    y = jnp.sum(jnp.where(e1, y1, 0.0) + jnp.where(e2, y2, 0.0), axis=0, keepdims=True)
    return S, y


def _split(x):
    hi = x.astype(BF16)
    return hi, (x - hi.astype(F32)).astype(BF16)


def _mm3(xs, ys, nt=False):
    lhs = jnp.concatenate([xs[0], xs[0], xs[1]], axis=1)
    if nt:
        rhs = jnp.concatenate([ys[0], ys[1], ys[0]], axis=1)
        return lax.dot_general(lhs, rhs, (((1,), (1,)), ((), ())), preferred_element_type=F32)
    rhs = jnp.concatenate([ys[0], ys[1], ys[0]], axis=0)
    return jnp.dot(lhs, rhs, preferred_element_type=F32)


def _mm1(x, y, nt=False):
    if nt:
        return lax.dot_general(x, y, (((1,), (1,)), ((), ())), preferred_element_type=F32)
    return jnp.dot(x, y, preferred_element_type=F32)


def _block_diag(x):
    lo = lax.broadcasted_iota(jnp.int32, x.shape, 1) < N_B
    zero = jnp.zeros_like(x)
    return jnp.concatenate([jnp.where(lo, x, zero), jnp.where(lo, zero, x)], axis=0)


def _each(fn, *lists):
    return [fn(*args) for args in zip(*lists)]


def _rwkv_chunk_pairs(zs, lws, a_s, bs, ks, vs, rs):
    c = lws[0].shape[0]
    tri = (lax.broadcasted_iota(jnp.int32, (c, c), 0) >= lax.broadcasted_iota(jnp.int32, (c, c), 1)).astype(BF16)
    tri3 = jnp.concatenate([tri, tri, tri], axis=1)

    def cumsum(lw):
        l1 = lw.astype(BF16)
        r1 = lw - l1.astype(F32)
        l2 = r1.astype(BF16)
        l3 = (r1 - l2.astype(F32)).astype(BF16)
        return jnp.dot(tri3, jnp.concatenate([l1, l2, l3], axis=0), preferred_element_type=F32)

    lam = _each(cumsum, lws)
    lam_c = [x[c - 1:c, :] for x in lam]
    w_inv = [jnp.exp(-x) for x in lam]
    w_rem = _each(lambda lc, x: jnp.exp(lc - x), lam_c, lam)
    r_h = _each(lambda r, x: _block_diag(r * jnp.exp(x)), rs, lam)
    s_a = _each(lambda a, x, lw: _split(_block_diag(a * jnp.exp(x - lw))), a_s, lam, lws)
    h_r = [x.astype(BF16) for x in r_h]
    s_b = _each(lambda b, wi: _split(_block_diag(b * wi)), bs, w_inv)
    s_k = _each(lambda k, wi: _split(_block_diag(k * wi)), ks, w_inv)
    s_v = [_split(_block_diag(v)) for v in vs]
    n = 2 * c
    ri = lax.broadcasted_iota(jnp.int32, (n, n), 0)
    ci = lax.broadcasted_iota(jnp.int32, (n, n), 1)
    strict, incl, eye = ci < ri, ci <= ri, ci == ri
    gram = lambda x, y: jnp.where(strict, _mm3(x, y, nt=True), 0.0)
    lab = _each(gram, s_a, s_b)
    lak = _each(gram, s_a, s_k)
    h_lrb = _each(lambda x, y: jnp.where(incl, _mm1(x, y[0], nt=True), 0.0).astype(BF16), h_r, s_b)
    h_lrk = _each(lambda x, y: jnp.where(incl, _mm1(x, y[0], nt=True), 0.0).astype(BF16), h_r, s_k)
    t = [jnp.where(eye, 1.0, 0.0) + x for x in lab]
    s_p = _each(_split, lab)
    for _ in range(int(math.log2(c)) - 1):
        s_p = _each(lambda p: _split(_mm3(p, p)), s_p)
        t = _each(lambda tt, p: tt + _mm3(_split(tt), p), t, s_p)
    s_t = _each(_split, t)
    s_at = _each(lambda tt, a: _split(_mm3(tt, a)), s_t, s_a)
    x2 = _each(lambda l, v: _split(_mm3(_split(l), v)), lak, s_v)
    s_u0 = _each(lambda tt, x: _split(_mm3(tt, x)), s_t, x2)
    rt = _each(lambda rh, l, at: rh + _mm1(l, at[0]), r_h, h_lrb, s_at)
    y0 = _each(lambda l, u, lk, v: _mm1(l, u[0]) + _mm1(lk, v[0]), h_lrb, s_u0, h_lrk, s_v)
    s_bt = _each(lambda b, wr: _split(_block_diag(b * wr).T), bs, w_rem)
    s_kt = _each(lambda k, wr: _split(_block_diag(k * wr).T), ks, w_rem)
    m = _each(lambda lc, bt, at: jnp.where(eye, jnp.exp(lc), 0.0) + _mm3(bt, at), lam_c, s_bt, s_at)
    nn = _each(lambda bt, u, kt, v: _mm3(bt, u) + _mm3(kt, v), s_bt, s_u0, s_kt, s_v)
    s_z = _each(_split, zs)
    ybd = _each(lambda r, z, y: _mm1(r.astype(BF16), z[0]) + y, rt, s_z, y0)
    z_new = _each(lambda mm, z, x: _mm3(_split(mm), z) + x, m, s_z, nn)
    return z_new, [y[:c] + y[c:] for y in ybd]


def _rwkv_scan_seq_kernel(r_ref, w_ref, k_ref, v_ref, a_ref, b_ref, y_ref, sf_ref, z_ref, *, nb):
    npair = H_B // 2
    ci = pl.program_id(0)

    @pl.when(ci == 0)
    def _():
        z_ref[...] = jnp.zeros_like(z_ref)

    sls = [(bi, slice(None), slice(p * LANES, (p + 1) * LANES)) for bi in range(nb) for p in range(npair)]
    rows = lambda ref: [ref[sl] for sl in sls]
    zs, ys = _rwkv_chunk_pairs([z_ref[c] for c in range(len(sls))], rows(w_ref), rows(a_ref),
                               rows(b_ref), rows(k_ref), rows(v_ref), rows(r_ref))
    for c, sl in enumerate(sls):
        z_ref[c] = zs[c]
        y_ref[sl] = ys[c]

    @pl.when(ci == pl.num_programs(0) - 1)
    def _():
        for c, (bi, _, _) in enumerate(sls):
            st = z_ref[c].T
            sf_ref[bi, 2 * (c % npair)] = st[:N_B, :N_B]
            sf_ref[bi, 2 * (c % npair) + 1] = st[N_B:, N_B:]


def _rwkv_scan_seq(r, lw, k, v, a, b, *, tc):
    nb, t, _ = r.shape
    spec = pl.BlockSpec((nb, tc, D_B), lambda c: (0, c, 0))
    return pl.pallas_call(
        functools.partial(_rwkv_scan_seq_kernel, nb=nb),
        grid=(t // tc,),
        in_specs=[spec] * 6,
        out_specs=[spec, pl.BlockSpec((nb, H_B, N_B, N_B), lambda c: (0, 0, 0, 0))],
        out_shape=[jax.ShapeDtypeStruct((nb, t, D_B), F32), jax.ShapeDtypeStruct((nb, H_B, N_B, N_B), F32)],
        scratch_shapes=[pltpu.VMEM((nb * H_B // 2, LANES, LANES), F32)],
        compiler_params=_cparams("arbitrary"),
        name="rwkv_scan_seq",
    )(r, lw, k, v, a, b)


def _rwkv_scan_rows_kernel(r_ref, w_ref, k_ref, v_ref, a_ref, b_ref, s0_ref, y_ref, sf_ref, *, tb):
    def group(gi, carry):
        base = pl.multiple_of(gi * SUBLANES, SUBLANES)
        for p in range(H_B // 2):
            sl = (pl.ds(base, SUBLANES), slice(p * LANES, (p + 1) * LANES))
            a, w, b, k, v, r = (ref[sl] for ref in (a_ref, w_ref, b_ref, k_ref, v_ref, r_ref))
            ys = []
            for j in range(SUBLANES):
                row = slice(j, j + 1)
                S0 = jnp.concatenate([s0_ref[base + j, 2 * p], s0_ref[base + j, 2 * p + 1]], axis=1)
                S, y = _scan_pair_step(S0, a[row], w[row], b[row], k[row], v[row], r[row])
                sf_ref[base + j, 2 * p] = S[:, :N_B]
                sf_ref[base + j, 2 * p + 1] = S[:, N_B:]
                ys.append(y)
            y_ref[sl] = jnp.concatenate(ys, axis=0)
        return carry

    lax.fori_loop(0, tb // SUBLANES, group, 0)


def _rwkv_scan_rows(r, w, k, v, a, b, s0, *, tb):
    m = r.shape[0]
    spec = pl.BlockSpec((tb, D_B), lambda i: (i, 0))
    sspec = pl.BlockSpec((tb, H_B, N_B, N_B), lambda i: (i, 0, 0, 0))
    return pl.pallas_call(
        functools.partial(_rwkv_scan_rows_kernel, tb=tb),
        grid=(m // tb,),
        in_specs=[spec] * 6 + [sspec],
        out_specs=[spec, sspec],
        out_shape=[jax.ShapeDtypeStruct((m, D_B), F32), jax.ShapeDtypeStruct((m, H_B, N_B, N_B), F32)],
        compiler_params=_cparams("parallel"),
        name="rwkv_scan_rows",
    )(r, w, k, v, a, b, s0)


def _rwkv_post_kernel(y_ref, rkv_ref, g_ref, lg_ref, lb_ref, bd_ref, o_ref):
    y = y_ref[...]
    bd = bd_ref[...]
    d = y - _seg_sum(y, bd) * (1.0 / N_B)
    var = _seg_sum(d * d, bd) * (1.0 / N_B)
    yn = d * lax.rsqrt(var + GN_EPS_B) * lg_ref[...] + lb_ref[...]
    o_ref[...] = ((yn + rkv_ref[...]) * g_ref[...]).astype(o_ref.dtype)


def _rwkv_post(y, rkv, g, lg, lb, bd, *, tm):
    m = y.shape[0]
    spec = pl.BlockSpec((tm, D_B), lambda i: (i, 0))
    row = pl.BlockSpec((1, D_B), lambda i: (0, 0))
    return pl.pallas_call(
        _rwkv_post_kernel,
        grid=(m // tm,),
        in_specs=[spec, spec, spec, row, row, pl.BlockSpec((D_B, D_B), lambda i: (0, 0))],
        out_specs=spec,
        out_shape=jax.ShapeDtypeStruct((m, D_B), BF16),
        compiler_params=_cparams("parallel"),
        name="rwkv_post",
    )(y, rkv, g, lg, lb, bd)


def _lam_full(lq1, lk1, lq2, lk2, lam_init):
    return (jnp.exp(jnp.sum(lq1 * lk1, axis=-1, keepdims=True))
            - jnp.exp(jnp.sum(lq2 * lk2, axis=-1, keepdims=True)) + lam_init)


ONES_ROWS = 16


def _attn_prompt_kernel(q_ref, k_ref, vt_ref, lq1, lk1, lq2, lk2, sg_ref, o_ref, m1, a1, m2, a2, *, tk, lam_init):
    qi = pl.program_id(2)
    for m_ref, a_ref in ((m1, a1), (m2, a2)):
        m_ref[...] = jnp.full_like(m_ref, -jnp.inf)
        a_ref[...] = jnp.zeros_like(a_ref)

    q = q_ref[0]
    lane = lax.broadcasted_iota(jnp.int32, q.shape, 1)
    zero = jnp.zeros_like(q)
    q_sub = (jnp.where(lane < D_HEAD, q, zero), jnp.where(lane < D_HEAD, zero, q))
    ones = jnp.ones((ONES_ROWS, tk), BF16)

    def update(kis, masked):
        ks = [k_ref[0, pl.ds(pl.multiple_of(ki * tk, tk), tk), :] for ki in kis]
        vts = [jnp.concatenate([vt_ref[0, ki], ones], axis=0) for ki in kis]
        sts = [[lax.dot_general(k, qs, (((1,), (1,)), ((), ())), preferred_element_type=F32)
                for k in ks] for qs in q_sub]
        for st_blocks, m_ref, a_ref in ((sts[0], m1, a1), (sts[1], m2, a2)):
            if masked:
                kr = lax.broadcasted_iota(jnp.int32, st_blocks[0].shape, 0)
                qc = lax.broadcasted_iota(jnp.int32, st_blocks[0].shape, 1)
                st_blocks = [jnp.where(kr <= qc, st, -jnp.inf) for st in st_blocks]
            m_old = m_ref[...]
            m_new = m_old
            for st in st_blocks:
                m_new = jnp.maximum(m_new, jnp.max(st, axis=0, keepdims=True))
            acc = jnp.exp2(m_old - m_new) * a_ref[...]
            for st, vt1 in zip(st_blocks, vts):
                acc = acc + jnp.dot(vt1, jnp.exp2(st - m_new).astype(BF16), preferred_element_type=F32)
            a_ref[...] = acc
            m_ref[...] = m_new

    def group_body(j, carry):
        update([ATTN_GROUP * j + i for i in range(ATTN_GROUP)], False)
        return carry

    def single_body(ki, carry):
        update([ki], False)
        return carry

    n_groups = lax.shift_right_logical(qi, ATTN_GROUP.bit_length() - 1)
    lax.fori_loop(0, n_groups, group_body, 0)
    lax.fori_loop(n_groups * ATTN_GROUP, qi, single_body, 0)
    update([qi], True)
    lam = _lam_full(lq1[...], lk1[...], lq2[...], lk2[...], lam_init)
    d = 2 * D_HEAD
    ot = a1[:d, :] / a1[d:d + 1, :] - lam * (a2[:d, :] / a2[d:d + 1, :])
    on = ot * lax.rsqrt(jnp.mean(ot * ot, axis=0, keepdims=True) + SUBLN_EPS)
    o_ref[0] = (on.T * sg_ref[...] * (1.0 - lam_init)).astype(o_ref.dtype)


def _attn_prompt(q, k, vt, lams, subln_g, lam_init, *, tq):
    nb, t, _ = q.shape
    tk = vt.shape[3]
    assert tk == tq
    qspec = pl.BlockSpec((1, tq, LANES), lambda b, h, i: (b, i, h))
    kspec = pl.BlockSpec((1, t, LANES), lambda b, h, i: (b, 0, h))
    vspec = pl.BlockSpec((1, t // tk, LANES, tk), lambda b, h, i: (b, 0, h, 0))
    row = lambda n: pl.BlockSpec((1, n), lambda b, h, i: (0, 0))
    acc = 2 * D_HEAD + ONES_ROWS
    return pl.pallas_call(
        functools.partial(_attn_prompt_kernel, tk=tk, lam_init=lam_init),
        grid=(nb, H_C, t // tq),
        in_specs=[qspec, kspec, vspec, row(D_HEAD), row(D_HEAD), row(D_HEAD), row(D_HEAD), row(LANES)],
        out_specs=qspec,
        out_shape=jax.ShapeDtypeStruct(q.shape, BF16),
        scratch_shapes=[pltpu.VMEM((1, tq), F32), pltpu.VMEM((acc, tq), F32),
                        pltpu.VMEM((1, tq), F32), pltpu.VMEM((acc, tq), F32)],
        compiler_params=_cparams("parallel", "parallel", "arbitrary"),
        name="attn_prompt",
    )(q, k, vt, *lams, subln_g)


def _attn_sample_kernel(pt_ref, q_ref, kn_ref, vn_ref, *refs, npg, lam_init):
    k_refs = refs[:npg]
    v_refs = refs[npg:2 * npg]
    lq1, lk1, lq2, lk2, sg_ref, o_ref, m_ref, l_ref, acc_ref = refs[2 * npg:]
    g = pl.program_id(1)
    nrow, ncol = 2 * H_C, PAGE_SIZE * H_C
    q = q_ref[0]
    lo = lax.broadcasted_iota(jnp.int32, q.shape, 1) < D_HEAD
    qz = jnp.concatenate([jnp.where(lo, q, 0.0), jnp.where(lo, 0.0, q)], axis=0).astype(BF16)
    valid = ((lax.broadcasted_iota(jnp.int32, (nrow, ncol), 1) & (H_C - 1))
             == (lax.broadcasted_iota(jnp.int32, (nrow, ncol), 0) & (H_C - 1)))

    @pl.when(g == 0)
    def _():
        as_mxu = lambda ref: jnp.concatenate([ref[0], ref[0]], axis=0).astype(BF16).astype(F32)
        m_ref[...] = jnp.sum(qz.astype(F32) * as_mxu(kn_ref), axis=1, keepdims=True)
        l_ref[...] = jnp.ones_like(l_ref)
        acc_ref[...] = as_mxu(vn_ref)

    sts = []
    for r in range(npg):
        kmat = k_refs[r][...].reshape(ncol, LANES).astype(BF16)
        st = lax.dot_general(qz, kmat, (((1,), (1,)), ((), ())), preferred_element_type=F32)
        sts.append(jnp.where(valid, st, -jnp.inf))
    m_old = m_ref[...]
    m_new = m_old
    for st in sts:
        m_new = jnp.maximum(m_new, jnp.max(st, axis=1, keepdims=True))
    alpha = jnp.exp2(m_old - m_new)
    l = alpha * l_ref[...]
    acc = alpha * acc_ref[...]
    for st, v_ref in zip(sts, v_refs):
        p = jnp.exp2(st - m_new)
        l = l + jnp.sum(p, axis=1, keepdims=True)
        acc = acc + jnp.dot(p.astype(BF16), v_ref[...].reshape(ncol, LANES).astype(BF16),
                            preferred_element_type=F32)
    m_ref[...] = m_new
    l_ref[...] = l
    acc_ref[...] = acc

    @pl.when(g == pl.num_programs(1) - 1)
    def _():
        o = acc_ref[...] / l_ref[...]
        lam = _lam_full(lq1[...], lk1[...], lq2[...], lk2[...], lam_init)
        o_ref[0] = (_rms(o[:H_C] - lam * o[H_C:], sg_ref[...], SUBLN_EPS) * (1.0 - lam_init)).astype(o_ref.dtype)


def _attn_sample(q, k_new, v_new, cache_k, cache_v, page_table, layer, lams, subln_g, lam_init, *, npg):
    nb = q.shape[0]
    n_pages = page_table.shape[1]
    hspec = pl.BlockSpec((1, H_C, LANES), lambda b, g, pt: (b, 0, 0))
    row = lambda n: pl.BlockSpec((1, n), lambda b, g, pt: (0, 0))

    def page_spec(r):
        return pl.BlockSpec((None, None, PAGE_SIZE, H_C, LANES),
                            lambda b, g, pt: (layer, pt[b, g * npg + r], 0, 0, 0))

    grid_spec = pltpu.PrefetchScalarGridSpec(
        num_scalar_prefetch=1,
        grid=(nb, n_pages // npg),
        in_specs=[hspec, hspec, hspec] + [page_spec(r) for r in range(npg)] * 2
                 + [row(D_HEAD), row(D_HEAD), row(D_HEAD), row(D_HEAD), row(LANES)],
        out_specs=hspec,
        scratch_shapes=[pltpu.VMEM((2 * H_C, 1), F32), pltpu.VMEM((2 * H_C, 1), F32),
                        pltpu.VMEM((2 * H_C, LANES), F32)],
    )
    return pl.pallas_call(
        functools.partial(_attn_sample_kernel, npg=npg, lam_init=lam_init),
        grid_spec=grid_spec,
        out_shape=jax.ShapeDtypeStruct((nb, H_C, LANES), BF16),
        compiler_params=_cparams("parallel", "arbitrary"),
        name="attn_sample",
    )(page_table, q, k_new, v_new, *([cache_k] * npg), *([cache_v] * npg), *lams, subln_g)


def _merge_kernel(x_ref, oa_ref, ob_ref, oc_ref, cg_ref, wa_ref, wb_ref, wc_ref, wo_ref, xo_ref):
    cg = cg_ref[...]
    dot = lambda a, w: jnp.dot(a, w, preferred_element_type=F32)
    m = (jax.nn.sigmoid(cg[:, 0:D_MODEL]) * dot(oa_ref[...], wa_ref[...])
         + jax.nn.sigmoid(cg[:, D_MODEL:2 * D_MODEL]) * dot(ob_ref[...], wb_ref[...])
         + jax.nn.sigmoid(cg[:, 2 * D_MODEL:3 * D_MODEL]) * dot(oc_ref[...], wc_ref[...]))
    xo_ref[...] = x_ref[...] + dot(m.astype(BF16), wo_ref[...])


def _merge(x, oa, ob, oc, cg, wa, wb, wc, wo, *, tm):
    m = x.shape[0]
    rows = lambda n: pl.BlockSpec((tm, n), lambda i: (i, 0))
    full = lambda a: pl.BlockSpec(a.shape, lambda i: (0, 0))
    return pl.pallas_call(
        _merge_kernel,
        grid=(m // tm,),
        in_specs=[rows(D_MODEL), rows(D_A), rows(D_B), rows(D_C), rows(3 * D_MODEL),
                  full(wa), full(wb), full(wc), full(wo)],
        out_specs=rows(D_MODEL),
        out_shape=jax.ShapeDtypeStruct((m, D_MODEL), F32),
        compiler_params=_cparams("parallel"),
        name="merge",
    )(x, oa, ob, oc, cg, wa, wb, wc, wo)


def _ffn_kernel(x_ref, g_ref, wu_ref, wd_ref, gf_ref, o_ref, h_ref, acc_ref, *, final):
    j = pl.program_id(1)

    @pl.when(j == 0)
    def _():
        h_ref[...] = _rms(x_ref[...], g_ref[...], EPS).astype(BF16)
        acc_ref[...] = jnp.zeros_like(acc_ref)

    up = jnp.dot(h_ref[...], wu_ref[...], preferred_element_type=F32)
    act = jnp.square(jnp.maximum(up, 0.0)).astype(BF16)
    acc_ref[...] += jnp.dot(act, wd_ref[...], preferred_element_type=F32)

    @pl.when(j == pl.num_programs(1) - 1)
    def _():
        xn = x_ref[...] + acc_ref[...]
        o_ref[...] = _rms(xn, gf_ref[...], EPS) if final else xn


def _ffn(x, g, wu, wd, gf, *, final, tm, tf):
    m = x.shape[0]
    row = pl.BlockSpec((1, D_MODEL), lambda i, j: (0, 0))
    return pl.pallas_call(
        functools.partial(_ffn_kernel, final=final),
        grid=(m // tm, D_FF // tf),
        in_specs=[pl.BlockSpec((tm, D_MODEL), lambda i, j: (i, 0)), row,
                  pl.BlockSpec((D_MODEL, tf), lambda i, j: (0, j)),
                  pl.BlockSpec((tf, D_MODEL), lambda i, j: (j, 0)), row],
        out_specs=pl.BlockSpec((tm, D_MODEL), lambda i, j: (i, 0)),
        out_shape=jax.ShapeDtypeStruct((m, D_MODEL), F32),
        scratch_shapes=[pltpu.VMEM((tm, D_MODEL), BF16), pltpu.VMEM((tm, D_MODEL), F32)],
        compiler_params=_cparams("parallel", "arbitrary"),
        name="ffn",
    )(x, g, wu, wd, gf)


def _pad_b_cols(a):
    o = 3 * D_B
    z = lambda n: jnp.zeros(a.shape[:-1] + (n,), a.dtype)
    return jnp.concatenate([a[..., :o + LORA_W], z(LW_PAD - LORA_W),
                            a[..., o + LORA_W:o + LORA_W + LORA_A], z(LA_PAD - LORA_A),
                            a[..., o + LORA_W + LORA_A:], z(LG_PAD - LORA_G)], axis=-1)


def _unpad_b_cols(a):
    o = 3 * D_B
    return jnp.concatenate([a[..., :o + LORA_W], a[..., o + LW_PAD:o + LW_PAD + LORA_A],
                            a[..., o + LW_PAD + LA_PAD:o + LW_PAD + LA_PAD + LORA_G]], axis=-1)


def _pad_rows(a, n):
    return jnp.concatenate([a, jnp.zeros((n - a.shape[0],) + a.shape[1:], a.dtype)], axis=0)


def _ones_blockdiag(n, seg):
    i = np.arange(n) // seg
    return jnp.asarray((i[:, None] == i[None, :]).astype(np.float32), dtype=BF16)


def _layer_weights(l, p):
    w_in = p["w_in"][l]
    o_b = N_A_COLS
    o_q = o_b + N_B_COLS
    w = dict(
        norm_mix=p["norm_mix"][l].reshape(1, D_MODEL),
        w_a=w_in[:, :o_b].astype(BF16),
        w_b=_pad_b_cols(w_in[:, o_b:o_q]).astype(BF16),
        w_q=w_in[:, o_q:o_q + D_C].astype(BF16),
        w_k=w_in[:, o_q + D_C:o_q + 2 * D_C].astype(BF16),
        w_v=w_in[:, o_q + 2 * D_C:o_q + 3 * D_C].astype(BF16),
        w_g=w_in[:, o_q + 3 * D_C:].astype(BF16),
        lng=p["sgu_ln_g"][l].reshape(1, D_A), lnb=p["sgu_ln_b"][l].reshape(1, D_A),
        sgu_w=p["sgu_w"][l],
        sgu_bias_td=jnp.repeat(p["sgu_b"][l].T, D_A // G_A, axis=1),
        sgu_w0=jnp.repeat(p["sgu_w"][l][:, 0, 0], D_A // G_A).reshape(1, D_A),
        sgu_b0=jnp.repeat(p["sgu_b"][l][:, 0], D_A // G_A).reshape(1, D_A),
        rwkv=[_pad_b_cols(p["shift_mu"][l]).reshape(1, N_B_PAD),
              p["w0"][l].reshape(1, D_B), _pad_rows(p["w2"][l], LW_PAD).astype(BF16),
              p["a0"][l].reshape(1, D_B), _pad_rows(p["a2"][l], LA_PAD).astype(BF16),
              _pad_rows(p["g2"][l], LG_PAD).astype(BF16),
              p["k_k"][l].reshape(1, D_B), p["k_a"][l].reshape(1, D_B), p["r_k"][l].reshape(1, D_B),
              _ones_blockdiag(D_B, N_B)],
        lnx_g=p["lnx_g"][l].reshape(1, D_B), lnx_b=p["lnx_b"][l].reshape(1, D_B),
        lams=[p[n][l].reshape(1, D_HEAD) for n in ("lam_q1", "lam_k1", "lam_q2", "lam_k2")],
        subln_g=p["subln_g"][l].reshape(1, 2 * D_HEAD),
        w_br_a=p["w_br_a"][l].astype(BF16), w_br_b=p["w_br_b"][l].astype(BF16),
        w_br_c=p["w_br_c"][l].astype(BF16), w_out=p["w_out"][l].astype(BF16),
        norm_ffn=p["norm_ffn"][l].reshape(1, D_MODEL),
        w_up=p["w_up"][l].astype(BF16), w_down=p["w_down"][l].astype(BF16),
        norm_final=p["norm_final"].reshape(1, D_MODEL),
    )
    return w


def _project_all(x, w, rope_tabs, tm, t_seq=None, kv_stack=None):
    pj = functools.partial(_proj, x, w["norm_mix"], tm=tm, tn=D_MODEL)
    qscale = math.log2(math.e) / math.sqrt(D_HEAD)
    (ca,) = pj(w["w_a"], out_dtypes=(F32,), name="proj_a")
    (cb,) = pj(w["w_b"], out_dtypes=(F32,), name="proj_b")
    (cg,) = pj(w["w_g"], out_dtypes=(F32,), name="proj_g")
    if t_seq is None:
        (q,) = pj(w["w_q"], out_dtypes=(F32,), rope_tabs=rope_tabs, scale=qscale, name="proj_q")
        (kf,) = pj(w["w_k"], out_dtypes=(F32,), rope_tabs=rope_tabs, name="proj_k")
        (vf,) = pj(w["w_v"], out_dtypes=(F32,), name="proj_v")
        return ca, cb, cg, q, kf, vf
    (q,) = pj(w["w_q"], out_dtypes=(BF16,), rope_tabs=rope_tabs, scale=qscale, name="proj_q")
    depth, layer, k_buf, v_buf = kv_stack
    kf, kb = pj(w["w_k"], out_dtypes=(F32, BF16), rope_tabs=rope_tabs, stack=(depth, layer, k_buf), name="proj_k")
    vf, vt = _proj(x, w["norm_mix"], w["w_v"], tm=ATTN_BLOCK, tn=D_MODEL, out_dtypes=(F32, BF16), t_seq=t_seq,
                   stack=(depth, layer, v_buf), name="proj_v")
    return ca, cb, cg, q, kf, vf, kb, vt


def _prompt_layer(x, l, w, rope_tabs, nb, t, final, kv_stack):
    lam_init = 0.8 - 0.6 * math.exp(-0.3 * l)
    ca, cb, cg, qb, kf, vf, kb, vt = _project_all(x, w, rope_tabs, 1024, t_seq=t, kv_stack=kv_stack)
    o_a = _sgu_prompt(ca, w["lng"], w["lnb"], w["sgu_w"], w["sgu_bias_td"], tm=512)
    prev0 = jnp.zeros((nb, 1, N_B_PAD), F32)
    *rw, last = _rwkv_pre(cb.reshape(nb, t, N_B_PAD), prev0, w["rwkv"], seq=True, tm=256)
    r, dec, k2, v, na, bb, g, rkv = rw
    y, s_fin = _rwkv_scan_seq(r, dec, k2, v, na, bb, tc=64)
    flat = lambda a: a.reshape(nb * t, D_B)
    o_b = _rwkv_post(flat(y), flat(rkv), flat(g), w["lnx_g"], w["lnx_b"], w["rwkv"][-1], tm=1024)
    b3 = lambda a: a.reshape(nb, t, D_C)
    o_c = _attn_prompt(b3(qb), b3(kb), vt, w["lams"], w["subln_g"], lam_init, tq=ATTN_BLOCK)
    x = _merge(x, o_a, o_b, o_c.reshape(nb * t, D_C), cg, w["w_br_a"], w["w_br_b"], w["w_br_c"], w["w_out"], tm=256)
    x = _ffn(x, w["norm_ffn"], w["w_up"], w["w_down"], w["norm_final"], final=final, tm=1024, tf=1024)
    return x, kf, vf, s_fin, _unpad_b_cols(last[:, 0, :])


def _sample_layer(x, l, w, rope_tabs, cache_k, cache_v, page_table, prev_shift, s0, final):
    nb = x.shape[0]
    lam_init = 0.8 - 0.6 * math.exp(-0.3 * l)
    ca, cb, cg, qb, kf, vf = _project_all(x, w, rope_tabs, nb)
    o_a, va = _sgu_sample(ca, w["lng"], w["lnb"], w["sgu_w0"], w["sgu_b0"])
    r, dec, k2, v, na, bb, g, rkv = _rwkv_pre(cb, _pad_b_cols(prev_shift), w["rwkv"], seq=False, tm=nb)
    y, s_new = _rwkv_scan_rows(r, dec, k2, v, na, bb, s0, tb=32)
    o_b = _rwkv_post(y, rkv, g, w["lnx_g"], w["lnx_b"], w["rwkv"][-1], tm=nb)
    h3 = lambda a: a.reshape(nb, H_C, LANES)
    o_c = _attn_sample(h3(qb), h3(kf), h3(vf), cache_k, cache_v, page_table, l,
                       w["lams"], w["subln_g"], lam_init, npg=8)
    x = _merge(x, o_a, o_b, o_c.reshape(nb, D_C), cg, w["w_br_a"], w["w_br_b"], w["w_br_c"], w["w_out"], tm=nb)
    x = _ffn(x, w["norm_ffn"], w["w_up"], w["w_down"], w["norm_final"], final=final, tm=nb, tf=1024)
    return x, kf, vf, s_new, _unpad_b_cols(cb), va


def kernel(x_prompt, x_sample, cache_k, cache_v, state_rwkv, state_shift, page_table, norm_mix, w_in, sgu_ln_g, sgu_ln_b, sgu_w, sgu_b, shift_mu, w0, w2, a0, a2, g2, k_k, k_a, r_k, lnx_g, lnx_b, lam_q1, lam_k1, lam_q2, lam_k2, subln_g, w_br_a, w_br_b, w_br_c, w_out, norm_ffn, w_up, w_down, norm_final):
    p = dict(norm_mix=norm_mix, w_in=w_in, sgu_ln_g=sgu_ln_g, sgu_ln_b=sgu_ln_b, sgu_w=sgu_w, sgu_b=sgu_b,
             shift_mu=shift_mu, w0=w0, w2=w2, a0=a0, a2=a2, g2=g2, k_k=k_k, k_a=k_a, r_k=r_k, lnx_g=lnx_g,
             lnx_b=lnx_b, lam_q1=lam_q1, lam_k1=lam_k1, lam_q2=lam_q2, lam_k2=lam_k2, subln_g=subln_g,
             w_br_a=w_br_a, w_br_b=w_br_b, w_br_c=w_br_c, w_out=w_out, norm_ffn=norm_ffn, w_up=w_up,
             w_down=w_down, norm_final=norm_final)
    depth = w_in.shape[0]
    bp, tp, _ = x_prompt.shape
    bs, ts, _ = x_sample.shape
    tabs_p = _rope_tables(jnp.arange(tp))
    tabs_s = _rope_tables(jnp.full((bs * ts,), PAST_LEN, jnp.int32))
    xp = x_prompt.reshape(bp * tp, D_MODEL)
    xs = x_sample.reshape(bs * ts, D_MODEL)
    outs_p, outs_s = [], []
    kp = vp = None
    for l in range(depth):
        w = _layer_weights(l, p)
        final = l == depth - 1
        xp, kp, vp, sp, shp = _prompt_layer(xp, l, w, tabs_p, bp, tp, final, (depth, l, kp, vp))
        xs, ks, vs, ss, shs, va = _sample_layer(xs, l, w, tabs_s, cache_k, cache_v, page_table,
                                                state_shift[l], state_rwkv[l], final)
        outs_p.append((sp, shp))
        outs_s.append((ks.reshape(bs, ts, H_C, 2 * D_HEAD), vs.reshape(bs, ts, H_C, 2 * D_HEAD), ss, shs,
                       va.reshape(bs, ts, D_A)))
    stack = lambda outs, i: jnp.stack([o[i] for o in outs])
    return (xp.reshape(bp, tp, D_MODEL), xs.reshape(bs, ts, D_MODEL),
            kp.reshape(depth, bp, tp, H_C, 2 * D_HEAD), vp.reshape(depth, bp, tp, H_C, 2 * D_HEAD),
            stack(outs_p, 0), stack(outs_p, 1),
            stack(outs_s, 0), stack(outs_s, 1), stack(outs_s, 2), stack(outs_s, 3), stack(outs_s, 4))
```

```python
import functools
import math

import numpy as np
import jax
import jax.numpy as jnp
from jax import lax
from jax.experimental import pallas as pl
from jax.experimental.pallas import tpu as pltpu

F32 = jnp.float32
BF16 = jnp.bfloat16

D_MODEL = 1024
PAST_LEN = 2048
PAGE_SIZE = 128
CHUNK = 128
D_A = 512
G_A = 8
H_B = 8
N_B = 64
D_B = H_B * N_B
LORA_W = 64
LORA_A = 64
LORA_G = 160
GN_EPS_B = 64e-5
H_C = 8
D_HEAD = 64
D_C = H_C * 2 * D_HEAD
ROPE_DIM = D_HEAD // 4
ROPE_THETA = 500000.0
SUBLN_EPS = 1e-5
D_FF = 4 * D_MODEL
EPS = 1e-6
N_A_COLS = 2 * D_A
N_B_COLS = 3 * D_B + LORA_W + LORA_A + LORA_G
LANES = 128
SUBLANES = 8
LW_PAD = 128
LA_PAD = 128
LG_PAD = 256
N_B_PAD = 3 * D_B + LW_PAD + LA_PAD + LG_PAD
VMEM_LIMIT = 52 * 1024 * 1024
ATTN_BLOCK = 512
ATTN_GROUP = 4
ATTN_STRIP = 256


def _cparams(*sem):
    return pltpu.CompilerParams(dimension_semantics=sem, vmem_limit_bytes=VMEM_LIMIT)


def _rms(x, g, eps):
    return x * lax.rsqrt(jnp.mean(x * x, axis=-1, keepdims=True) + eps) * g


def _proj_kernel(*refs, rope, scale, n_out, aliased, gate):
    x_ref, g_ref, w_ref = refs[:3]
    pos = 3
    if rope:
        c_ref, s1_ref, s2_ref = refs[3:6]
        pos = 6
    pos += aliased
    out_refs = refs[pos:pos + n_out]
    h_ref = refs[pos + n_out]

    @pl.when(pl.program_id(1) == 0)
    def _():
        h_ref[...] = _rms(x_ref[...], g_ref[...], EPS).astype(BF16)

    acc = jnp.dot(h_ref[...], w_ref[...], preferred_element_type=F32)
    if rope:
        c, s1, s2 = c_ref[...], s1_ref[...], s2_ref[...]
        parts = []
        for hh in range(acc.shape[1] // LANES):
            a = acc[:, hh * LANES:(hh + 1) * LANES]
            parts.append(a * c + pltpu.roll(a, ROPE_DIM // 2, 1) * s1
                         + pltpu.roll(a, LANES - ROPE_DIM // 2, 1) * s2)
        acc = jnp.concatenate(parts, axis=1)
    if scale != 1.0:
        acc = acc * scale
    if gate:
        acc = jax.nn.sigmoid(acc)
    for o_ref in out_refs:
        if len(o_ref.shape) == 4:
            o_ref[0, 0] = acc.T.astype(o_ref.dtype)
        else:
            o_ref[...] = acc.astype(o_ref.dtype)


def _proj(x, g, w, *, tm, tn, out_dtypes, rope_tabs=None, scale=1.0, t_seq=None, stack=None, gate=False,
          name="proj"):
    m, d = x.shape
    n = w.shape[1]
    in_specs = [pl.BlockSpec((tm, d), lambda i, j: (i, 0)),
                pl.BlockSpec((1, d), lambda i, j: (0, 0)),
                pl.BlockSpec((d, tn), lambda i, j: (0, j))]
    args = [x, g, w]
    if rope_tabs is not None:
        nt = rope_tabs[0].shape[0] // tm
        in_specs += [pl.BlockSpec((tm, LANES), lambda i, j: (i % nt, 0))] * 3
        args += list(rope_tabs)
    out_specs = [pl.BlockSpec((tm, tn), lambda i, j: (i, j)) for _ in out_dtypes]
    out_shape = [jax.ShapeDtypeStruct((m, n), dt) for dt in out_dtypes]
    if t_seq is not None:
        nts = t_seq // tm
        out_specs[-1] = pl.BlockSpec((1, 1, tn, tm), lambda i, j: (i // nts, i % nts, j, 0))
        out_shape[-1] = jax.ShapeDtypeStruct((m // t_seq, nts, n, tm), out_dtypes[-1])
    aliases = {}
    if stack is not None:
        depth, layer, prev = stack
        off = layer * (m // tm)
        out_specs[0] = pl.BlockSpec((tm, tn), lambda i, j: (i + off, j))
        out_shape[0] = jax.ShapeDtypeStruct((depth * m, n), out_dtypes[0])
        if prev is not None:
            aliases = {len(args): 0}
            in_specs.append(pl.BlockSpec(memory_space=pl.ANY))
            args.append(prev)
    outs = pl.pallas_call(
        functools.partial(_proj_kernel, rope=rope_tabs is not None, scale=scale, n_out=len(out_dtypes),
                          aliased=len(aliases), gate=gate),
        grid=(m // tm, n // tn),
        in_specs=in_specs,
        out_specs=out_specs,
        out_shape=out_shape,
        input_output_aliases=aliases,
        scratch_shapes=[pltpu.VMEM((tm, d), BF16)],
        compiler_params=_cparams("parallel", "arbitrary"),
        name=name,
    )(*args)
    return outs


def _rope_kernel(pos_ref, invf_ref, c_ref, s1_ref, s2_ref):
    ang = pos_ref[...] * invf_ref[...]
    lane = lax.broadcasted_iota(jnp.int32, ang.shape, 1) % D_HEAD
    first = lane < ROPE_DIM // 2
    second = (lane >= ROPE_DIM // 2) & (lane < ROPE_DIM)
    cos, sin = jnp.cos(ang), jnp.sin(ang)
    c_ref[...] = jnp.where(first | second, cos, 1.0)
    s1_ref[...] = jnp.where(second, sin, 0.0)
    s2_ref[...] = jnp.where(first, -sin, 0.0)


def _rope_tables(pos):
    t = pos.shape[0]
    half = ROPE_DIM // 2
    inv_freq = ROPE_THETA ** (-jnp.arange(half, dtype=F32) / half)
    blk = jnp.concatenate([inv_freq, inv_freq, jnp.zeros((D_HEAD - ROPE_DIM,), F32)])
    invf = jnp.concatenate([blk, blk]).reshape(1, LANES)
    tm = min(t, 512)
    return pl.pallas_call(
        _rope_kernel,
        grid=(t // tm,),
        in_specs=[pl.BlockSpec((tm, 1), lambda i: (i, 0)), pl.BlockSpec((1, LANES), lambda i: (0, 0))],
        out_specs=[pl.BlockSpec((tm, LANES), lambda i: (i, 0))] * 3,
        out_shape=[jax.ShapeDtypeStruct((t, LANES), F32)] * 3,
        compiler_params=_cparams("parallel"),
        name="rope_tables",
    )(pos.astype(F32).reshape(t, 1), invf)


def _gelu_ln(ca, lng, lnb):
    gx = 0.5 * ca * (1.0 + lax.erf(ca * math.sqrt(0.5)))
    u, v = gx[:, :D_A], gx[:, D_A:]
    d = v - jnp.mean(v, axis=-1, keepdims=True)
    va = d * lax.rsqrt(jnp.mean(d * d, axis=-1, keepdims=True) + EPS) * lng + lnb
    return u, va


def _sgu_prompt_kernel(ca_ref, lng_ref, lnb_ref, w_ref, b_ref, oa_ref, *, nchunk):
    u, va = _gelu_ln(ca_ref[...], lng_ref[...], lnb_ref[...])
    vab = va.astype(BF16)
    row = lax.broadcasted_iota(jnp.int32, (CHUNK, CHUNK), 0)
    col = lax.broadcasted_iota(jnp.int32, (CHUNK, CHUNK), 1)
    wcat = jnp.concatenate([jnp.where(col <= row, w_ref[g], 0.0).astype(BF16) for g in range(G_A)], axis=1)
    grp = lax.broadcasted_iota(jnp.int32, (CHUNK, D_A), 1) // (D_A // G_A)
    for ci in range(nchunk):
        vc = vab[ci * CHUNK:(ci + 1) * CHUNK]
        vbig = jnp.concatenate([jnp.where(grp == g, vc, jnp.zeros_like(vc)) for g in range(G_A)], axis=0)
        s = jnp.dot(wcat, vbig, preferred_element_type=F32) + b_ref[...]
        oa_ref[ci * CHUNK:(ci + 1) * CHUNK, :] = (u[ci * CHUNK:(ci + 1) * CHUNK] * s).astype(oa_ref.dtype)


def _sgu_prompt(ca, lng, lnb, w, bias_td, *, tm):
    m = ca.shape[0]
    return pl.pallas_call(
        functools.partial(_sgu_prompt_kernel, nchunk=tm // CHUNK),
        grid=(m // tm,),
        in_specs=[pl.BlockSpec((tm, N_A_COLS), lambda i: (i, 0)),
                  pl.BlockSpec((1, D_A), lambda i: (0, 0)),
                  pl.BlockSpec((1, D_A), lambda i: (0, 0)),
                  pl.BlockSpec((G_A, CHUNK, CHUNK), lambda i: (0, 0, 0)),
                  pl.BlockSpec((CHUNK, D_A), lambda i: (0, 0))],
        out_specs=pl.BlockSpec((tm, D_A), lambda i: (i, 0)),
        out_shape=jax.ShapeDtypeStruct((m, D_A), BF16),
        compiler_params=_cparams("parallel"),
        name="sgu_prompt",
    )(ca, lng, lnb, w, bias_td)


def _sgu_sample_kernel(ca_ref, lng_ref, lnb_ref, w0_ref, b0_ref, oa_ref, va_ref):
    u, va = _gelu_ln(ca_ref[...], lng_ref[...], lnb_ref[...])
    va_ref[...] = va
    oa_ref[...] = (u * (va * w0_ref[...] + b0_ref[...])).astype(oa_ref.dtype)


def _sgu_sample(ca, lng, lnb, w0, b0):
    m = ca.shape[0]
    row = lambda n: pl.BlockSpec((1, n), lambda i: (0, 0))
    return pl.pallas_call(
        _sgu_sample_kernel,
        grid=(1,),
        in_specs=[pl.BlockSpec((m, N_A_COLS), lambda i: (0, 0)), row(D_A), row(D_A), row(D_A), row(D_A)],
        out_specs=[pl.BlockSpec((m, D_A), lambda i: (0, 0))] * 2,
        out_shape=[jax.ShapeDtypeStruct((m, D_A), BF16), jax.ShapeDtypeStruct((m, D_A), F32)],
        compiler_params=_cparams("arbitrary"),
        name="sgu_sample",
    )(ca, lng, lnb, w0, b0)


def _seg_sum(x, ones_bd):
    hi = x.astype(BF16)
    lo = (x - hi.astype(F32)).astype(BF16)
    return (jnp.dot(hi, ones_bd, preferred_element_type=F32)
            + jnp.dot(lo, ones_bd, preferred_element_type=F32))


def _rwkv_pre_kernel(*refs, seq):
    if seq:
        c_ref, prev_ref = refs[:2]
    else:
        c_ref, sh_ref = refs[:2]
    (mu_ref, w0_ref, w2_ref, a0_ref, a2_ref, g2_ref, kk_ref, ka_ref, rk_ref, bd_ref) = refs[2:12]
    (r_o, w_o, k_o, v_o, a_o, b_o, g_o, rkv_o) = refs[12:20]
    if seq:
        last_o, carry_ref = refs[20:22]
        cols = c_ref[0]
        tm = cols.shape[0]

        @pl.when(pl.program_id(1) == 0)
        def _():
            carry_ref[...] = prev_ref[0]

        rolled = pltpu.roll(cols, 1, 0)
        first = lax.broadcasted_iota(jnp.int32, cols.shape, 0) == 0
        shifted = jnp.where(first, carry_ref[...], rolled)
        carry_ref[...] = cols[tm - 1:tm, :]
        last_o[0] = cols[tm - 1:tm, :]
    else:
        cols = c_ref[...]
        shifted = sh_ref[...]
    xs = cols + (shifted - cols) * mu_ref[...]
    r = xs[:, 0:D_B]
    k = xs[:, D_B:2 * D_B]
    v = xs[:, 2 * D_B:3 * D_B]
    o = 3 * D_B
    wl = xs[:, o:o + LW_PAD]
    al = xs[:, o + LW_PAD:o + LW_PAD + LA_PAD]
    gl = xs[:, o + LW_PAD + LA_PAD:o + LW_PAD + LA_PAD + LG_PAD]
    z = -(w0_ref[...] + jnp.dot(jnp.tanh(wl).astype(BF16), w2_ref[...], preferred_element_type=F32))
    softplus = jnp.maximum(z, 0.0) + jnp.log1p(jnp.exp(-jnp.abs(z)))
    log_decay = -jnp.exp(-softplus - 0.5)
    decay = log_decay if seq else jnp.exp(log_decay)
    a = jax.nn.sigmoid(a0_ref[...] + jnp.dot(al.astype(BF16), a2_ref[...], preferred_element_type=F32))
    g = jnp.dot(jax.nn.sigmoid(gl).astype(BF16), g2_ref[...], preferred_element_type=F32)
    bd = bd_ref[...]
    kk = k * kk_ref[...]
    kk = kk / jnp.maximum(jnp.sqrt(_seg_sum(kk * kk, bd)), 1e-12)
    k2 = k * (1.0 + (a - 1.0) * ka_ref[...])
    if seq:
        r_o[0], w_o[0], k_o[0], v_o[0], a_o[0], b_o[0], g_o[0] = r, decay, k2, v, -kk, kk * a, g
        rkv_o[0] = _seg_sum(r * k2 * rk_ref[...], bd) * v
    else:
        r_o[...], w_o[...], k_o[...], v_o[...], a_o[...], b_o[...], g_o[...] = r, decay, k2, v, -kk, kk * a, g
        rkv_o[...] = _seg_sum(r * k2 * rk_ref[...], bd) * v


def _rwkv_pre(cols, shift_src, params, *, seq, tm):
    prm_specs = []
    for p in params:
        prm_specs.append(pl.BlockSpec(p.shape, (lambda b, i: (0, 0)) if seq else (lambda i: (0, 0))))
    if seq:
        nb, t, _ = cols.shape
        grid = (nb, t // tm)
        in_specs = [pl.BlockSpec((1, tm, N_B_PAD), lambda b, i: (b, i, 0)),
                    pl.BlockSpec((1, 1, N_B_PAD), lambda b, i: (b, 0, 0))] + prm_specs
        ospec = pl.BlockSpec((1, tm, D_B), lambda b, i: (b, i, 0))
        oshape = jax.ShapeDtypeStruct((nb, t, D_B), F32)
        out_specs = [ospec] * 8 + [pl.BlockSpec((1, 1, N_B_PAD), lambda b, i: (b, 0, 0))]
        out_shape = [oshape] * 8 + [jax.ShapeDtypeStruct((nb, 1, N_B_PAD), F32)]
        scratch = [pltpu.VMEM((1, N_B_PAD), F32)]
        sem = ("parallel", "arbitrary")
    else:
        m = cols.shape[0]
        grid = (m // tm,)
        in_specs = [pl.BlockSpec((tm, N_B_PAD), lambda i: (i, 0))] * 2 + prm_specs
        out_specs = [pl.BlockSpec((tm, D_B), lambda i: (i, 0))] * 8
        out_shape = [jax.ShapeDtypeStruct((m, D_B), F32)] * 8
        scratch = []
        sem = ("parallel",)
    return pl.pallas_call(
        functools.partial(_rwkv_pre_kernel, seq=seq),
        grid=grid, in_specs=in_specs, out_specs=out_specs, out_shape=out_shape,
        scratch_shapes=scratch, compiler_params=_cparams(*sem),
        name="rwkv_pre_seq" if seq else "rwkv_pre_rows",
    )(cols, shift_src, *params)


def _scan_pair_step(S, a, w, b, k, v, r):
    lane = lax.broadcasted_iota(jnp.int32, (N_B, LANES), 1)
    row = lax.broadcasted_iota(jnp.int32, (N_B, LANES), 0)
    lo = lane < N_B
    e1 = lane == row
    e2 = lane == row + N_B

    def half_sums(p):
        t1 = jnp.sum(jnp.where(lo, p, 0.0), axis=1, keepdims=True)
        t2 = jnp.sum(jnp.where(lo, 0.0, p), axis=1, keepdims=True)
        return t1, t2

    t1, t2 = half_sums(S * a)
    sa = jnp.where(lo, t1, t2)
    vb = jnp.where(lo, jnp.sum(jnp.where(e1, v, 0.0), axis=1, keepdims=True),
                   jnp.sum(jnp.where(e2, v, 0.0), axis=1, keepdims=True))
    S = S * w + sa * b + vb * k
    y1, y2 = half_sums(S * r)
    y = jnp.sum(jnp.where(e1, y1, 0.0) + jnp.where(e2, y2, 0.0), axis=0, keepdims=True)
    return S, y


def _split(x):
    hi = x.astype(BF16)
    return hi, (x - hi.astype(F32)).astype(BF16)


def _mm3(xs, ys, nt=False):
    lhs = jnp.concatenate([xs[0], xs[0], xs[1]], axis=1)
    if nt:
        rhs = jnp.concatenate([ys[0], ys[1], ys[0]], axis=1)
        return lax.dot_general(lhs, rhs, (((1,), (1,)), ((), ())), preferred_element_type=F32)
    rhs = jnp.concatenate([ys[0], ys[1], ys[0]], axis=0)
    return jnp.dot(lhs, rhs, preferred_element_type=F32)


def _mm1(x, y, nt=False):
    if nt:
        return lax.dot_general(x, y, (((1,), (1,)), ((), ())), preferred_element_type=F32)
    return jnp.dot(x, y, preferred_element_type=F32)


def _block_diag(x):
    lo = lax.broadcasted_iota(jnp.int32, x.shape, 1) < N_B
    zero = jnp.zeros_like(x)
    return jnp.concatenate([jnp.where(lo, x, zero), jnp.where(lo, zero, x)], axis=0)


def _each(fn, *lists):
    return [fn(*args) for args in zip(*lists)]


def _rwkv_chunk_pairs(zs, lws, a_s, bs, ks, vs, rs):
    c = lws[0].shape[0]
    tri = (lax.broadcasted_iota(jnp.int32, (c, c), 0) >= lax.broadcasted_iota(jnp.int32, (c, c), 1)).astype(BF16)
    tri3 = jnp.concatenate([tri, tri, tri], axis=1)

    def cumsum(lw):
        l1 = lw.astype(BF16)
        r1 = lw - l1.astype(F32)
        l2 = r1.astype(BF16)
        l3 = (r1 - l2.astype(F32)).astype(BF16)
        return jnp.dot(tri3, jnp.concatenate([l1, l2, l3], axis=0), preferred_element_type=F32)

    lam = _each(cumsum, lws)
    lam_c = [x[c - 1:c, :] for x in lam]
    w_inv = [jnp.exp(-x) for x in lam]
    w_rem = _each(lambda lc, x: jnp.exp(lc - x), lam_c, lam)
    r_h = _each(lambda r, x: _block_diag(r * jnp.exp(x)), rs, lam)
    s_a = _each(lambda a, x, lw: _split(_block_diag(a * jnp.exp(x - lw))), a_s, lam, lws)
    h_r = [x.astype(BF16) for x in r_h]
    s_b = _each(lambda b, wi: _split(_block_diag(b * wi)), bs, w_inv)
    s_k = _each(lambda k, wi: _split(_block_diag(k * wi)), ks, w_inv)
    s_v = [_split(_block_diag(v)) for v in vs]
    n = 2 * c
    ri = lax.broadcasted_iota(jnp.int32, (n, n), 0)
    ci = lax.broadcasted_iota(jnp.int32, (n, n), 1)
    strict, incl, eye = ci < ri, ci <= ri, ci == ri
    gram = lambda x, y: jnp.where(strict, _mm3(x, y, nt=True), 0.0)
    lab = _each(gram, s_a, s_b)
    lak = _each(gram, s_a, s_k)
    h_lrb = _each(lambda x, y: jnp.where(incl, _mm1(x, y[0], nt=True), 0.0).astype(BF16), h_r, s_b)
    h_lrk = _each(lambda x, y: jnp.where(incl, _mm1(x, y[0], nt=True), 0.0).astype(BF16), h_r, s_k)
    t = [jnp.where(eye, 1.0, 0.0) + x for x in lab]
    s_p = _each(_split, lab)
    for _ in range(int(math.log2(c)) - 1):
        s_p = _each(lambda p: _split(_mm3(p, p)), s_p)
        t = _each(lambda tt, p: tt + _mm3(_split(tt), p), t, s_p)
    s_t = _each(_split, t)
    s_at = _each(lambda tt, a: _split(_mm3(tt, a)), s_t, s_a)
    x2 = _each(lambda l, v: _split(_mm3(_split(l), v)), lak, s_v)
    s_u0 = _each(lambda tt, x: _split(_mm3(tt, x)), s_t, x2)
    rt = _each(lambda rh, l, at: rh + _mm1(l, at[0]), r_h, h_lrb, s_at)
    y0 = _each(lambda l, u, lk, v: _mm1(l, u[0]) + _mm1(lk, v[0]), h_lrb, s_u0, h_lrk, s_v)
    s_bt = _each(lambda b, wr: _split(_block_diag(b * wr).T), bs, w_rem)
    s_kt = _each(lambda k, wr: _split(_block_diag(k * wr).T), ks, w_rem)
    m = _each(lambda lc, bt, at: jnp.where(eye, jnp.exp(lc), 0.0) + _mm3(bt, at), lam_c, s_bt, s_at)
    nn = _each(lambda bt, u, kt, v: _mm3(bt, u) + _mm3(kt, v), s_bt, s_u0, s_kt, s_v)
    s_z = _each(_split, zs)
    ybd = _each(lambda r, z, y: _mm1(r.astype(BF16), z[0]) + y, rt, s_z, y0)
    z_new = _each(lambda mm, z, x: _mm3(_split(mm), z) + x, m, s_z, nn)
    return z_new, [y[:c] + y[c:] for y in ybd]


def _rwkv_scan_seq_kernel(r_ref, w_ref, k_ref, v_ref, a_ref, b_ref, y_ref, sf_ref, z_ref, *, nb):
    npair = H_B // 2
    ci = pl.program_id(0)

    @pl.when(ci == 0)
    def _():
        z_ref[...] = jnp.zeros_like(z_ref)

    sls = [(bi, slice(None), slice(p * LANES, (p + 1) * LANES)) for bi in range(nb) for p in range(npair)]
    rows = lambda ref: [ref[sl] for sl in sls]
    zs, ys = _rwkv_chunk_pairs([z_ref[c] for c in range(len(sls))], rows(w_ref), rows(a_ref),
                               rows(b_ref), rows(k_ref), rows(v_ref), rows(r_ref))
    for c, sl in enumerate(sls):
        z_ref[c] = zs[c]
        y_ref[sl] = ys[c]

    @pl.when(ci == pl.num_programs(0) - 1)
    def _():
        for c, (bi, _, _) in enumerate(sls):
            st = z_ref[c].T
            sf_ref[bi, 2 * (c % npair)] = st[:N_B, :N_B]
            sf_ref[bi, 2 * (c % npair) + 1] = st[N_B:, N_B:]


def _rwkv_scan_seq(r, lw, k, v, a, b, *, tc):
    nb, t, _ = r.shape
    spec = pl.BlockSpec((nb, tc, D_B), lambda c: (0, c, 0))
    return pl.pallas_call(
        functools.partial(_rwkv_scan_seq_kernel, nb=nb),
        grid=(t // tc,),
        in_specs=[spec] * 6,
        out_specs=[spec, pl.BlockSpec((nb, H_B, N_B, N_B), lambda c: (0, 0, 0, 0))],
        out_shape=[jax.ShapeDtypeStruct((nb, t, D_B), F32), jax.ShapeDtypeStruct((nb, H_B, N_B, N_B), F32)],
        scratch_shapes=[pltpu.VMEM((nb * H_B // 2, LANES, LANES), F32)],
        compiler_params=_cparams("arbitrary"),
        name="rwkv_scan_seq",
    )(r, lw, k, v, a, b)


def _rwkv_scan_rows_kernel(r_ref, w_ref, k_ref, v_ref, a_ref, b_ref, s0_ref, y_ref, sf_ref, *, tb):
    def group(gi, carry):
        base = pl.multiple_of(gi * SUBLANES, SUBLANES)
        for p in range(H_B // 2):
            sl = (pl.ds(base, SUBLANES), slice(p * LANES, (p + 1) * LANES))
            a, w, b, k, v, r = (ref[sl] for ref in (a_ref, w_ref, b_ref, k_ref, v_ref, r_ref))
            ys = []
            for j in range(SUBLANES):
                row = slice(j, j + 1)
                S0 = jnp.concatenate([s0_ref[base + j, 2 * p], s0_ref[base + j, 2 * p + 1]], axis=1)
                S, y = _scan_pair_step(S0, a[row], w[row], b[row], k[row], v[row], r[row])
                sf_ref[base + j, 2 * p] = S[:, :N_B]
                sf_ref[base + j, 2 * p + 1] = S[:, N_B:]
                ys.append(y)
            y_ref[sl] = jnp.concatenate(ys, axis=0)
        return carry

    lax.fori_loop(0, tb // SUBLANES, group, 0)


def _rwkv_scan_rows(r, w, k, v, a, b, s0, *, tb):
    m = r.shape[0]
    spec = pl.BlockSpec((tb, D_B), lambda i: (i, 0))
    sspec = pl.BlockSpec((tb, H_B, N_B, N_B), lambda i: (i, 0, 0, 0))
    return pl.pallas_call(
        functools.partial(_rwkv_scan_rows_kernel, tb=tb),
        grid=(m // tb,),
        in_specs=[spec] * 6 + [sspec],
        out_specs=[spec, sspec],
        out_shape=[jax.ShapeDtypeStruct((m, D_B), F32), jax.ShapeDtypeStruct((m, H_B, N_B, N_B), F32)],
        compiler_params=_cparams("parallel"),
        name="rwkv_scan_rows",
    )(r, w, k, v, a, b, s0)


def _rwkv_post_kernel(y_ref, rkv_ref, g_ref, lg_ref, lb_ref, bd_ref, o_ref):
    y = y_ref[...]
    bd = bd_ref[...]
    d = y - _seg_sum(y, bd) * (1.0 / N_B)
    var = _seg_sum(d * d, bd) * (1.0 / N_B)
    yn = d * lax.rsqrt(var + GN_EPS_B) * lg_ref[...] + lb_ref[...]
    o_ref[...] = ((yn + rkv_ref[...]) * g_ref[...]).astype(o_ref.dtype)


def _rwkv_post(y, rkv, g, lg, lb, bd, *, tm):
    m = y.shape[0]
    spec = pl.BlockSpec((tm, D_B), lambda i: (i, 0))
    row = pl.BlockSpec((1, D_B), lambda i: (0, 0))
    return pl.pallas_call(
        _rwkv_post_kernel,
        grid=(m // tm,),
        in_specs=[spec, spec, spec, row, row, pl.BlockSpec((D_B, D_B), lambda i: (0, 0))],
        out_specs=spec,
        out_shape=jax.ShapeDtypeStruct((m, D_B), BF16),
        compiler_params=_cparams("parallel"),
        name="rwkv_post",
    )(y, rkv, g, lg, lb, bd)


def _lam_full(lq1, lk1, lq2, lk2, lam_init):
    return (jnp.exp(jnp.sum(lq1 * lk1, axis=-1, keepdims=True))
            - jnp.exp(jnp.sum(lq2 * lk2, axis=-1, keepdims=True)) + lam_init)


ONES_ROWS = 16


def _attn_prompt_kernel(q_ref, k_ref, vt_ref, lq1, lk1, lq2, lk2, sg_ref, o_ref, m1, a1, m2, a2, *, tk, lam_init):
    qi = pl.program_id(2)
    for m_ref, a_ref in ((m1, a1), (m2, a2)):
        m_ref[...] = jnp.full_like(m_ref, -jnp.inf)
        a_ref[...] = jnp.zeros_like(a_ref)

    q = q_ref[0]
    lane = lax.broadcasted_iota(jnp.int32, q.shape, 1)
    zero = jnp.zeros_like(q)
    q_sub = (jnp.where(lane < D_HEAD, q, zero), jnp.where(lane < D_HEAD, zero, q))
    ones = jnp.ones((ONES_ROWS, tk), BF16)

    def update(kis, masked):
        ks = [k_ref[0, pl.ds(pl.multiple_of(ki * tk, tk), tk), :] for ki in kis]
        vts = [jnp.concatenate([vt_ref[0, ki], ones], axis=0) for ki in kis]
        units = [(qs[c0:c0 + ATTN_STRIP], m_ref, a_ref, c0)
                 for qs, m_ref, a_ref in ((q_sub[0], m1, a1), (q_sub[1], m2, a2))
                 for c0 in range(0, q.shape[0], ATTN_STRIP)]
        sts = [[lax.dot_general(k, qs, (((1,), (1,)), ((), ())), preferred_element_type=F32)
                for k in ks] for qs, _, _, _ in units]
        for st_blocks, (_, m_ref, a_ref, c0) in zip(sts, units):
            cols = slice(c0, c0 + ATTN_STRIP)
            if masked:
                kr = lax.broadcasted_iota(jnp.int32, st_blocks[0].shape, 0)
                qc = lax.broadcasted_iota(jnp.int32, st_blocks[0].shape, 1) + c0
                st_blocks = [jnp.where(kr <= qc, st, -jnp.inf) for st in st_blocks]
            m_old = m_ref[:, cols]
            m_new = m_old
            for st in st_blocks:
                m_new = jnp.maximum(m_new, jnp.max(st, axis=0, keepdims=True))
            acc = jnp.exp2(m_old - m_new) * a_ref[:, cols]
            for st, vt1 in zip(st_blocks, vts):
                acc = acc + jnp.dot(vt1, jnp.exp2(st - m_new).astype(BF16), preferred_element_type=F32)
            a_ref[:, cols] = acc
            m_ref[:, cols] = m_new

    def group_body(j, carry):
        update([ATTN_GROUP * j + i for i in range(ATTN_GROUP)], False)
        return carry

    def single_body(ki, carry):
        update([ki], False)
        return carry

    n_groups = lax.shift_right_logical(qi, ATTN_GROUP.bit_length() - 1)
    lax.fori_loop(0, n_groups, group_body, 0)
    lax.fori_loop(n_groups * ATTN_GROUP, qi, single_body, 0)
    update([qi], True)
    lam = _lam_full(lq1[...], lk1[...], lq2[...], lk2[...], lam_init)
    d = 2 * D_HEAD
    ot = a1[:d, :] / a1[d:d + 1, :] - lam * (a2[:d, :] / a2[d:d + 1, :])
    on = ot * lax.rsqrt(jnp.mean(ot * ot, axis=0, keepdims=True) + SUBLN_EPS)
    o_ref[0] = (on.T * sg_ref[...] * (1.0 - lam_init)).astype(o_ref.dtype)


def _attn_prompt(q, k, vt, lams, subln_g, lam_init, *, tq):
    nb, t, _ = q.shape
    tk = vt.shape[3]
    assert tk == tq
    qspec = pl.BlockSpec((1, tq, LANES), lambda b, h, i: (b, i, h))
    kspec = pl.BlockSpec((1, t, LANES), lambda b, h, i: (b, 0, h))
    vspec = pl.BlockSpec((1, t // tk, LANES, tk), lambda b, h, i: (b, 0, h, 0))
    row = lambda n: pl.BlockSpec((1, n), lambda b, h, i: (0, 0))
    acc = 2 * D_HEAD + ONES_ROWS
    return pl.pallas_call(
        functools.partial(_attn_prompt_kernel, tk=tk, lam_init=lam_init),
        grid=(nb, H_C, t // tq),
        in_specs=[qspec, kspec, vspec, row(D_HEAD), row(D_HEAD), row(D_HEAD), row(D_HEAD), row(LANES)],
        out_specs=qspec,
        out_shape=jax.ShapeDtypeStruct(q.shape, BF16),
        scratch_shapes=[pltpu.VMEM((1, tq), F32), pltpu.VMEM((acc, tq), F32),
                        pltpu.VMEM((1, tq), F32), pltpu.VMEM((acc, tq), F32)],
        compiler_params=_cparams("parallel", "parallel", "arbitrary"),
        name="attn_prompt",
    )(q, k, vt, *lams, subln_g)


def _attn_sample_kernel(pt_ref, q_ref, kn_ref, vn_ref, *refs, npg, lam_init):
    k_refs = refs[:npg]
    v_refs = refs[npg:2 * npg]
    lq1, lk1, lq2, lk2, sg_ref, o_ref, m_ref, l_ref, acc_ref = refs[2 * npg:]
    g = pl.program_id(1)
    nrow, ncol = 2 * H_C, PAGE_SIZE * H_C
    q = q_ref[0]
    lo = lax.broadcasted_iota(jnp.int32, q.shape, 1) < D_HEAD
    qz = jnp.concatenate([jnp.where(lo, q, 0.0), jnp.where(lo, 0.0, q)], axis=0).astype(BF16)
    valid = ((lax.broadcasted_iota(jnp.int32, (nrow, ncol), 1) & (H_C - 1))
             == (lax.broadcasted_iota(jnp.int32, (nrow, ncol), 0) & (H_C - 1)))

    @pl.when(g == 0)
    def _():
        as_mxu = lambda ref: jnp.concatenate([ref[0], ref[0]], axis=0).astype(BF16).astype(F32)
        m_ref[...] = jnp.sum(qz.astype(F32) * as_mxu(kn_ref), axis=1, keepdims=True)
        l_ref[...] = jnp.ones_like(l_ref)
        acc_ref[...] = as_mxu(vn_ref)

    sts = []
    for r in range(npg):
        kmat = k_refs[r][...].reshape(ncol, LANES).astype(BF16)
        st = lax.dot_general(qz, kmat, (((1,), (1,)), ((), ())), preferred_element_type=F32)
        sts.append(jnp.where(valid, st, -jnp.inf))
    m_old = m_ref[...]
    m_new = m_old
    for st in sts:
        m_new = jnp.maximum(m_new, jnp.max(st, axis=1, keepdims=True))
    alpha = jnp.exp2(m_old - m_new)
    l = alpha * l_ref[...]
    acc = alpha * acc_ref[...]
    for st, v_ref in zip(sts, v_refs):
        p = jnp.exp2(st - m_new)
        l = l + jnp.sum(p, axis=1, keepdims=True)
        acc = acc + jnp.dot(p.astype(BF16), v_ref[...].reshape(ncol, LANES).astype(BF16),
                            preferred_element_type=F32)
    m_ref[...] = m_new
    l_ref[...] = l
    acc_ref[...] = acc

    @pl.when(g == pl.num_programs(1) - 1)
    def _():
        o = acc_ref[...] / l_ref[...]
        lam = _lam_full(lq1[...], lk1[...], lq2[...], lk2[...], lam_init)
        o_ref[0] = (_rms(o[:H_C] - lam * o[H_C:], sg_ref[...], SUBLN_EPS) * (1.0 - lam_init)).astype(o_ref.dtype)


def _attn_sample(q, k_new, v_new, cache_k, cache_v, page_table, layer, lams, subln_g, lam_init, *, npg):
    nb = q.shape[0]
    n_pages = page_table.shape[1]
    hspec = pl.BlockSpec((1, H_C, LANES), lambda b, g, pt: (b, 0, 0))
    row = lambda n: pl.BlockSpec((1, n), lambda b, g, pt: (0, 0))

    def page_spec(r):
        return pl.BlockSpec((None, None, PAGE_SIZE, H_C, LANES),
                            lambda b, g, pt: (layer, pt[b, g * npg + r], 0, 0, 0))

    grid_spec = pltpu.PrefetchScalarGridSpec(
        num_scalar_prefetch=1,
        grid=(nb, n_pages // npg),
        in_specs=[hspec, hspec, hspec] + [page_spec(r) for r in range(npg)] * 2
                 + [row(D_HEAD), row(D_HEAD), row(D_HEAD), row(D_HEAD), row(LANES)],
        out_specs=hspec,
        scratch_shapes=[pltpu.VMEM((2 * H_C, 1), F32), pltpu.VMEM((2 * H_C, 1), F32),
                        pltpu.VMEM((2 * H_C, LANES), F32)],
    )
    return pl.pallas_call(
        functools.partial(_attn_sample_kernel, npg=npg, lam_init=lam_init),
        grid_spec=grid_spec,
        out_shape=jax.ShapeDtypeStruct((nb, H_C, LANES), BF16),
        compiler_params=_cparams("parallel", "arbitrary"),
        name="attn_sample",
    )(page_table, q, k_new, v_new, *([cache_k] * npg), *([cache_v] * npg), *lams, subln_g)


def _merge_kernel(x_ref, oa_ref, ob_ref, oc_ref, cg_ref, wa_ref, wb_ref, wc_ref, wo_ref, xo_ref):
    gates = cg_ref[...].astype(F32)
    dot = lambda a, w: jnp.dot(a, w, preferred_element_type=F32)
    m = (gates[:, 0:D_MODEL] * dot(oa_ref[...], wa_ref[...])
         + gates[:, D_MODEL:2 * D_MODEL] * dot(ob_ref[...], wb_ref[...])
         + gates[:, 2 * D_MODEL:3 * D_MODEL] * dot(oc_ref[...], wc_ref[...]))
    xo_ref[...] = x_ref[...] + dot(m.astype(BF16), wo_ref[...])


def _merge(x, oa, ob, oc, cg, wa, wb, wc, wo, *, tm):
    m = x.shape[0]
    rows = lambda n: pl.BlockSpec((tm, n), lambda i: (i, 0))
    full = lambda a: pl.BlockSpec(a.shape, lambda i: (0, 0))
    return pl.pallas_call(
        _merge_kernel,
        grid=(m // tm,),
        in_specs=[rows(D_MODEL), rows(D_A), rows(D_B), rows(D_C), rows(3 * D_MODEL),
                  full(wa), full(wb), full(wc), full(wo)],
        out_specs=rows(D_MODEL),
        out_shape=jax.ShapeDtypeStruct((m, D_MODEL), F32),
        compiler_params=_cparams("parallel"),
        name="merge",
    )(x, oa, ob, oc, cg, wa, wb, wc, wo)


def _ffn_kernel(x_ref, g_ref, wu_ref, wd_ref, gf_ref, o_ref, h_ref, acc_ref, *, final):
    j = pl.program_id(1)

    @pl.when(j == 0)
    def _():
        h_ref[...] = _rms(x_ref[...], g_ref[...], EPS).astype(BF16)
        acc_ref[...] = jnp.zeros_like(acc_ref)

    up = jnp.dot(h_ref[...], wu_ref[...], preferred_element_type=F32)
    act = jnp.square(jnp.maximum(up, 0.0)).astype(BF16)
    acc_ref[...] += jnp.dot(act, wd_ref[...], preferred_element_type=F32)

    @pl.when(j == pl.num_programs(1) - 1)
    def _():
        xn = x_ref[...] + acc_ref[...]
        o_ref[...] = _rms(xn, gf_ref[...], EPS) if final else xn


def _ffn(x, g, wu, wd, gf, *, final, tm, tf):
    m = x.shape[0]
    row = pl.BlockSpec((1, D_MODEL), lambda i, j: (0, 0))
    return pl.pallas_call(
        functools.partial(_ffn_kernel, final=final),
        grid=(m // tm, D_FF // tf),
        in_specs=[pl.BlockSpec((tm, D_MODEL), lambda i, j: (i, 0)), row,
                  pl.BlockSpec((D_MODEL, tf), lambda i, j: (0, j)),
                  pl.BlockSpec((tf, D_MODEL), lambda i, j: (j, 0)), row],
        out_specs=pl.BlockSpec((tm, D_MODEL), lambda i, j: (i, 0)),
        out_shape=jax.ShapeDtypeStruct((m, D_MODEL), F32),
        scratch_shapes=[pltpu.VMEM((tm, D_MODEL), BF16), pltpu.VMEM((tm, D_MODEL), F32)],
        compiler_params=_cparams("parallel", "arbitrary"),
        name="ffn",
    )(x, g, wu, wd, gf)


def _pad_b_cols(a):
    o = 3 * D_B
    z = lambda n: jnp.zeros(a.shape[:-1] + (n,), a.dtype)
    return jnp.concatenate([a[..., :o + LORA_W], z(LW_PAD - LORA_W),
                            a[..., o + LORA_W:o + LORA_W + LORA_A], z(LA_PAD - LORA_A),
                            a[..., o + LORA_W + LORA_A:], z(LG_PAD - LORA_G)], axis=-1)


def _unpad_b_cols(a):
    o = 3 * D_B
    return jnp.concatenate([a[..., :o + LORA_W], a[..., o + LW_PAD:o + LW_PAD + LORA_A],
                            a[..., o + LW_PAD + LA_PAD:o + LW_PAD + LA_PAD + LORA_G]], axis=-1)


def _pad_rows(a, n):
    return jnp.concatenate([a, jnp.zeros((n - a.shape[0],) + a.shape[1:], a.dtype)], axis=0)


def _ones_blockdiag(n, seg):
    i = np.arange(n) // seg
    return jnp.asarray((i[:, None] == i[None, :]).astype(np.float32), dtype=BF16)


def _layer_weights(l, p):
    w_in = p["w_in"][l]
    o_b = N_A_COLS
    o_q = o_b + N_B_COLS
    w = dict(
        norm_mix=p["norm_mix"][l].reshape(1, D_MODEL),
        w_a=w_in[:, :o_b].astype(BF16),
        w_b=_pad_b_cols(w_in[:, o_b:o_q]).astype(BF16),
        w_q=w_in[:, o_q:o_q + D_C].astype(BF16),
        w_k=w_in[:, o_q + D_C:o_q + 2 * D_C].astype(BF16),
        w_v=w_in[:, o_q + 2 * D_C:o_q + 3 * D_C].astype(BF16),
        w_g=w_in[:, o_q + 3 * D_C:].astype(BF16),
        lng=p["sgu_ln_g"][l].reshape(1, D_A), lnb=p["sgu_ln_b"][l].reshape(1, D_A),
        sgu_w=p["sgu_w"][l],
        sgu_bias_td=jnp.repeat(p["sgu_b"][l].T, D_A // G_A, axis=1),
        sgu_w0=jnp.repeat(p["sgu_w"][l][:, 0, 0], D_A // G_A).reshape(1, D_A),
        sgu_b0=jnp.repeat(p["sgu_b"][l][:, 0], D_A // G_A).reshape(1, D_A),
        rwkv=[_pad_b_cols(p["shift_mu"][l]).reshape(1, N_B_PAD),
              p["w0"][l].reshape(1, D_B), _pad_rows(p["w2"][l], LW_PAD).astype(BF16),
              p["a0"][l].reshape(1, D_B), _pad_rows(p["a2"][l], LA_PAD).astype(BF16),
              _pad_rows(p["g2"][l], LG_PAD).astype(BF16),
              p["k_k"][l].reshape(1, D_B), p["k_a"][l].reshape(1, D_B), p["r_k"][l].reshape(1, D_B),
              _ones_blockdiag(D_B, N_B)],
        lnx_g=p["lnx_g"][l].reshape(1, D_B), lnx_b=p["lnx_b"][l].reshape(1, D_B),
        lams=[p[n][l].reshape(1, D_HEAD) for n in ("lam_q1", "lam_k1", "lam_q2", "lam_k2")],
        subln_g=p["subln_g"][l].reshape(1, 2 * D_HEAD),
        w_br_a=p["w_br_a"][l].astype(BF16), w_br_b=p["w_br_b"][l].astype(BF16),
        w_br_c=p["w_br_c"][l].astype(BF16), w_out=p["w_out"][l].astype(BF16),
        norm_ffn=p["norm_ffn"][l].reshape(1, D_MODEL),
        w_up=p["w_up"][l].astype(BF16), w_down=p["w_down"][l].astype(BF16),
        norm_final=p["norm_final"].reshape(1, D_MODEL),
    )
    return w


def _project_all(x, w, rope_tabs, tm, t_seq=None, kv_stack=None):
    pj = functools.partial(_proj, x, w["norm_mix"], tm=tm, tn=D_MODEL)
    qscale = math.log2(math.e) / math.sqrt(D_HEAD)
    (ca,) = pj(w["w_a"], out_dtypes=(F32,), name="proj_a")
    (cb,) = pj(w["w_b"], out_dtypes=(F32,), name="proj_b")
    (cg,) = pj(w["w_g"], out_dtypes=(BF16,), gate=True, name="proj_g")
    if t_seq is None:
        (q,) = pj(w["w_q"], out_dtypes=(F32,), rope_tabs=rope_tabs, scale=qscale, name="proj_q")
        (kf,) = pj(w["w_k"], out_dtypes=(F32,), rope_tabs=rope_tabs, name="proj_k")
        (vf,) = pj(w["w_v"], out_dtypes=(F32,), name="proj_v")
        return ca, cb, cg, q, kf, vf
    (q,) = pj(w["w_q"], out_dtypes=(BF16,), rope_tabs=rope_tabs, scale=qscale, name="proj_q")
    depth, layer, k_buf, v_buf = kv_stack
    kf, kb = pj(w["w_k"], out_dtypes=(F32, BF16), rope_tabs=rope_tabs, stack=(depth, layer, k_buf), name="proj_k")
    vf, vt = _proj(x, w["norm_mix"], w["w_v"], tm=ATTN_BLOCK, tn=D_MODEL, out_dtypes=(F32, BF16), t_seq=t_seq,
                   stack=(depth, layer, v_buf), name="proj_v")
    return ca, cb, cg, q, kf, vf, kb, vt


def _prompt_layer(x, l, w, rope_tabs, nb, t, final, kv_stack):
    lam_init = 0.8 - 0.6 * math.exp(-0.3 * l)
    ca, cb, cg, qb, kf, vf, kb, vt = _project_all(x, w, rope_tabs, 1024, t_seq=t, kv_stack=kv_stack)
    o_a = _sgu_prompt(ca, w["lng"], w["lnb"], w["sgu_w"], w["sgu_bias_td"], tm=512)
    prev0 = jnp.zeros((nb, 1, N_B_PAD), F32)
    *rw, last = _rwkv_pre(cb.reshape(nb, t, N_B_PAD), prev0, w["rwkv"], seq=True, tm=256)
    r, dec, k2, v, na, bb, g, rkv = rw
    y, s_fin = _rwkv_scan_seq(r, dec, k2, v, na, bb, tc=64)
    flat = lambda a: a.reshape(nb * t, D_B)
    o_b = _rwkv_post(flat(y), flat(rkv), flat(g), w["lnx_g"], w["lnx_b"], w["rwkv"][-1], tm=1024)
    b3 = lambda a: a.reshape(nb, t, D_C)
    o_c = _attn_prompt(b3(qb), b3(kb), vt, w["lams"], w["subln_g"], lam_init, tq=ATTN_BLOCK)
    x = _merge(x, o_a, o_b, o_c.reshape(nb * t, D_C), cg, w["w_br_a"], w["w_br_b"], w["w_br_c"], w["w_out"], tm=256)
    x = _ffn(x, w["norm_ffn"], w["w_up"], w["w_down"], w["norm_final"], final=final, tm=1024, tf=1024)
    return x, kf, vf, s_fin, _unpad_b_cols(last[:, 0, :])


def _sample_layer(x, l, w, rope_tabs, cache_k, cache_v, page_table, prev_shift, s0, final):
    nb = x.shape[0]
    lam_init = 0.8 - 0.6 * math.exp(-0.3 * l)
    ca, cb, cg, qb, kf, vf = _project_all(x, w, rope_tabs, nb)
    o_a, va = _sgu_sample(ca, w["lng"], w["lnb"], w["sgu_w0"], w["sgu_b0"])
    r, dec, k2, v, na, bb, g, rkv = _rwkv_pre(cb, _pad_b_cols(prev_shift), w["rwkv"], seq=False, tm=nb)
    y, s_new = _rwkv_scan_rows(r, dec, k2, v, na, bb, s0, tb=32)
    o_b = _rwkv_post(y, rkv, g, w["lnx_g"], w["lnx_b"], w["rwkv"][-1], tm=nb)
    h3 = lambda a: a.reshape(nb, H_C, LANES)
    o_c = _attn_sample(h3(qb), h3(kf), h3(vf), cache_k, cache_v, page_table, l,
                       w["lams"], w["subln_g"], lam_init, npg=8)
    x = _merge(x, o_a, o_b, o_c.reshape(nb, D_C), cg, w["w_br_a"], w["w_br_b"], w["w_br_c"], w["w_out"], tm=nb)
    x = _ffn(x, w["norm_ffn"], w["w_up"], w["w_down"], w["norm_final"], final=final, tm=nb, tf=1024)
    return x, kf, vf, s_new, _unpad_b_cols(cb), va


def kernel(x_prompt, x_sample, cache_k, cache_v, state_rwkv, state_shift, page_table, norm_mix, w_in, sgu_ln_g, sgu_ln_b, sgu_w, sgu_b, shift_mu, w0, w2, a0, a2, g2, k_k, k_a, r_k, lnx_g, lnx_b, lam_q1, lam_k1, lam_q2, lam_k2, subln_g, w_br_a, w_br_b, w_br_c, w_out, norm_ffn, w_up, w_down, norm_final):
    p = dict(norm_mix=norm_mix, w_in=w_in, sgu_ln_g=sgu_ln_g, sgu_ln_b=sgu_ln_b, sgu_w=sgu_w, sgu_b=sgu_b,
             shift_mu=shift_mu, w0=w0, w2=w2, a0=a0, a2=a2, g2=g2, k_k=k_k, k_a=k_a, r_k=r_k, lnx_g=lnx_g,
             lnx_b=lnx_b, lam_q1=lam_q1, lam_k1=lam_k1, lam_q2=lam_q2, lam_k2=lam_k2, subln_g=subln_g,
             w_br_a=w_br_a, w_br_b=w_br_b, w_br_c=w_br_c, w_out=w_out, norm_ffn=norm_ffn, w_up=w_up,
             w_down=w_down, norm_final=norm_final)
    depth = w_in.shape[0]
    bp, tp, _ = x_prompt.shape
    bs, ts, _ = x_sample.shape
    tabs_p = _rope_tables(jnp.arange(tp))
    tabs_s = _rope_tables(jnp.full((bs * ts,), PAST_LEN, jnp.int32))
    xp = x_prompt.reshape(bp * tp, D_MODEL)
    xs = x_sample.reshape(bs * ts, D_MODEL)
    outs_p, outs_s = [], []
    kp = vp = None
    for l in range(depth):
        w = _layer_weights(l, p)
        final = l == depth - 1
        xp, kp, vp, sp, shp = _prompt_layer(xp, l, w, tabs_p, bp, tp, final, (depth, l, kp, vp))
        xs, ks, vs, ss, shs, va = _sample_layer(xs, l, w, tabs_s, cache_k, cache_v, page_table,
                                                state_shift[l], state_rwkv[l], final)
        outs_p.append((sp, shp))
        outs_s.append((ks.reshape(bs, ts, H_C, 2 * D_HEAD), vs.reshape(bs, ts, H_C, 2 * D_HEAD), ss, shs,
                       va.reshape(bs, ts, D_A)))
    stack = lambda outs, i: jnp.stack([o[i] for o in outs])
    return (xp.reshape(bp, tp, D_MODEL), xs.reshape(bs, ts, D_MODEL),
            kp.reshape(depth, bp, tp, H_C, 2 * D_HEAD), vp.reshape(depth, bp, tp, H_C, 2 * D_HEAD),
            stack(outs_p, 0), stack(outs_p, 1),
            stack(outs_s, 0), stack(outs_s, 1), stack(outs_s, 2), stack(outs_s, 3), stack(outs_s, 4))
```

```python
import functools
import math

import numpy as np
import jax
import jax.numpy as jnp
from jax import lax
from jax.experimental import pallas as pl
from jax.experimental.pallas import tpu as pltpu

F32 = jnp.float32
BF16 = jnp.bfloat16

D_MODEL = 1024
PAST_LEN = 2048
PAGE_SIZE = 128
CHUNK = 128
D_A = 512
G_A = 8
H_B = 8
N_B = 64
D_B = H_B * N_B
LORA_W = 64
LORA_A = 64
LORA_G = 160
GN_EPS_B = 64e-5
H_C = 8
D_HEAD = 64
D_C = H_C * 2 * D_HEAD
ROPE_DIM = D_HEAD // 4
ROPE_THETA = 500000.0
SUBLN_EPS = 1e-5
D_FF = 4 * D_MODEL
EPS = 1e-6
N_A_COLS = 2 * D_A
N_B_COLS = 3 * D_B + LORA_W + LORA_A + LORA_G
LANES = 128
SUBLANES = 8
LW_PAD = 128
LA_PAD = 128
LG_PAD = 256
N_B_PAD = 3 * D_B + LW_PAD + LA_PAD + LG_PAD
VMEM_LIMIT = 52 * 1024 * 1024
ATTN_BLOCK = 512
ATTN_GROUP = 4
ATTN_STRIP = 256


def _cparams(*sem):
    return pltpu.CompilerParams(dimension_semantics=sem, vmem_limit_bytes=VMEM_LIMIT)


def _rms(x, g, eps):
    return x * lax.rsqrt(jnp.mean(x * x, axis=-1, keepdims=True) + eps) * g


def _proj_kernel(*refs, rope, scale, n_out, aliased, gate):
    x_ref, g_ref, w_ref = refs[:3]
    pos = 3
    if rope:
        c_ref, s1_ref, s2_ref = refs[3:6]
        pos = 6
    pos += aliased
    out_refs = refs[pos:pos + n_out]
    h_ref = refs[pos + n_out]

    @pl.when(pl.program_id(1) == 0)
    def _():
        h_ref[...] = _rms(x_ref[...], g_ref[...], EPS).astype(BF16)

    acc = jnp.dot(h_ref[...], w_ref[...], preferred_element_type=F32)
    if rope:
        c, s1, s2 = c_ref[...], s1_ref[...], s2_ref[...]
        parts = []
        for hh in range(acc.shape[1] // LANES):
            a = acc[:, hh * LANES:(hh + 1) * LANES]
            parts.append(a * c + pltpu.roll(a, ROPE_DIM // 2, 1) * s1
                         + pltpu.roll(a, LANES - ROPE_DIM // 2, 1) * s2)
        acc = jnp.concatenate(parts, axis=1)
    if scale != 1.0:
        acc = acc * scale
    if gate:
        acc = jax.nn.sigmoid(acc)
    for o_ref in out_refs:
        if len(o_ref.shape) == 4:
            o_ref[0, 0] = acc.T.astype(o_ref.dtype)
        else:
            o_ref[...] = acc.astype(o_ref.dtype)


def _proj(x, g, w, *, tm, tn, out_dtypes, rope_tabs=None, scale=1.0, t_seq=None, stack=None, gate=False,
          name="proj"):
    m, d = x.shape
    n = w.shape[1]
    in_specs = [pl.BlockSpec((tm, d), lambda i, j: (i, 0)),
                pl.BlockSpec((1, d), lambda i, j: (0, 0)),
                pl.BlockSpec((d, tn), lambda i, j: (0, j))]
    args = [x, g, w]
    if rope_tabs is not None:
        nt = rope_tabs[0].shape[0] // tm
        in_specs += [pl.BlockSpec((tm, LANES), lambda i, j: (i % nt, 0))] * 3
        args += list(rope_tabs)
    out_specs = [pl.BlockSpec((tm, tn), lambda i, j: (i, j)) for _ in out_dtypes]
    out_shape = [jax.ShapeDtypeStruct((m, n), dt) for dt in out_dtypes]
    if t_seq is not None:
        nts = t_seq // tm
        out_specs[-1] = pl.BlockSpec((1, 1, tn, tm), lambda i, j: (i // nts, i % nts, j, 0))
        out_shape[-1] = jax.ShapeDtypeStruct((m // t_seq, nts, n, tm), out_dtypes[-1])
    aliases = {}
    if stack is not None:
        layer, buf = stack
        off = layer * (m // tm)
        out_specs[0] = pl.BlockSpec((tm, tn), lambda i, j: (i + off, j))
        out_shape[0] = jax.ShapeDtypeStruct(buf.shape, out_dtypes[0])
        aliases = {len(args): 0}
        in_specs.append(pl.BlockSpec(memory_space=pl.ANY))
        args.append(buf)
    outs = pl.pallas_call(
        functools.partial(_proj_kernel, rope=rope_tabs is not None, scale=scale, n_out=len(out_dtypes),
                          aliased=len(aliases), gate=gate),
        grid=(m // tm, n // tn),
        in_specs=in_specs,
        out_specs=out_specs,
        out_shape=out_shape,
        input_output_aliases=aliases,
        scratch_shapes=[pltpu.VMEM((tm, d), BF16)],
        compiler_params=_cparams("parallel", "arbitrary"),
        name=name,
    )(*args)
    return outs


def _rope_kernel(pos_ref, invf_ref, c_ref, s1_ref, s2_ref):
    ang = pos_ref[...] * invf_ref[...]
    lane = lax.broadcasted_iota(jnp.int32, ang.shape, 1) % D_HEAD
    first = lane < ROPE_DIM // 2
    second = (lane >= ROPE_DIM // 2) & (lane < ROPE_DIM)
    cos, sin = jnp.cos(ang), jnp.sin(ang)
    c_ref[...] = jnp.where(first | second, cos, 1.0)
    s1_ref[...] = jnp.where(second, sin, 0.0)
    s2_ref[...] = jnp.where(first, -sin, 0.0)


def _rope_tables(pos):
    t = pos.shape[0]
    half = ROPE_DIM // 2
    inv_freq = ROPE_THETA ** (-jnp.arange(half, dtype=F32) / half)
    blk = jnp.concatenate([inv_freq, inv_freq, jnp.zeros((D_HEAD - ROPE_DIM,), F32)])
    invf = jnp.concatenate([blk, blk]).reshape(1, LANES)
    tm = min(t, 512)
    return pl.pallas_call(
        _rope_kernel,
        grid=(t // tm,),
        in_specs=[pl.BlockSpec((tm, 1), lambda i: (i, 0)), pl.BlockSpec((1, LANES), lambda i: (0, 0))],
        out_specs=[pl.BlockSpec((tm, LANES), lambda i: (i, 0))] * 3,
        out_shape=[jax.ShapeDtypeStruct((t, LANES), F32)] * 3,
        compiler_params=_cparams("parallel"),
        name="rope_tables",
    )(pos.astype(F32).reshape(t, 1), invf)


def _gelu_ln(ca, lng, lnb):
    gx = 0.5 * ca * (1.0 + lax.erf(ca * math.sqrt(0.5)))
    u, v = gx[:, :D_A], gx[:, D_A:]
    d = v - jnp.mean(v, axis=-1, keepdims=True)
    va = d * lax.rsqrt(jnp.mean(d * d, axis=-1, keepdims=True) + EPS) * lng + lnb
    return u, va


def _sgu_prompt_kernel(ca_ref, lng_ref, lnb_ref, w_ref, b_ref, oa_ref, *, nchunk):
    u, va = _gelu_ln(ca_ref[...], lng_ref[...], lnb_ref[...])
    vab = va.astype(BF16)
    row = lax.broadcasted_iota(jnp.int32, (CHUNK, CHUNK), 0)
    col = lax.broadcasted_iota(jnp.int32, (CHUNK, CHUNK), 1)
    wcat = jnp.concatenate([jnp.where(col <= row, w_ref[g], 0.0).astype(BF16) for g in range(G_A)], axis=1)
    grp = lax.broadcasted_iota(jnp.int32, (CHUNK, D_A), 1) // (D_A // G_A)
    for ci in range(nchunk):
        vc = vab[ci * CHUNK:(ci + 1) * CHUNK]
        vbig = jnp.concatenate([jnp.where(grp == g, vc, jnp.zeros_like(vc)) for g in range(G_A)], axis=0)
        s = jnp.dot(wcat, vbig, preferred_element_type=F32) + b_ref[...]
        oa_ref[ci * CHUNK:(ci + 1) * CHUNK, :] = (u[ci * CHUNK:(ci + 1) * CHUNK] * s).astype(oa_ref.dtype)


def _sgu_prompt(ca, lng, lnb, w, bias_td, *, tm):
    m = ca.shape[0]
    return pl.pallas_call(
        functools.partial(_sgu_prompt_kernel, nchunk=tm // CHUNK),
        grid=(m // tm,),
        in_specs=[pl.BlockSpec((tm, N_A_COLS), lambda i: (i, 0)),
                  pl.BlockSpec((1, D_A), lambda i: (0, 0)),
                  pl.BlockSpec((1, D_A), lambda i: (0, 0)),
                  pl.BlockSpec((G_A, CHUNK, CHUNK), lambda i: (0, 0, 0)),
                  pl.BlockSpec((CHUNK, D_A), lambda i: (0, 0))],
        out_specs=pl.BlockSpec((tm, D_A), lambda i: (i, 0)),
        out_shape=jax.ShapeDtypeStruct((m, D_A), BF16),
        compiler_params=_cparams("parallel"),
        name="sgu_prompt",
    )(ca, lng, lnb, w, bias_td)


def _sgu_sample_kernel(ca_ref, lng_ref, lnb_ref, w0_ref, b0_ref, oa_ref, va_ref):
    u, va = _gelu_ln(ca_ref[...], lng_ref[...], lnb_ref[...])
    va_ref[...] = va
    oa_ref[...] = (u * (va * w0_ref[...] + b0_ref[...])).astype(oa_ref.dtype)


def _sgu_sample(ca, lng, lnb, w0, b0):
    m = ca.shape[0]
    row = lambda n: pl.BlockSpec((1, n), lambda i: (0, 0))
    return pl.pallas_call(
        _sgu_sample_kernel,
        grid=(1,),
        in_specs=[pl.BlockSpec((m, N_A_COLS), lambda i: (0, 0)), row(D_A), row(D_A), row(D_A), row(D_A)],
        out_specs=[pl.BlockSpec((m, D_A), lambda i: (0, 0))] * 2,
        out_shape=[jax.ShapeDtypeStruct((m, D_A), BF16), jax.ShapeDtypeStruct((m, D_A), F32)],
        compiler_params=_cparams("arbitrary"),
        name="sgu_sample",
    )(ca, lng, lnb, w0, b0)


def _seg_sum(x, ones_bd):
    hi = x.astype(BF16)
    lo = (x - hi.astype(F32)).astype(BF16)
    return (jnp.dot(hi, ones_bd, preferred_element_type=F32)
            + jnp.dot(lo, ones_bd, preferred_element_type=F32))


def _rwkv_pre_kernel(*refs, seq):
    if seq:
        c_ref, prev_ref = refs[:2]
    else:
        c_ref, sh_ref = refs[:2]
    (mu_ref, w0_ref, w2_ref, a0_ref, a2_ref, g2_ref, kk_ref, ka_ref, rk_ref, bd_ref) = refs[2:12]
    (r_o, w_o, k_o, v_o, a_o, b_o, g_o, rkv_o) = refs[12:20]
    if seq:
        last_o, carry_ref = refs[20:22]
        cols = c_ref[0]
        tm = cols.shape[0]

        @pl.when(pl.program_id(1) == 0)
        def _():
            carry_ref[...] = prev_ref[0]

        rolled = pltpu.roll(cols, 1, 0)
        first = lax.broadcasted_iota(jnp.int32, cols.shape, 0) == 0
        shifted = jnp.where(first, carry_ref[...], rolled)
        carry_ref[...] = cols[tm - 1:tm, :]
        last_o[0] = cols[tm - 1:tm, :]
    else:
        cols = c_ref[...]
        shifted = sh_ref[...]
    xs = cols + (shifted - cols) * mu_ref[...]
    r = xs[:, 0:D_B]
    k = xs[:, D_B:2 * D_B]
    v = xs[:, 2 * D_B:3 * D_B]
    o = 3 * D_B
    wl = xs[:, o:o + LW_PAD]
    al = xs[:, o + LW_PAD:o + LW_PAD + LA_PAD]
    gl = xs[:, o + LW_PAD + LA_PAD:o + LW_PAD + LA_PAD + LG_PAD]
    z = -(w0_ref[...] + jnp.dot(jnp.tanh(wl).astype(BF16), w2_ref[...], preferred_element_type=F32))
    softplus = jnp.maximum(z, 0.0) + jnp.log1p(jnp.exp(-jnp.abs(z)))
    log_decay = -jnp.exp(-softplus - 0.5)
    decay = log_decay if seq else jnp.exp(log_decay)
    a = jax.nn.sigmoid(a0_ref[...] + jnp.dot(al.astype(BF16), a2_ref[...], preferred_element_type=F32))
    g = jnp.dot(jax.nn.sigmoid(gl).astype(BF16), g2_ref[...], preferred_element_type=F32)
    bd = bd_ref[...]
    kk = k * kk_ref[...]
    kk = kk / jnp.maximum(jnp.sqrt(_seg_sum(kk * kk, bd)), 1e-12)
    k2 = k * (1.0 + (a - 1.0) * ka_ref[...])
    if seq:
        r_o[0], w_o[0], k_o[0], v_o[0], a_o[0], b_o[0], g_o[0] = r, decay, k2, v, -kk, kk * a, g
        rkv_o[0] = _seg_sum(r * k2 * rk_ref[...], bd) * v
    else:
        r_o[...], w_o[...], k_o[...], v_o[...], a_o[...], b_o[...], g_o[...] = r, decay, k2, v, -kk, kk * a, g
        rkv_o[...] = _seg_sum(r * k2 * rk_ref[...], bd) * v


def _rwkv_pre(cols, shift_src, params, *, seq, tm):
    prm_specs = []
    for p in params:
        prm_specs.append(pl.BlockSpec(p.shape, (lambda b, i: (0, 0)) if seq else (lambda i: (0, 0))))
    if seq:
        nb, t, _ = cols.shape
        grid = (nb, t // tm)
        in_specs = [pl.BlockSpec((1, tm, N_B_PAD), lambda b, i: (b, i, 0)),
                    pl.BlockSpec((1, 1, N_B_PAD), lambda b, i: (b, 0, 0))] + prm_specs
        ospec = pl.BlockSpec((1, tm, D_B), lambda b, i: (b, i, 0))
        oshape = jax.ShapeDtypeStruct((nb, t, D_B), F32)
        out_specs = [ospec] * 8 + [pl.BlockSpec((1, 1, N_B_PAD), lambda b, i: (b, 0, 0))]
        out_shape = [oshape] * 8 + [jax.ShapeDtypeStruct((nb, 1, N_B_PAD), F32)]
        scratch = [pltpu.VMEM((1, N_B_PAD), F32)]
        sem = ("parallel", "arbitrary")
    else:
        m = cols.shape[0]
        grid = (m // tm,)
        in_specs = [pl.BlockSpec((tm, N_B_PAD), lambda i: (i, 0))] * 2 + prm_specs
        out_specs = [pl.BlockSpec((tm, D_B), lambda i: (i, 0))] * 8
        out_shape = [jax.ShapeDtypeStruct((m, D_B), F32)] * 8
        scratch = []
        sem = ("parallel",)
    return pl.pallas_call(
        functools.partial(_rwkv_pre_kernel, seq=seq),
        grid=grid, in_specs=in_specs, out_specs=out_specs, out_shape=out_shape,
        scratch_shapes=scratch, compiler_params=_cparams(*sem),
        name="rwkv_pre_seq" if seq else "rwkv_pre_rows",
    )(cols, shift_src, *params)


def _scan_pair_step(S, a, w, b, k, v, r):
    lane = lax.broadcasted_iota(jnp.int32, (N_B, LANES), 1)
    row = lax.broadcasted_iota(jnp.int32, (N_B, LANES), 0)
    lo = lane < N_B
    e1 = lane == row
    e2 = lane == row + N_B

    def half_sums(p):
        t1 = jnp.sum(jnp.where(lo, p, 0.0), axis=1, keepdims=True)
        t2 = jnp.sum(jnp.where(lo, 0.0, p), axis=1, keepdims=True)
        return t1, t2

    t1, t2 = half_sums(S * a)
    sa = jnp.where(lo, t1, t2)
    vb = jnp.where(lo, jnp.sum(jnp.where(e1, v, 0.0), axis=1, keepdims=True),
                   jnp.sum(jnp.where(e2, v, 0.0), axis=1, keepdims=True))
    S = S * w + sa * b + vb * k
    y1, y2 = half_sums(S * r)
    y = jnp.sum(jnp.where(e1, y1, 0.0) + jnp.where(e2, y2, 0.0), axis=0, keepdims=True)
    return S, y


def _split(x):
    hi = x.astype(BF16)
    return hi, (x - hi.astype(F32)).astype(BF16)


def _mm3(xs, ys, nt=False):
    lhs = jnp.concatenate([xs[0], xs[0], xs[1]], axis=1)
    if nt:
        rhs = jnp.concatenate([ys[0], ys[1], ys[0]], axis=1)
        return lax.dot_general(lhs, rhs, (((1,), (1,)), ((), ())), preferred_element_type=F32)
    rhs = jnp.concatenate([ys[0], ys[1], ys[0]], axis=0)
    return jnp.dot(lhs, rhs, preferred_element_type=F32)


def _mm1(x, y, nt=False):
    if nt:
        return lax.dot_general(x, y, (((1,), (1,)), ((), ())), preferred_element_type=F32)
    return jnp.dot(x, y, preferred_element_type=F32)


def _block_diag(x):
    lo = lax.broadcasted_iota(jnp.int32, x.shape, 1) < N_B
    zero = jnp.zeros_like(x)
    return jnp.concatenate([jnp.where(lo, x, zero), jnp.where(lo, zero, x)], axis=0)


def _each(fn, *lists):
    return [fn(*args) for args in zip(*lists)]


def _rwkv_chunk_pairs(zs, lws, a_s, bs, ks, vs, rs):
    c = lws[0].shape[0]
    tri = (lax.broadcasted_iota(jnp.int32, (c, c), 0) >= lax.broadcasted_iota(jnp.int32, (c, c), 1)).astype(BF16)
    tri3 = jnp.concatenate([tri, tri, tri], axis=1)

    def cumsum(lw):
        l1 = lw.astype(BF16)
        r1 = lw - l1.astype(F32)
        l2 = r1.astype(BF16)
        l3 = (r1 - l2.astype(F32)).astype(BF16)
        return jnp.dot(tri3, jnp.concatenate([l1, l2, l3], axis=0), preferred_element_type=F32)

    lam = _each(cumsum, lws)
    lam_c = [x[c - 1:c, :] for x in lam]
    w_inv = [jnp.exp(-x) for x in lam]
    w_rem = _each(lambda lc, x: jnp.exp(lc - x), lam_c, lam)
    r_h = _each(lambda r, x: _block_diag(r * jnp.exp(x)), rs, lam)
    s_a = _each(lambda a, x, lw: _split(_block_diag(a * jnp.exp(x - lw))), a_s, lam, lws)
    h_r = [x.astype(BF16) for x in r_h]
    s_b = _each(lambda b, wi: _split(_block_diag(b * wi)), bs, w_inv)
    s_k = _each(lambda k, wi: _split(_block_diag(k * wi)), ks, w_inv)
    s_v = [_split(_block_diag(v)) for v in vs]
    n = 2 * c
    ri = lax.broadcasted_iota(jnp.int32, (n, n), 0)
    ci = lax.broadcasted_iota(jnp.int32, (n, n), 1)
    strict, incl, eye = ci < ri, ci <= ri, ci == ri
    gram = lambda x, y: jnp.where(strict, _mm3(x, y, nt=True), 0.0)
    lab = _each(gram, s_a, s_b)
    lak = _each(gram, s_a, s_k)
    h_lrb = _each(lambda x, y: jnp.where(incl, _mm1(x, y[0], nt=True), 0.0).astype(BF16), h_r, s_b)
    h_lrk = _each(lambda x, y: jnp.where(incl, _mm1(x, y[0], nt=True), 0.0).astype(BF16), h_r, s_k)
    t = [jnp.where(eye, 1.0, 0.0) + x for x in lab]
    s_p = _each(_split, lab)
    for _ in range(int(math.log2(c)) - 1):
        s_p = _each(lambda p: _split(_mm3(p, p)), s_p)
        t = _each(lambda tt, p: tt + _mm3(_split(tt), p), t, s_p)
    s_t = _each(_split, t)
    s_at = _each(lambda tt, a: _split(_mm3(tt, a)), s_t, s_a)
    x2 = _each(lambda l, v: _split(_mm3(_split(l), v)), lak, s_v)
    s_u0 = _each(lambda tt, x: _split(_mm3(tt, x)), s_t, x2)
    rt = _each(lambda rh, l, at: rh + _mm1(l, at[0]), r_h, h_lrb, s_at)
    y0 = _each(lambda l, u, lk, v: _mm1(l, u[0]) + _mm1(lk, v[0]), h_lrb, s_u0, h_lrk, s_v)
    s_bt = _each(lambda b, wr: _split(_block_diag(b * wr).T), bs, w_rem)
    s_kt = _each(lambda k, wr: _split(_block_diag(k * wr).T), ks, w_rem)
    m = _each(lambda lc, bt, at: jnp.where(eye, jnp.exp(lc), 0.0) + _mm3(bt, at), lam_c, s_bt, s_at)
    nn = _each(lambda bt, u, kt, v: _mm3(bt, u) + _mm3(kt, v), s_bt, s_u0, s_kt, s_v)
    s_z = _each(_split, zs)
    ybd = _each(lambda r, z, y: _mm1(r.astype(BF16), z[0]) + y, rt, s_z, y0)
    z_new = _each(lambda mm, z, x: _mm3(_split(mm), z) + x, m, s_z, nn)
    return z_new, [y[:c] + y[c:] for y in ybd]


def _rwkv_scan_seq_kernel(r_ref, w_ref, k_ref, v_ref, a_ref, b_ref, y_ref, sf_ref, z_ref, *, nb):
    npair = H_B // 2
    ci = pl.program_id(0)

    @pl.when(ci == 0)
    def _():
        z_ref[...] = jnp.zeros_like(z_ref)

    sls = [(bi, slice(None), slice(p * LANES, (p + 1) * LANES)) for bi in range(nb) for p in range(npair)]
    rows = lambda ref: [ref[sl] for sl in sls]
    zs, ys = _rwkv_chunk_pairs([z_ref[c] for c in range(len(sls))], rows(w_ref), rows(a_ref),
                               rows(b_ref), rows(k_ref), rows(v_ref), rows(r_ref))
    for c, sl in enumerate(sls):
        z_ref[c] = zs[c]
        y_ref[sl] = ys[c]

    @pl.when(ci == pl.num_programs(0) - 1)
    def _():
        for c, (bi, _, _) in enumerate(sls):
            st = z_ref[c].T
            sf_ref[bi, 2 * (c % npair)] = st[:N_B, :N_B]
            sf_ref[bi, 2 * (c % npair) + 1] = st[N_B:, N_B:]


def _rwkv_scan_seq(r, lw, k, v, a, b, *, tc):
    nb, t, _ = r.shape
    spec = pl.BlockSpec((nb, tc, D_B), lambda c: (0, c, 0))
    return pl.pallas_call(
        functools.partial(_rwkv_scan_seq_kernel, nb=nb),
        grid=(t // tc,),
        in_specs=[spec] * 6,
        out_specs=[spec, pl.BlockSpec((nb, H_B, N_B, N_B), lambda c: (0, 0, 0, 0))],
        out_shape=[jax.ShapeDtypeStruct((nb, t, D_B), F32), jax.ShapeDtypeStruct((nb, H_B, N_B, N_B), F32)],
        scratch_shapes=[pltpu.VMEM((nb * H_B // 2, LANES, LANES), F32)],
        compiler_params=_cparams("arbitrary"),
        name="rwkv_scan_seq",
    )(r, lw, k, v, a, b)


def _rwkv_scan_rows_kernel(r_ref, w_ref, k_ref, v_ref, a_ref, b_ref, s0_ref, y_ref, sf_ref, *, tb):
    def group(gi, carry):
        base = pl.multiple_of(gi * SUBLANES, SUBLANES)
        for p in range(H_B // 2):
            sl = (pl.ds(base, SUBLANES), slice(p * LANES, (p + 1) * LANES))
            a, w, b, k, v, r = (ref[sl] for ref in (a_ref, w_ref, b_ref, k_ref, v_ref, r_ref))
            ys = []
            for j in range(SUBLANES):
                row = slice(j, j + 1)
                S0 = jnp.concatenate([s0_ref[base + j, 2 * p], s0_ref[base + j, 2 * p + 1]], axis=1)
                S, y = _scan_pair_step(S0, a[row], w[row], b[row], k[row], v[row], r[row])
                sf_ref[base + j, 2 * p] = S[:, :N_B]
                sf_ref[base + j, 2 * p + 1] = S[:, N_B:]
                ys.append(y)
            y_ref[sl] = jnp.concatenate(ys, axis=0)
        return carry

    lax.fori_loop(0, tb // SUBLANES, group, 0)


def _rwkv_scan_rows(r, w, k, v, a, b, s0, *, tb):
    m = r.shape[0]
    spec = pl.BlockSpec((tb, D_B), lambda i: (i, 0))
    sspec = pl.BlockSpec((tb, H_B, N_B, N_B), lambda i: (i, 0, 0, 0))
    return pl.pallas_call(
        functools.partial(_rwkv_scan_rows_kernel, tb=tb),
        grid=(m // tb,),
        in_specs=[spec] * 6 + [sspec],
        out_specs=[spec, sspec],
        out_shape=[jax.ShapeDtypeStruct((m, D_B), F32), jax.ShapeDtypeStruct((m, H_B, N_B, N_B), F32)],
        compiler_params=_cparams("parallel"),
        name="rwkv_scan_rows",
    )(r, w, k, v, a, b, s0)


def _rwkv_post_kernel(y_ref, rkv_ref, g_ref, lg_ref, lb_ref, bd_ref, o_ref):
    y = y_ref[...]
    bd = bd_ref[...]
    d = y - _seg_sum(y, bd) * (1.0 / N_B)
    var = _seg_sum(d * d, bd) * (1.0 / N_B)
    yn = d * lax.rsqrt(var + GN_EPS_B) * lg_ref[...] + lb_ref[...]
    o_ref[...] = ((yn + rkv_ref[...]) * g_ref[...]).astype(o_ref.dtype)


def _rwkv_post(y, rkv, g, lg, lb, bd, *, tm):
    m = y.shape[0]
    spec = pl.BlockSpec((tm, D_B), lambda i: (i, 0))
    row = pl.BlockSpec((1, D_B), lambda i: (0, 0))
    return pl.pallas_call(
        _rwkv_post_kernel,
        grid=(m // tm,),
        in_specs=[spec, spec, spec, row, row, pl.BlockSpec((D_B, D_B), lambda i: (0, 0))],
        out_specs=spec,
        out_shape=jax.ShapeDtypeStruct((m, D_B), BF16),
        compiler_params=_cparams("parallel"),
        name="rwkv_post",
    )(y, rkv, g, lg, lb, bd)


def _lam_full(lq1, lk1, lq2, lk2, lam_init):
    return (jnp.exp(jnp.sum(lq1 * lk1, axis=-1, keepdims=True))
            - jnp.exp(jnp.sum(lq2 * lk2, axis=-1, keepdims=True)) + lam_init)


ONES_ROWS = 16


def _attn_prompt_kernel(q_ref, k_ref, vt_ref, lq1, lk1, lq2, lk2, sg_ref, o_ref, m1, a1, m2, a2, *, tk, lam_init):
    qi = pl.program_id(2)
    for m_ref, a_ref in ((m1, a1), (m2, a2)):
        m_ref[...] = jnp.full_like(m_ref, -jnp.inf)
        a_ref[...] = jnp.zeros_like(a_ref)

    q = q_ref[0]
    lane = lax.broadcasted_iota(jnp.int32, q.shape, 1)
    zero = jnp.zeros_like(q)
    q_sub = (jnp.where(lane < D_HEAD, q, zero), jnp.where(lane < D_HEAD, zero, q))
    ones = jnp.ones((ONES_ROWS, tk), BF16)

    def update(kis, masked):
        ks = [k_ref[0, pl.ds(pl.multiple_of(ki * tk, tk), tk), :] for ki in kis]
        vts = [jnp.concatenate([vt_ref[0, ki], ones], axis=0) for ki in kis]
        units = [(qs[c0:c0 + ATTN_STRIP], m_ref, a_ref, c0)
                 for qs, m_ref, a_ref in ((q_sub[0], m1, a1), (q_sub[1], m2, a2))
                 for c0 in range(0, q.shape[0], ATTN_STRIP)]
        sts = [[lax.dot_general(k, qs, (((1,), (1,)), ((), ())), preferred_element_type=F32)
                for k in ks] for qs, _, _, _ in units]
        for st_blocks, (_, m_ref, a_ref, c0) in zip(sts, units):
            cols = slice(c0, c0 + ATTN_STRIP)
            if masked:
                kr = lax.broadcasted_iota(jnp.int32, st_blocks[0].shape, 0)
                qc = lax.broadcasted_iota(jnp.int32, st_blocks[0].shape, 1) + c0
                st_blocks = [jnp.where(kr <= qc, st, -jnp.inf) for st in st_blocks]
            m_old = m_ref[:, cols]
            m_new = m_old
            for st in st_blocks:
                m_new = jnp.maximum(m_new, jnp.max(st, axis=0, keepdims=True))
            acc = jnp.exp2(m_old - m_new) * a_ref[:, cols]
            for st, vt1 in zip(st_blocks, vts):
                acc = acc + jnp.dot(vt1, jnp.exp2(st - m_new).astype(BF16), preferred_element_type=F32)
            a_ref[:, cols] = acc
            m_ref[:, cols] = m_new

    def group_body(j, carry):
        update([ATTN_GROUP * j + i for i in range(ATTN_GROUP)], False)
        return carry

    def single_body(ki, carry):
        update([ki], False)
        return carry

    n_groups = lax.shift_right_logical(qi, ATTN_GROUP.bit_length() - 1)
    lax.fori_loop(0, n_groups, group_body, 0)
    lax.fori_loop(n_groups * ATTN_GROUP, qi, single_body, 0)
    update([qi], True)
    lam = _lam_full(lq1[...], lk1[...], lq2[...], lk2[...], lam_init)
    d = 2 * D_HEAD
    ot = a1[:d, :] / a1[d:d + 1, :] - lam * (a2[:d, :] / a2[d:d + 1, :])
    on = ot * lax.rsqrt(jnp.mean(ot * ot, axis=0, keepdims=True) + SUBLN_EPS)
    o_ref[0] = (on.T * sg_ref[...] * (1.0 - lam_init)).astype(o_ref.dtype)


def _attn_prompt(q, k, vt, lams, subln_g, lam_init, *, tq):
    nb, t, _ = q.shape
    tk = vt.shape[3]
    assert tk == tq
    qspec = pl.BlockSpec((1, tq, LANES), lambda b, h, i: (b, i, h))
    kspec = pl.BlockSpec((1, t, LANES), lambda b, h, i: (b, 0, h))
    vspec = pl.BlockSpec((1, t // tk, LANES, tk), lambda b, h, i: (b, 0, h, 0))
    row = lambda n: pl.BlockSpec((1, n), lambda b, h, i: (0, 0))
    acc = 2 * D_HEAD + ONES_ROWS
    return pl.pallas_call(
        functools.partial(_attn_prompt_kernel, tk=tk, lam_init=lam_init),
        grid=(nb, H_C, t // tq),
        in_specs=[qspec, kspec, vspec, row(D_HEAD), row(D_HEAD), row(D_HEAD), row(D_HEAD), row(LANES)],
        out_specs=qspec,
        out_shape=jax.ShapeDtypeStruct(q.shape, BF16),
        scratch_shapes=[pltpu.VMEM((1, tq), F32), pltpu.VMEM((acc, tq), F32),
                        pltpu.VMEM((1, tq), F32), pltpu.VMEM((acc, tq), F32)],
        compiler_params=_cparams("parallel", "parallel", "arbitrary"),
        name="attn_prompt",
    )(q, k, vt, *lams, subln_g)


def _attn_sample_kernel(pt_ref, q_ref, kn_ref, vn_ref, *refs, npg, lam_init):
    k_refs = refs[:npg]
    v_refs = refs[npg:2 * npg]
    lq1, lk1, lq2, lk2, sg_ref, o_ref, m_ref, l_ref, acc_ref = refs[2 * npg:]
    g = pl.program_id(1)
    nrow, ncol = 2 * H_C, PAGE_SIZE * H_C
    q = q_ref[0]
    lo = lax.broadcasted_iota(jnp.int32, q.shape, 1) < D_HEAD
    qz = jnp.concatenate([jnp.where(lo, q, 0.0), jnp.where(lo, 0.0, q)], axis=0).astype(BF16)
    valid = ((lax.broadcasted_iota(jnp.int32, (nrow, ncol), 1) & (H_C - 1))
             == (lax.broadcasted_iota(jnp.int32, (nrow, ncol), 0) & (H_C - 1)))

    @pl.when(g == 0)
    def _():
        as_mxu = lambda ref: jnp.concatenate([ref[0], ref[0]], axis=0).astype(BF16).astype(F32)
        m_ref[...] = jnp.sum(qz.astype(F32) * as_mxu(kn_ref), axis=1, keepdims=True)
        l_ref[...] = jnp.ones_like(l_ref)
        acc_ref[...] = as_mxu(vn_ref)

    sts = []
    for r in range(npg):
        kmat = k_refs[r][...].reshape(ncol, LANES).astype(BF16)
        st = lax.dot_general(qz, kmat, (((1,), (1,)), ((), ())), preferred_element_type=F32)
        sts.append(jnp.where(valid, st, -jnp.inf))
    m_old = m_ref[...]
    m_new = m_old
    for st in sts:
        m_new = jnp.maximum(m_new, jnp.max(st, axis=1, keepdims=True))
    alpha = jnp.exp2(m_old - m_new)
    l = alpha * l_ref[...]
    acc = alpha * acc_ref[...]
    for st, v_ref in zip(sts, v_refs):
        p = jnp.exp2(st - m_new)
        l = l + jnp.sum(p, axis=1, keepdims=True)
        acc = acc + jnp.dot(p.astype(BF16), v_ref[...].reshape(ncol, LANES).astype(BF16),
                            preferred_element_type=F32)
    m_ref[...] = m_new
    l_ref[...] = l
    acc_ref[...] = acc

    @pl.when(g == pl.num_programs(1) - 1)
    def _():
        o = acc_ref[...] / l_ref[...]
        lam = _lam_full(lq1[...], lk1[...], lq2[...], lk2[...], lam_init)
        o_ref[0] = (_rms(o[:H_C] - lam * o[H_C:], sg_ref[...], SUBLN_EPS) * (1.0 - lam_init)).astype(o_ref.dtype)


def _attn_sample(q, k_new, v_new, cache_k, cache_v, page_table, layer, lams, subln_g, lam_init, *, npg):
    nb = q.shape[0]
    n_pages = page_table.shape[1]
    hspec = pl.BlockSpec((1, H_C, LANES), lambda b, g, pt: (b, 0, 0))
    row = lambda n: pl.BlockSpec((1, n), lambda b, g, pt: (0, 0))

    def page_spec(r):
        return pl.BlockSpec((None, None, PAGE_SIZE, H_C, LANES),
                            lambda b, g, pt: (layer, pt[b, g * npg + r], 0, 0, 0))

    grid_spec = pltpu.PrefetchScalarGridSpec(
        num_scalar_prefetch=1,
        grid=(nb, n_pages // npg),
        in_specs=[hspec, hspec, hspec] + [page_spec(r) for r in range(npg)] * 2
                 + [row(D_HEAD), row(D_HEAD), row(D_HEAD), row(D_HEAD), row(LANES)],
        out_specs=hspec,
        scratch_shapes=[pltpu.VMEM((2 * H_C, 1), F32), pltpu.VMEM((2 * H_C, 1), F32),
                        pltpu.VMEM((2 * H_C, LANES), F32)],
    )
    return pl.pallas_call(
        functools.partial(_attn_sample_kernel, npg=npg, lam_init=lam_init),
        grid_spec=grid_spec,
        out_shape=jax.ShapeDtypeStruct((nb, H_C, LANES), BF16),
        compiler_params=_cparams("parallel", "arbitrary"),
        name="attn_sample",
    )(page_table, q, k_new, v_new, *([cache_k] * npg), *([cache_v] * npg), *lams, subln_g)


def _merge_kernel(x_ref, oa_ref, ob_ref, oc_ref, cg_ref, wa_ref, wb_ref, wc_ref, wo_ref, xo_ref):
    gates = cg_ref[...].astype(F32)
    dot = lambda a, w: jnp.dot(a, w, preferred_element_type=F32)
    m = (gates[:, 0:D_MODEL] * dot(oa_ref[...], wa_ref[...])
         + gates[:, D_MODEL:2 * D_MODEL] * dot(ob_ref[...], wb_ref[...])
         + gates[:, 2 * D_MODEL:3 * D_MODEL] * dot(oc_ref[...], wc_ref[...]))
    xo_ref[...] = x_ref[...] + dot(m.astype(BF16), wo_ref[...])


def _merge(x, oa, ob, oc, cg, wa, wb, wc, wo, *, tm):
    m = x.shape[0]
    rows = lambda n: pl.BlockSpec((tm, n), lambda i: (i, 0))
    full = lambda a: pl.BlockSpec(a.shape, lambda i: (0, 0))
    return pl.pallas_call(
        _merge_kernel,
        grid=(m // tm,),
        in_specs=[rows(D_MODEL), rows(D_A), rows(D_B), rows(D_C), rows(3 * D_MODEL),
                  full(wa), full(wb), full(wc), full(wo)],
        out_specs=rows(D_MODEL),
        out_shape=jax.ShapeDtypeStruct((m, D_MODEL), F32),
        compiler_params=_cparams("parallel"),
        name="merge",
    )(x, oa, ob, oc, cg, wa, wb, wc, wo)


def _ffn_kernel(x_ref, g_ref, wu_ref, wd_ref, gf_ref, o_ref, h_ref, acc_ref, *, final):
    j = pl.program_id(1)

    @pl.when(j == 0)
    def _():
        h_ref[...] = _rms(x_ref[...], g_ref[...], EPS).astype(BF16)
        acc_ref[...] = jnp.zeros_like(acc_ref)

    up = jnp.dot(h_ref[...], wu_ref[...], preferred_element_type=F32)
    act = jnp.square(jnp.maximum(up, 0.0)).astype(BF16)
    acc_ref[...] += jnp.dot(act, wd_ref[...], preferred_element_type=F32)

    @pl.when(j == pl.num_programs(1) - 1)
    def _():
        xn = x_ref[...] + acc_ref[...]
        o_ref[...] = _rms(xn, gf_ref[...], EPS) if final else xn


def _ffn(x, g, wu, wd, gf, *, final, tm, tf):
    m = x.shape[0]
    row = pl.BlockSpec((1, D_MODEL), lambda i, j: (0, 0))
    return pl.pallas_call(
        functools.partial(_ffn_kernel, final=final),
        grid=(m // tm, D_FF // tf),
        in_specs=[pl.BlockSpec((tm, D_MODEL), lambda i, j: (i, 0)), row,
                  pl.BlockSpec((D_MODEL, tf), lambda i, j: (0, j)),
                  pl.BlockSpec((tf, D_MODEL), lambda i, j: (j, 0)), row],
        out_specs=pl.BlockSpec((tm, D_MODEL), lambda i, j: (i, 0)),
        out_shape=jax.ShapeDtypeStruct((m, D_MODEL), F32),
        scratch_shapes=[pltpu.VMEM((tm, D_MODEL), BF16), pltpu.VMEM((tm, D_MODEL), F32)],
        compiler_params=_cparams("parallel", "arbitrary"),
        name="ffn",
    )(x, g, wu, wd, gf)


def _pad_b_cols(a):
    o = 3 * D_B
    z = lambda n: jnp.zeros(a.shape[:-1] + (n,), a.dtype)
    return jnp.concatenate([a[..., :o + LORA_W], z(LW_PAD - LORA_W),
                            a[..., o + LORA_W:o + LORA_W + LORA_A], z(LA_PAD - LORA_A),
                            a[..., o + LORA_W + LORA_A:], z(LG_PAD - LORA_G)], axis=-1)


def _unpad_b_cols(a):
    o = 3 * D_B
    return jnp.concatenate([a[..., :o + LORA_W], a[..., o + LW_PAD:o + LW_PAD + LORA_A],
                            a[..., o + LW_PAD + LA_PAD:o + LW_PAD + LA_PAD + LORA_G]], axis=-1)


def _pad_rows(a, n):
    return jnp.concatenate([a, jnp.zeros((n - a.shape[0],) + a.shape[1:], a.dtype)], axis=0)


def _ones_blockdiag(n, seg):
    i = np.arange(n) // seg
    return jnp.asarray((i[:, None] == i[None, :]).astype(np.float32), dtype=BF16)


def _layer_weights(l, p):
    w_in = p["w_in"][l]
    o_b = N_A_COLS
    o_q = o_b + N_B_COLS
    w = dict(
        norm_mix=p["norm_mix"][l].reshape(1, D_MODEL),
        w_a=w_in[:, :o_b].astype(BF16),
        w_b=_pad_b_cols(w_in[:, o_b:o_q]).astype(BF16),
        w_q=w_in[:, o_q:o_q + D_C].astype(BF16),
        w_k=w_in[:, o_q + D_C:o_q + 2 * D_C].astype(BF16),
        w_v=w_in[:, o_q + 2 * D_C:o_q + 3 * D_C].astype(BF16),
        w_g=w_in[:, o_q + 3 * D_C:].astype(BF16),
        lng=p["sgu_ln_g"][l].reshape(1, D_A), lnb=p["sgu_ln_b"][l].reshape(1, D_A),
        sgu_w=p["sgu_w"][l],
        sgu_bias_td=jnp.repeat(p["sgu_b"][l].T, D_A // G_A, axis=1),
        sgu_w0=jnp.repeat(p["sgu_w"][l][:, 0, 0], D_A // G_A).reshape(1, D_A),
        sgu_b0=jnp.repeat(p["sgu_b"][l][:, 0], D_A // G_A).reshape(1, D_A),
        rwkv=[_pad_b_cols(p["shift_mu"][l]).reshape(1, N_B_PAD),
              p["w0"][l].reshape(1, D_B), _pad_rows(p["w2"][l], LW_PAD).astype(BF16),
              p["a0"][l].reshape(1, D_B), _pad_rows(p["a2"][l], LA_PAD).astype(BF16),
              _pad_rows(p["g2"][l], LG_PAD).astype(BF16),
              p["k_k"][l].reshape(1, D_B), p["k_a"][l].reshape(1, D_B), p["r_k"][l].reshape(1, D_B),
              _ones_blockdiag(D_B, N_B)],
        lnx_g=p["lnx_g"][l].reshape(1, D_B), lnx_b=p["lnx_b"][l].reshape(1, D_B),
        lams=[p[n][l].reshape(1, D_HEAD) for n in ("lam_q1", "lam_k1", "lam_q2", "lam_k2")],
        subln_g=p["subln_g"][l].reshape(1, 2 * D_HEAD),
        w_br_a=p["w_br_a"][l].astype(BF16), w_br_b=p["w_br_b"][l].astype(BF16),
        w_br_c=p["w_br_c"][l].astype(BF16), w_out=p["w_out"][l].astype(BF16),
        norm_ffn=p["norm_ffn"][l].reshape(1, D_MODEL),
        w_up=p["w_up"][l].astype(BF16), w_down=p["w_down"][l].astype(BF16),
        norm_final=p["norm_final"].reshape(1, D_MODEL),
    )
    return w


def _project_all(x, w, rope_tabs, tm, t_seq=None, kv_stack=None):
    pj = functools.partial(_proj, x, w["norm_mix"], tm=tm, tn=D_MODEL)
    qscale = math.log2(math.e) / math.sqrt(D_HEAD)
    (ca,) = pj(w["w_a"], out_dtypes=(F32,), name="proj_a")
    (cb,) = pj(w["w_b"], out_dtypes=(F32,), name="proj_b")
    (cg,) = pj(w["w_g"], out_dtypes=(BF16,), gate=True, name="proj_g")
    if t_seq is None:
        (q,) = pj(w["w_q"], out_dtypes=(F32,), rope_tabs=rope_tabs, scale=qscale, name="proj_q")
        (kf,) = pj(w["w_k"], out_dtypes=(F32,), rope_tabs=rope_tabs, name="proj_k")
        (vf,) = pj(w["w_v"], out_dtypes=(F32,), name="proj_v")
        return ca, cb, cg, q, kf, vf
    (q,) = pj(w["w_q"], out_dtypes=(BF16,), rope_tabs=rope_tabs, scale=qscale, name="proj_q")
    layer, k_buf, v_buf = kv_stack
    kf, kb = pj(w["w_k"], out_dtypes=(F32, BF16), rope_tabs=rope_tabs, stack=(layer, k_buf), name="proj_k")
    vf, vt = _proj(x, w["norm_mix"], w["w_v"], tm=ATTN_BLOCK, tn=D_MODEL, out_dtypes=(F32, BF16), t_seq=t_seq,
                   stack=(layer, v_buf), name="proj_v")
    return ca, cb, cg, q, kf, vf, kb, vt


def _prompt_layer(x, l, w, rope_tabs, nb, t, final, kv_stack):
    lam_init = 0.8 - 0.6 * math.exp(-0.3 * l)
    ca, cb, cg, qb, kf, vf, kb, vt = _project_all(x, w, rope_tabs, 1024, t_seq=t, kv_stack=kv_stack)
    o_a = _sgu_prompt(ca, w["lng"], w["lnb"], w["sgu_w"], w["sgu_bias_td"], tm=512)
    prev0 = jnp.zeros((nb, 1, N_B_PAD), F32)
    *rw, last = _rwkv_pre(cb.reshape(nb, t, N_B_PAD), prev0, w["rwkv"], seq=True, tm=256)
    r, dec, k2, v, na, bb, g, rkv = rw
    y, s_fin = _rwkv_scan_seq(r, dec, k2, v, na, bb, tc=64)
    flat = lambda a: a.reshape(nb * t, D_B)
    o_b = _rwkv_post(flat(y), flat(rkv), flat(g), w["lnx_g"], w["lnx_b"], w["rwkv"][-1], tm=1024)
    b3 = lambda a: a.reshape(nb, t, D_C)
    o_c = _attn_prompt(b3(qb), b3(kb), vt, w["lams"], w["subln_g"], lam_init, tq=ATTN_BLOCK)
    x = _merge(x, o_a, o_b, o_c.reshape(nb * t, D_C), cg, w["w_br_a"], w["w_br_b"], w["w_br_c"], w["w_out"], tm=256)
    x = _ffn(x, w["norm_ffn"], w["w_up"], w["w_down"], w["norm_final"], final=final, tm=1024, tf=1024)
    return x, kf, vf, s_fin, _unpad_b_cols(last[:, 0, :])


def _sample_layer(x, l, w, rope_tabs, cache_k, cache_v, page_table, prev_shift, s0, final):
    nb = x.shape[0]
    lam_init = 0.8 - 0.6 * math.exp(-0.3 * l)
    ca, cb, cg, qb, kf, vf = _project_all(x, w, rope_tabs, nb)
    o_a, va = _sgu_sample(ca, w["lng"], w["lnb"], w["sgu_w0"], w["sgu_b0"])
    r, dec, k2, v, na, bb, g, rkv = _rwkv_pre(cb, _pad_b_cols(prev_shift), w["rwkv"], seq=False, tm=nb)
    y, s_new = _rwkv_scan_rows(r, dec, k2, v, na, bb, s0, tb=32)
    o_b = _rwkv_post(y, rkv, g, w["lnx_g"], w["lnx_b"], w["rwkv"][-1], tm=nb)
    h3 = lambda a: a.reshape(nb, H_C, LANES)
    o_c = _attn_sample(h3(qb), h3(kf), h3(vf), cache_k, cache_v, page_table, l,
                       w["lams"], w["subln_g"], lam_init, npg=8)
    x = _merge(x, o_a, o_b, o_c.reshape(nb, D_C), cg, w["w_br_a"], w["w_br_b"], w["w_br_c"], w["w_out"], tm=nb)
    x = _ffn(x, w["norm_ffn"], w["w_up"], w["w_down"], w["norm_final"], final=final, tm=nb, tf=1024)
    return x, kf, vf, s_new, _unpad_b_cols(cb), va


def kernel(x_prompt, x_sample, cache_k, cache_v, state_rwkv, state_shift, page_table, norm_mix, w_in, sgu_ln_g, sgu_ln_b, sgu_w, sgu_b, shift_mu, w0, w2, a0, a2, g2, k_k, k_a, r_k, lnx_g, lnx_b, lam_q1, lam_k1, lam_q2, lam_k2, subln_g, w_br_a, w_br_b, w_br_c, w_out, norm_ffn, w_up, w_down, norm_final):
    p = dict(norm_mix=norm_mix, w_in=w_in, sgu_ln_g=sgu_ln_g, sgu_ln_b=sgu_ln_b, sgu_w=sgu_w, sgu_b=sgu_b,
             shift_mu=shift_mu, w0=w0, w2=w2, a0=a0, a2=a2, g2=g2, k_k=k_k, k_a=k_a, r_k=r_k, lnx_g=lnx_g,
             lnx_b=lnx_b, lam_q1=lam_q1, lam_k1=lam_k1, lam_q2=lam_q2, lam_k2=lam_k2, subln_g=subln_g,
             w_br_a=w_br_a, w_br_b=w_br_b, w_br_c=w_br_c, w_out=w_out, norm_ffn=norm_ffn, w_up=w_up,
             w_down=w_down, norm_final=norm_final)
    depth = w_in.shape[0]
    bp, tp, _ = x_prompt.shape
    bs, ts, _ = x_sample.shape
    tabs_p = _rope_tables(jnp.arange(tp))
    tabs_s = _rope_tables(jnp.full((bs * ts,), PAST_LEN, jnp.int32))
    xp = x_prompt.reshape(bp * tp, D_MODEL)
    xs = x_sample.reshape(bs * ts, D_MODEL)
    outs_p, outs_s = [], []
    kp = jnp.zeros((depth * bp * tp, D_C), F32)
    vp = jnp.zeros((depth * bp * tp, D_C), F32)
    for l in range(depth):
        w = _layer_weights(l, p)
        final = l == depth - 1
        xp, kp, vp, sp, shp = _prompt_layer(xp, l, w, tabs_p, bp, tp, final, (l, kp, vp))
        xs, ks, vs, ss, shs, va = _sample_layer(xs, l, w, tabs_s, cache_k, cache_v, page_table,
                                                state_shift[l], state_rwkv[l], final)
        outs_p.append((sp, shp))
        outs_s.append((ks.reshape(bs, ts, H_C, 2 * D_HEAD), vs.reshape(bs, ts, H_C, 2 * D_HEAD), ss, shs,
                       va.reshape(bs, ts, D_A)))
    stack = lambda outs, i: jnp.stack([o[i] for o in outs])
    return (xp.reshape(bp, tp, D_MODEL), xs.reshape(bs, ts, D_MODEL),
            kp.reshape(depth, bp, tp, H_C, 2 * D_HEAD), vp.reshape(depth, bp, tp, H_C, 2 * D_HEAD),
            stack(outs_p, 0), stack(outs_p, 1),
            stack(outs_s, 0), stack(outs_s, 1), stack(outs_s, 2), stack(outs_s, 3), stack(outs_s, 4))
```

```python
import functools
import math

import numpy as np
import jax
import jax.numpy as jnp
from jax import lax
from jax.experimental import pallas as pl
from jax.experimental.pallas import tpu as pltpu

F32 = jnp.float32
BF16 = jnp.bfloat16

D_MODEL = 1024
PAST_LEN = 2048
PAGE_SIZE = 128
CHUNK = 128
D_A = 512
G_A = 8
H_B = 8
N_B = 64
D_B = H_B * N_B
LORA_W = 64
LORA_A = 64
LORA_G = 160
GN_EPS_B = 64e-5
H_C = 8
D_HEAD = 64
D_C = H_C * 2 * D_HEAD
ROPE_DIM = D_HEAD // 4
ROPE_THETA = 500000.0
SUBLN_EPS = 1e-5
D_FF = 4 * D_MODEL
EPS = 1e-6
N_A_COLS = 2 * D_A
N_B_COLS = 3 * D_B + LORA_W + LORA_A + LORA_G
LANES = 128
SUBLANES = 8
LW_PAD = 128
LA_PAD = 128
LG_PAD = 256
N_B_PAD = 3 * D_B + LW_PAD + LA_PAD + LG_PAD
VMEM_LIMIT = 52 * 1024 * 1024
ATTN_BLOCK = 512
ATTN_GROUP = 2
ATTN_STRIP = 256


def _cparams(*sem):
    return pltpu.CompilerParams(dimension_semantics=sem, vmem_limit_bytes=VMEM_LIMIT)


def _rms(x, g, eps):
    return x * lax.rsqrt(jnp.mean(x * x, axis=-1, keepdims=True) + eps) * g


def _proj_kernel(*refs, rope, scale, n_out, aliased, gate):
    x_ref, g_ref, w_ref = refs[:3]
    pos = 3
    if rope:
        c_ref, s1_ref, s2_ref = refs[3:6]
        pos = 6
    pos += aliased
    out_refs = refs[pos:pos + n_out]
    h_ref = refs[pos + n_out]

    @pl.when(pl.program_id(1) == 0)
    def _():
        h_ref[...] = _rms(x_ref[...], g_ref[...], EPS).astype(BF16)

    acc = jnp.dot(h_ref[...], w_ref[...], preferred_element_type=F32)
    if rope:
        c, s1, s2 = c_ref[...], s1_ref[...], s2_ref[...]
        parts = []
        for hh in range(acc.shape[1] // LANES):
            a = acc[:, hh * LANES:(hh + 1) * LANES]
            parts.append(a * c + pltpu.roll(a, ROPE_DIM // 2, 1) * s1
                         + pltpu.roll(a, LANES - ROPE_DIM // 2, 1) * s2)
        acc = jnp.concatenate(parts, axis=1)
    if scale != 1.0:
        acc = acc * scale
    if gate:
        acc = jax.nn.sigmoid(acc)
    for o_ref in out_refs:
        if len(o_ref.shape) == 4:
            o_ref[0, 0] = acc.T.astype(o_ref.dtype)
        else:
            o_ref[...] = acc.astype(o_ref.dtype)


def _proj(x, g, w, *, tm, tn, out_dtypes, rope_tabs=None, scale=1.0, t_seq=None, stack=None, gate=False,
          name="proj"):
    m, d = x.shape
    n = w.shape[1]
    in_specs = [pl.BlockSpec((tm, d), lambda i, j: (i, 0)),
                pl.BlockSpec((1, d), lambda i, j: (0, 0)),
                pl.BlockSpec((d, tn), lambda i, j: (0, j))]
    args = [x, g, w]
    if rope_tabs is not None:
        nt = rope_tabs[0].shape[0] // tm
        in_specs += [pl.BlockSpec((tm, LANES), lambda i, j: (i % nt, 0))] * 3
        args += list(rope_tabs)
    out_specs = [pl.BlockSpec((tm, tn), lambda i, j: (i, j)) for _ in out_dtypes]
    out_shape = [jax.ShapeDtypeStruct((m, n), dt) for dt in out_dtypes]
    if t_seq is not None:
        nts = t_seq // tm
        out_specs[-1] = pl.BlockSpec((1, 1, tn, tm), lambda i, j: (i // nts, i % nts, j, 0))
        out_shape[-1] = jax.ShapeDtypeStruct((m // t_seq, nts, n, tm), out_dtypes[-1])
    aliases = {}
    if stack is not None:
        layer, buf = stack
        off = layer * (m // tm)
        out_specs[0] = pl.BlockSpec((tm, tn), lambda i, j: (i + off, j))
        out_shape[0] = jax.ShapeDtypeStruct(buf.shape, out_dtypes[0])
        aliases = {len(args): 0}
        in_specs.append(pl.BlockSpec(memory_space=pl.ANY))
        args.append(buf)
    outs = pl.pallas_call(
        functools.partial(_proj_kernel, rope=rope_tabs is not None, scale=scale, n_out=len(out_dtypes),
                          aliased=len(aliases), gate=gate),
        grid=(m // tm, n // tn),
        in_specs=in_specs,
        out_specs=out_specs,
        out_shape=out_shape,
        input_output_aliases=aliases,
        scratch_shapes=[pltpu.VMEM((tm, d), BF16)],
        compiler_params=_cparams("parallel", "arbitrary"),
        name=name,
    )(*args)
    return outs


def _rope_kernel(pos_ref, invf_ref, c_ref, s1_ref, s2_ref):
    ang = pos_ref[...] * invf_ref[...]
    lane = lax.broadcasted_iota(jnp.int32, ang.shape, 1) % D_HEAD
    first = lane < ROPE_DIM // 2
    second = (lane >= ROPE_DIM // 2) & (lane < ROPE_DIM)
    cos, sin = jnp.cos(ang), jnp.sin(ang)
    c_ref[...] = jnp.where(first | second, cos, 1.0)
    s1_ref[...] = jnp.where(second, sin, 0.0)
    s2_ref[...] = jnp.where(first, -sin, 0.0)


def _rope_tables(pos):
    t = pos.shape[0]
    half = ROPE_DIM // 2
    inv_freq = ROPE_THETA ** (-jnp.arange(half, dtype=F32) / half)
    blk = jnp.concatenate([inv_freq, inv_freq, jnp.zeros((D_HEAD - ROPE_DIM,), F32)])
    invf = jnp.concatenate([blk, blk]).reshape(1, LANES)
    tm = min(t, 512)
    return pl.pallas_call(
        _rope_kernel,
        grid=(t // tm,),
        in_specs=[pl.BlockSpec((tm, 1), lambda i: (i, 0)), pl.BlockSpec((1, LANES), lambda i: (0, 0))],
        out_specs=[pl.BlockSpec((tm, LANES), lambda i: (i, 0))] * 3,
        out_shape=[jax.ShapeDtypeStruct((t, LANES), F32)] * 3,
        compiler_params=_cparams("parallel"),
        name="rope_tables",
    )(pos.astype(F32).reshape(t, 1), invf)


def _gelu_ln(ca, lng, lnb):
    gx = 0.5 * ca * (1.0 + lax.erf(ca * math.sqrt(0.5)))
    u, v = gx[:, :D_A], gx[:, D_A:]
    d = v - jnp.mean(v, axis=-1, keepdims=True)
    va = d * lax.rsqrt(jnp.mean(d * d, axis=-1, keepdims=True) + EPS) * lng + lnb
    return u, va


def _sgu_prompt_kernel(ca_ref, lng_ref, lnb_ref, w_ref, b_ref, oa_ref, *, nchunk):
    u, va = _gelu_ln(ca_ref[...], lng_ref[...], lnb_ref[...])
    vab = va.astype(BF16)
    row = lax.broadcasted_iota(jnp.int32, (CHUNK, CHUNK), 0)
    col = lax.broadcasted_iota(jnp.int32, (CHUNK, CHUNK), 1)
    wcat = jnp.concatenate([jnp.where(col <= row, w_ref[g], 0.0).astype(BF16) for g in range(G_A)], axis=1)
    grp = lax.broadcasted_iota(jnp.int32, (CHUNK, D_A), 1) // (D_A // G_A)
    for ci in range(nchunk):
        vc = vab[ci * CHUNK:(ci + 1) * CHUNK]
        vbig = jnp.concatenate([jnp.where(grp == g, vc, jnp.zeros_like(vc)) for g in range(G_A)], axis=0)
        s = jnp.dot(wcat, vbig, preferred_element_type=F32) + b_ref[...]
        oa_ref[ci * CHUNK:(ci + 1) * CHUNK, :] = (u[ci * CHUNK:(ci + 1) * CHUNK] * s).astype(oa_ref.dtype)


def _sgu_prompt(ca, lng, lnb, w, bias_td, *, tm):
    m = ca.shape[0]
    return pl.pallas_call(
        functools.partial(_sgu_prompt_kernel, nchunk=tm // CHUNK),
        grid=(m // tm,),
        in_specs=[pl.BlockSpec((tm, N_A_COLS), lambda i: (i, 0)),
                  pl.BlockSpec((1, D_A), lambda i: (0, 0)),
                  pl.BlockSpec((1, D_A), lambda i: (0, 0)),
                  pl.BlockSpec((G_A, CHUNK, CHUNK), lambda i: (0, 0, 0)),
                  pl.BlockSpec((CHUNK, D_A), lambda i: (0, 0))],
        out_specs=pl.BlockSpec((tm, D_A), lambda i: (i, 0)),
        out_shape=jax.ShapeDtypeStruct((m, D_A), BF16),
        compiler_params=_cparams("parallel"),
        name="sgu_prompt",
    )(ca, lng, lnb, w, bias_td)


def _sgu_sample_kernel(ca_ref, lng_ref, lnb_ref, w0_ref, b0_ref, oa_ref, va_ref):
    u, va = _gelu_ln(ca_ref[...], lng_ref[...], lnb_ref[...])
    va_ref[...] = va
    oa_ref[...] = (u * (va * w0_ref[...] + b0_ref[...])).astype(oa_ref.dtype)


def _sgu_sample(ca, lng, lnb, w0, b0):
    m = ca.shape[0]
    row = lambda n: pl.BlockSpec((1, n), lambda i: (0, 0))
    return pl.pallas_call(
        _sgu_sample_kernel,
        grid=(1,),
        in_specs=[pl.BlockSpec((m, N_A_COLS), lambda i: (0, 0)), row(D_A), row(D_A), row(D_A), row(D_A)],
        out_specs=[pl.BlockSpec((m, D_A), lambda i: (0, 0))] * 2,
        out_shape=[jax.ShapeDtypeStruct((m, D_A), BF16), jax.ShapeDtypeStruct((m, D_A), F32)],
        compiler_params=_cparams("arbitrary"),
        name="sgu_sample",
    )(ca, lng, lnb, w0, b0)


def _seg_sum(x, ones_bd):
    hi = x.astype(BF16)
    lo = (x - hi.astype(F32)).astype(BF16)
    return (jnp.dot(hi, ones_bd, preferred_element_type=F32)
            + jnp.dot(lo, ones_bd, preferred_element_type=F32))


def _rwkv_pre_kernel(*refs, seq):
    if seq:
        c_ref, prev_ref = refs[:2]
    else:
        c_ref, sh_ref = refs[:2]
    (mu_ref, w0_ref, w2_ref, a0_ref, a2_ref, g2_ref, kk_ref, ka_ref, rk_ref, bd_ref) = refs[2:12]
    (r_o, w_o, k_o, v_o, a_o, b_o, g_o, rkv_o) = refs[12:20]
    if seq:
        last_o, carry_ref = refs[20:22]
        cols = c_ref[0]
        tm = cols.shape[0]

        @pl.when(pl.program_id(1) == 0)
        def _():
            carry_ref[...] = prev_ref[0]

        rolled = pltpu.roll(cols, 1, 0)
        first = lax.broadcasted_iota(jnp.int32, cols.shape, 0) == 0
        shifted = jnp.where(first, carry_ref[...], rolled)
        carry_ref[...] = cols[tm - 1:tm, :]
        last_o[0] = cols[tm - 1:tm, :]
    else:
        cols = c_ref[...]
        shifted = sh_ref[...]
    xs = cols + (shifted - cols) * mu_ref[...]
    r = xs[:, 0:D_B]
    k = xs[:, D_B:2 * D_B]
    v = xs[:, 2 * D_B:3 * D_B]
    o = 3 * D_B
    wl = xs[:, o:o + LW_PAD]
    al = xs[:, o + LW_PAD:o + LW_PAD + LA_PAD]
    gl = xs[:, o + LW_PAD + LA_PAD:o + LW_PAD + LA_PAD + LG_PAD]
    z = -(w0_ref[...] + jnp.dot(jnp.tanh(wl).astype(BF16), w2_ref[...], preferred_element_type=F32))
    softplus = jnp.maximum(z, 0.0) + jnp.log1p(jnp.exp(-jnp.abs(z)))
    log_decay = -jnp.exp(-softplus - 0.5)
    decay = log_decay if seq else jnp.exp(log_decay)
    a = jax.nn.sigmoid(a0_ref[...] + jnp.dot(al.astype(BF16), a2_ref[...], preferred_element_type=F32))
    g = jnp.dot(jax.nn.sigmoid(gl).astype(BF16), g2_ref[...], preferred_element_type=F32)
    bd = bd_ref[...]
    kk = k * kk_ref[...]
    kk = kk / jnp.maximum(jnp.sqrt(_seg_sum(kk * kk, bd)), 1e-12)
    k2 = k * (1.0 + (a - 1.0) * ka_ref[...])
    if seq:
        r_o[0], w_o[0], k_o[0], v_o[0], a_o[0], b_o[0], g_o[0] = r, decay, k2, v, -kk, kk * a, g
        rkv_o[0] = _seg_sum(r * k2 * rk_ref[...], bd) * v
    else:
        r_o[...], w_o[...], k_o[...], v_o[...], a_o[...], b_o[...], g_o[...] = r, decay, k2, v, -kk, kk * a, g
        rkv_o[...] = _seg_sum(r * k2 * rk_ref[...], bd) * v


def _rwkv_pre(cols, shift_src, params, *, seq, tm):
    prm_specs = []
    for p in params:
        prm_specs.append(pl.BlockSpec(p.shape, (lambda b, i: (0, 0)) if seq else (lambda i: (0, 0))))
    if seq:
        nb, t, _ = cols.shape
        grid = (nb, t // tm)
        in_specs = [pl.BlockSpec((1, tm, N_B_PAD), lambda b, i: (b, i, 0)),
                    pl.BlockSpec((1, 1, N_B_PAD), lambda b, i: (b, 0, 0))] + prm_specs
        ospec = pl.BlockSpec((1, tm, D_B), lambda b, i: (b, i, 0))
        oshape = jax.ShapeDtypeStruct((nb, t, D_B), F32)
        out_specs = [ospec] * 8 + [pl.BlockSpec((1, 1, N_B_PAD), lambda b, i: (b, 0, 0))]
        out_shape = [oshape] * 8 + [jax.ShapeDtypeStruct((nb, 1, N_B_PAD), F32)]
        scratch = [pltpu.VMEM((1, N_B_PAD), F32)]
        sem = ("parallel", "arbitrary")
    else:
        m = cols.shape[0]
        grid = (m // tm,)
        in_specs = [pl.BlockSpec((tm, N_B_PAD), lambda i: (i, 0))] * 2 + prm_specs
        out_specs = [pl.BlockSpec((tm, D_B), lambda i: (i, 0))] * 8
        out_shape = [jax.ShapeDtypeStruct((m, D_B), F32)] * 8
        scratch = []
        sem = ("parallel",)
    return pl.pallas_call(
        functools.partial(_rwkv_pre_kernel, seq=seq),
        grid=grid, in_specs=in_specs, out_specs=out_specs, out_shape=out_shape,
        scratch_shapes=scratch, compiler_params=_cparams(*sem),
        name="rwkv_pre_seq" if seq else "rwkv_pre_rows",
    )(cols, shift_src, *params)


def _scan_pair_step(S, a, w, b, k, v, r):
    lane = lax.broadcasted_iota(jnp.int32, (N_B, LANES), 1)
    row = lax.broadcasted_iota(jnp.int32, (N_B, LANES), 0)
    lo = lane < N_B
    e1 = lane == row
    e2 = lane == row + N_B

    def half_sums(p):
        t1 = jnp.sum(jnp.where(lo, p, 0.0), axis=1, keepdims=True)
        t2 = jnp.sum(jnp.where(lo, 0.0, p), axis=1, keepdims=True)
        return t1, t2

    t1, t2 = half_sums(S * a)
    sa = jnp.where(lo, t1, t2)
    vb = jnp.where(lo, jnp.sum(jnp.where(e1, v, 0.0), axis=1, keepdims=True),
                   jnp.sum(jnp.where(e2, v, 0.0), axis=1, keepdims=True))
    S = S * w + sa * b + vb * k
    y1, y2 = half_sums(S * r)
    y = jnp.sum(jnp.where(e1, y1, 0.0) + jnp.where(e2, y2, 0.0), axis=0, keepdims=True)
    return S, y


def _split(x):
    hi = x.astype(BF16)
    return hi, (x - hi.astype(F32)).astype(BF16)


def _mm3(xs, ys, nt=False):
    lhs = jnp.concatenate([xs[0], xs[0], xs[1]], axis=1)
    if nt:
        rhs = jnp.concatenate([ys[0], ys[1], ys[0]], axis=1)
        return lax.dot_general(lhs, rhs, (((1,), (1,)), ((), ())), preferred_element_type=F32)
    rhs = jnp.concatenate([ys[0], ys[1], ys[0]], axis=0)
    return jnp.dot(lhs, rhs, preferred_element_type=F32)


def _mm1(x, y, nt=False):
    if nt:
        return lax.dot_general(x, y, (((1,), (1,)), ((), ())), preferred_element_type=F32)
    return jnp.dot(x, y, preferred_element_type=F32)


def _block_diag(x):
    lo = lax.broadcasted_iota(jnp.int32, x.shape, 1) < N_B
    zero = jnp.zeros_like(x)
    return jnp.concatenate([jnp.where(lo, x, zero), jnp.where(lo, zero, x)], axis=0)


def _each(fn, *lists):
    return [fn(*args) for args in zip(*lists)]


def _rwkv_chunk_pairs(zs, lws, a_s, bs, ks, vs, rs):
    c = lws[0].shape[0]
    tri = (lax.broadcasted_iota(jnp.int32, (c, c), 0) >= lax.broadcasted_iota(jnp.int32, (c, c), 1)).astype(BF16)
    tri3 = jnp.concatenate([tri, tri, tri], axis=1)

    def cumsum(lw):
        l1 = lw.astype(BF16)
        r1 = lw - l1.astype(F32)
        l2 = r1.astype(BF16)
        l3 = (r1 - l2.astype(F32)).astype(BF16)
        return jnp.dot(tri3, jnp.concatenate([l1, l2, l3], axis=0), preferred_element_type=F32)

    lam = _each(cumsum, lws)
    lam_c = [x[c - 1:c, :] for x in lam]
    w_inv = [jnp.exp(-x) for x in lam]
    w_rem = _each(lambda lc, x: jnp.exp(lc - x), lam_c, lam)
    r_h = _each(lambda r, x: _block_diag(r * jnp.exp(x)), rs, lam)
    s_a = _each(lambda a, x, lw: _split(_block_diag(a * jnp.exp(x - lw))), a_s, lam, lws)
    h_r = [x.astype(BF16) for x in r_h]
    s_b = _each(lambda b, wi: _split(_block_diag(b * wi)), bs, w_inv)
    s_k = _each(lambda k, wi: _split(_block_diag(k * wi)), ks, w_inv)
    s_v = [_split(_block_diag(v)) for v in vs]
    n = 2 * c
    ri = lax.broadcasted_iota(jnp.int32, (n, n), 0)
    ci = lax.broadcasted_iota(jnp.int32, (n, n), 1)
    strict, incl, eye = ci < ri, ci <= ri, ci == ri
    gram = lambda x, y: jnp.where(strict, _mm3(x, y, nt=True), 0.0)
    lab = _each(gram, s_a, s_b)
    lak = _each(gram, s_a, s_k)
    h_lrb = _each(lambda x, y: jnp.where(incl, _mm1(x, y[0], nt=True), 0.0).astype(BF16), h_r, s_b)
    h_lrk = _each(lambda x, y: jnp.where(incl, _mm1(x, y[0], nt=True), 0.0).astype(BF16), h_r, s_k)
    t = [jnp.where(eye, 1.0, 0.0) + x for x in lab]
    s_p = _each(_split, lab)
    for _ in range(int(math.log2(c)) - 1):
        s_p = _each(lambda p: _split(_mm3(p, p)), s_p)
        t = _each(lambda tt, p: tt + _mm3(_split(tt), p), t, s_p)
    s_t = _each(_split, t)
    s_at = _each(lambda tt, a: _split(_mm3(tt, a)), s_t, s_a)
    x2 = _each(lambda l, v: _split(_mm3(_split(l), v)), lak, s_v)
    s_u0 = _each(lambda tt, x: _split(_mm3(tt, x)), s_t, x2)
    rt = _each(lambda rh, l, at: rh + _mm1(l, at[0]), r_h, h_lrb, s_at)
    y0 = _each(lambda l, u, lk, v: _mm1(l, u[0]) + _mm1(lk, v[0]), h_lrb, s_u0, h_lrk, s_v)
    s_bt = _each(lambda b, wr: _split(_block_diag(b * wr).T), bs, w_rem)
    s_kt = _each(lambda k, wr: _split(_block_diag(k * wr).T), ks, w_rem)
    m = _each(lambda lc, bt, at: jnp.where(eye, jnp.exp(lc), 0.0) + _mm3(bt, at), lam_c, s_bt, s_at)
    nn = _each(lambda bt, u, kt, v: _mm3(bt, u) + _mm3(kt, v), s_bt, s_u0, s_kt, s_v)
    s_z = _each(_split, zs)
    ybd = _each(lambda r, z, y: _mm1(r.astype(BF16), z[0]) + y, rt, s_z, y0)
    z_new = _each(lambda mm, z, x: _mm3(_split(mm), z) + x, m, s_z, nn)
    return z_new, [y[:c] + y[c:] for y in ybd]


def _rwkv_scan_seq_kernel(r_ref, w_ref, k_ref, v_ref, a_ref, b_ref, y_ref, sf_ref, z_ref, *, nb):
    npair = H_B // 2
    ci = pl.program_id(0)

    @pl.when(ci == 0)
    def _():
        z_ref[...] = jnp.zeros_like(z_ref)

    sls = [(bi, slice(None), slice(p * LANES, (p + 1) * LANES)) for bi in range(nb) for p in range(npair)]
    rows = lambda ref: [ref[sl] for sl in sls]
    zs, ys = _rwkv_chunk_pairs([z_ref[c] for c in range(len(sls))], rows(w_ref), rows(a_ref),
                               rows(b_ref), rows(k_ref), rows(v_ref), rows(r_ref))
    for c, sl in enumerate(sls):
        z_ref[c] = zs[c]
        y_ref[sl] = ys[c]

    @pl.when(ci == pl.num_programs(0) - 1)
    def _():
        for c, (bi, _, _) in enumerate(sls):
            st = z_ref[c].T
            sf_ref[bi, 2 * (c % npair)] = st[:N_B, :N_B]
            sf_ref[bi, 2 * (c % npair) + 1] = st[N_B:, N_B:]


def _rwkv_scan_seq(r, lw, k, v, a, b, *, tc):
    nb, t, _ = r.shape
    spec = pl.BlockSpec((nb, tc, D_B), lambda c: (0, c, 0))
    return pl.pallas_call(
        functools.partial(_rwkv_scan_seq_kernel, nb=nb),
        grid=(t // tc,),
        in_specs=[spec] * 6,
        out_specs=[spec, pl.BlockSpec((nb, H_B, N_B, N_B), lambda c: (0, 0, 0, 0))],
        out_shape=[jax.ShapeDtypeStruct((nb, t, D_B), F32), jax.ShapeDtypeStruct((nb, H_B, N_B, N_B), F32)],
        scratch_shapes=[pltpu.VMEM((nb * H_B // 2, LANES, LANES), F32)],
        compiler_params=_cparams("arbitrary"),
        name="rwkv_scan_seq",
    )(r, lw, k, v, a, b)


def _rwkv_scan_rows_kernel(r_ref, w_ref, k_ref, v_ref, a_ref, b_ref, s0_ref, y_ref, sf_ref, *, tb):
    def group(gi, carry):
        base = pl.multiple_of(gi * SUBLANES, SUBLANES)
        for p in range(H_B // 2):
            sl = (pl.ds(base, SUBLANES), slice(p * LANES, (p + 1) * LANES))
            a, w, b, k, v, r = (ref[sl] for ref in (a_ref, w_ref, b_ref, k_ref, v_ref, r_ref))
            ys = []
            for j in range(SUBLANES):
                row = slice(j, j + 1)
                S0 = jnp.concatenate([s0_ref[base + j, 2 * p], s0_ref[base + j, 2 * p + 1]], axis=1)
                S, y = _scan_pair_step(S0, a[row], w[row], b[row], k[row], v[row], r[row])
                sf_ref[base + j, 2 * p] = S[:, :N_B]
                sf_ref[base + j, 2 * p + 1] = S[:, N_B:]
                ys.append(y)
            y_ref[sl] = jnp.concatenate(ys, axis=0)
        return carry

    lax.fori_loop(0, tb // SUBLANES, group, 0)


def _rwkv_scan_rows(r, w, k, v, a, b, s0, *, tb):
    m = r.shape[0]
    spec = pl.BlockSpec((tb, D_B), lambda i: (i, 0))
    sspec = pl.BlockSpec((tb, H_B, N_B, N_B), lambda i: (i, 0, 0, 0))
    return pl.pallas_call(
        functools.partial(_rwkv_scan_rows_kernel, tb=tb),
        grid=(m // tb,),
        in_specs=[spec] * 6 + [sspec],
        out_specs=[spec, sspec],
        out_shape=[jax.ShapeDtypeStruct((m, D_B), F32), jax.ShapeDtypeStruct((m, H_B, N_B, N_B), F32)],
        compiler_params=_cparams("parallel"),
        name="rwkv_scan_rows",
    )(r, w, k, v, a, b, s0)


def _rwkv_post_kernel(y_ref, rkv_ref, g_ref, lg_ref, lb_ref, bd_ref, o_ref):
    y = y_ref[...]
    bd = bd_ref[...]
    d = y - _seg_sum(y, bd) * (1.0 / N_B)
    var = _seg_sum(d * d, bd) * (1.0 / N_B)
    yn = d * lax.rsqrt(var + GN_EPS_B) * lg_ref[...] + lb_ref[...]
    o_ref[...] = ((yn + rkv_ref[...]) * g_ref[...]).astype(o_ref.dtype)


def _rwkv_post(y, rkv, g, lg, lb, bd, *, tm):
    m = y.shape[0]
    spec = pl.BlockSpec((tm, D_B), lambda i: (i, 0))
    row = pl.BlockSpec((1, D_B), lambda i: (0, 0))
    return pl.pallas_call(
        _rwkv_post_kernel,
        grid=(m // tm,),
        in_specs=[spec, spec, spec, row, row, pl.BlockSpec((D_B, D_B), lambda i: (0, 0))],
        out_specs=spec,
        out_shape=jax.ShapeDtypeStruct((m, D_B), BF16),
        compiler_params=_cparams("parallel"),
        name="rwkv_post",
    )(y, rkv, g, lg, lb, bd)


def _lam_full(lq1, lk1, lq2, lk2, lam_init):
    return (jnp.exp(jnp.sum(lq1 * lk1, axis=-1, keepdims=True))
            - jnp.exp(jnp.sum(lq2 * lk2, axis=-1, keepdims=True)) + lam_init)


ONES_ROWS = 16


def _attn_prompt_kernel(q_ref, k_ref, vt_ref, lq1, lk1, lq2, lk2, sg_ref, o_ref, m1, a1, m2, a2, s_a, s_b, *,
                        tk, lam_init):
    qi = pl.program_id(2)
    for m_ref, a_ref in ((m1, a1), (m2, a2)):
        m_ref[...] = jnp.full_like(m_ref, -jnp.inf)
        a_ref[...] = jnp.zeros_like(a_ref)

    q = q_ref[0]
    lane = lax.broadcasted_iota(jnp.int32, q.shape, 1)
    zero = jnp.zeros_like(q)
    q_sub = (jnp.where(lane < D_HEAD, q, zero), jnp.where(lane < D_HEAD, zero, q))
    ones = jnp.ones((ONES_ROWS, tk), BF16)

    units = [(qs[c0:c0 + ATTN_STRIP], m_ref, a_ref, c0)
             for qs, m_ref, a_ref in ((q_sub[0], m1, a1), (q_sub[1], m2, a2))
             for c0 in range(0, q.shape[0], ATTN_STRIP)]

    n_units = len(units)

    def scores(u, kis):
        return [lax.dot_general(k_ref[0, pl.ds(pl.multiple_of(ki * tk, tk), tk), :], units[u][0],
                                (((1,), (1,)), ((), ())), preferred_element_type=F32) for ki in kis]

    def stash(s_ref, u, blocks):
        for i, st in enumerate(blocks):
            s_ref[u, i] = st

    def unstash(s_ref, u):
        return [s_ref[u, i] for i in range(ATTN_GROUP)]

    def softmax_pv(u, kis, st_blocks, masked):
        _, m_ref, a_ref, c0 = units[u]
        cols = slice(c0, c0 + ATTN_STRIP)
        if masked:
            kr = lax.broadcasted_iota(jnp.int32, st_blocks[0].shape, 0)
            qc = lax.broadcasted_iota(jnp.int32, st_blocks[0].shape, 1) + c0
            st_blocks = [jnp.where(kr <= qc, st, -jnp.inf) for st in st_blocks]
        m_old = m_ref[:, cols]
        m_new = m_old
        for st in st_blocks:
            m_new = jnp.maximum(m_new, jnp.max(st, axis=0, keepdims=True))
        acc = jnp.exp2(m_old - m_new) * a_ref[:, cols]
        for st, ki in zip(st_blocks, kis):
            vt1 = jnp.concatenate([vt_ref[0, ki], ones], axis=0)
            acc = acc + jnp.dot(vt1, jnp.exp2(st - m_new).astype(BF16), preferred_element_type=F32)
        a_ref[:, cols] = acc
        m_ref[:, cols] = m_new

    def update(kis, masked):
        sts = [scores(u, kis) for u in range(n_units)]
        for u in range(n_units):
            softmax_pv(u, kis, sts[u], masked)

    def pipelined(cur_kis, cur_ref, next_kis, next_ref):
        for u in range(n_units):
            stash(next_ref, u, scores(u, next_kis))
            softmax_pv(u, cur_kis, unstash(cur_ref, u), False)

    group = lambda g: [ATTN_GROUP * g + i for i in range(ATTN_GROUP)]
    n_groups = lax.shift_right_logical(qi, ATTN_GROUP.bit_length() - 1)

    @pl.when(n_groups > 0)
    def _():
        for u in range(n_units):
            stash(s_a, u, scores(u, group(0)))

    def pair_body(jj, carry):
        g0 = 2 * jj
        pipelined(group(g0), s_a, group(g0 + 1), s_b)
        pipelined(group(g0 + 1), s_b, [jnp.minimum(ki, qi) for ki in group(g0 + 2)], s_a)
        return carry

    lax.fori_loop(0, lax.shift_right_logical(n_groups, 1), pair_body, 0)

    @pl.when((n_groups & 1) == 1)
    def _():
        for u in range(n_units):
            softmax_pv(u, group(n_groups - 1), unstash(s_a, u), False)

    def single_body(ki, carry):
        update([ki], False)
        return carry

    lax.fori_loop(n_groups * ATTN_GROUP, qi, single_body, 0)
    update([qi], True)
    lam = _lam_full(lq1[...], lk1[...], lq2[...], lk2[...], lam_init)
    d = 2 * D_HEAD
    ot = a1[:d, :] / a1[d:d + 1, :] - lam * (a2[:d, :] / a2[d:d + 1, :])
    on = ot * lax.rsqrt(jnp.mean(ot * ot, axis=0, keepdims=True) + SUBLN_EPS)
    o_ref[0] = (on.T * sg_ref[...] * (1.0 - lam_init)).astype(o_ref.dtype)


def _attn_prompt(q, k, vt, lams, subln_g, lam_init, *, tq):
    nb, t, _ = q.shape
    tk = vt.shape[3]
    assert tk == tq
    qspec = pl.BlockSpec((1, tq, LANES), lambda b, h, i: (b, i, h))
    kspec = pl.BlockSpec((1, t, LANES), lambda b, h, i: (b, 0, h))
    vspec = pl.BlockSpec((1, t // tk, LANES, tk), lambda b, h, i: (b, 0, h, 0))
    row = lambda n: pl.BlockSpec((1, n), lambda b, h, i: (0, 0))
    acc = 2 * D_HEAD + ONES_ROWS
    return pl.pallas_call(
        functools.partial(_attn_prompt_kernel, tk=tk, lam_init=lam_init),
        grid=(nb, H_C, t // tq),
        in_specs=[qspec, kspec, vspec, row(D_HEAD), row(D_HEAD), row(D_HEAD), row(D_HEAD), row(LANES)],
        out_specs=qspec,
        out_shape=jax.ShapeDtypeStruct(q.shape, BF16),
        scratch_shapes=[pltpu.VMEM((1, tq), F32), pltpu.VMEM((acc, tq), F32),
                        pltpu.VMEM((1, tq), F32), pltpu.VMEM((acc, tq), F32)]
                       + [pltpu.VMEM((2 * tq // ATTN_STRIP, ATTN_GROUP, tk, ATTN_STRIP), F32)] * 2,
        compiler_params=_cparams("parallel", "parallel", "arbitrary"),
        name="attn_prompt",
    )(q, k, vt, *lams, subln_g)


def _attn_sample_kernel(pt_ref, q_ref, kn_ref, vn_ref, *refs, npg, lam_init):
    k_refs = refs[:npg]
    v_refs = refs[npg:2 * npg]
    lq1, lk1, lq2, lk2, sg_ref, o_ref, m_ref, l_ref, acc_ref = refs[2 * npg:]
    g = pl.program_id(1)
    nrow, ncol = 2 * H_C, PAGE_SIZE * H_C
    q = q_ref[0]
    lo = lax.broadcasted_iota(jnp.int32, q.shape, 1) < D_HEAD
    qz = jnp.concatenate([jnp.where(lo, q, 0.0), jnp.where(lo, 0.0, q)], axis=0).astype(BF16)
    valid = ((lax.broadcasted_iota(jnp.int32, (nrow, ncol), 1) & (H_C - 1))
             == (lax.broadcasted_iota(jnp.int32, (nrow, ncol), 0) & (H_C - 1)))

    @pl.when(g == 0)
    def _():
        as_mxu = lambda ref: jnp.concatenate([ref[0], ref[0]], axis=0).astype(BF16).astype(F32)
        m_ref[...] = jnp.sum(qz.astype(F32) * as_mxu(kn_ref), axis=1, keepdims=True)
        l_ref[...] = jnp.ones_like(l_ref)
        acc_ref[...] = as_mxu(vn_ref)

    sts = []
    for r in range(npg):
        kmat = k_refs[r][...].reshape(ncol, LANES).astype(BF16)
        st = lax.dot_general(qz, kmat, (((1,), (1,)), ((), ())), preferred_element_type=F32)
        sts.append(jnp.where(valid, st, -jnp.inf))
    m_old = m_ref[...]
    m_new = m_old
    for st in sts:
        m_new = jnp.maximum(m_new, jnp.max(st, axis=1, keepdims=True))
    alpha = jnp.exp2(m_old - m_new)
    l = alpha * l_ref[...]
    acc = alpha * acc_ref[...]
    for st, v_ref in zip(sts, v_refs):
        p = jnp.exp2(st - m_new)
        l = l + jnp.sum(p, axis=1, keepdims=True)
        acc = acc + jnp.dot(p.astype(BF16), v_ref[...].reshape(ncol, LANES).astype(BF16),
                            preferred_element_type=F32)
    m_ref[...] = m_new
    l_ref[...] = l
    acc_ref[...] = acc

    @pl.when(g == pl.num_programs(1) - 1)
    def _():
        o = acc_ref[...] / l_ref[...]
        lam = _lam_full(lq1[...], lk1[...], lq2[...], lk2[...], lam_init)
        o_ref[0] = (_rms(o[:H_C] - lam * o[H_C:], sg_ref[...], SUBLN_EPS) * (1.0 - lam_init)).astype(o_ref.dtype)


def _attn_sample(q, k_new, v_new, cache_k, cache_v, page_table, layer, lams, subln_g, lam_init, *, npg):
    nb = q.shape[0]
    n_pages = page_table.shape[1]
    hspec = pl.BlockSpec((1, H_C, LANES), lambda b, g, pt: (b, 0, 0))
    row = lambda n: pl.BlockSpec((1, n), lambda b, g, pt: (0, 0))

    def page_spec(r):
        return pl.BlockSpec((None, None, PAGE_SIZE, H_C, LANES),
                            lambda b, g, pt: (layer, pt[b, g * npg + r], 0, 0, 0))

    grid_spec = pltpu.PrefetchScalarGridSpec(
        num_scalar_prefetch=1,
        grid=(nb, n_pages // npg),
        in_specs=[hspec, hspec, hspec] + [page_spec(r) for r in range(npg)] * 2
                 + [row(D_HEAD), row(D_HEAD), row(D_HEAD), row(D_HEAD), row(LANES)],
        out_specs=hspec,
        scratch_shapes=[pltpu.VMEM((2 * H_C, 1), F32), pltpu.VMEM((2 * H_C, 1), F32),
                        pltpu.VMEM((2 * H_C, LANES), F32)],
    )
    return pl.pallas_call(
        functools.partial(_attn_sample_kernel, npg=npg, lam_init=lam_init),
        grid_spec=grid_spec,
        out_shape=jax.ShapeDtypeStruct((nb, H_C, LANES), BF16),
        compiler_params=_cparams("parallel", "arbitrary"),
        name="attn_sample",
    )(page_table, q, k_new, v_new, *([cache_k] * npg), *([cache_v] * npg), *lams, subln_g)


def _merge_kernel(x_ref, oa_ref, ob_ref, oc_ref, cg_ref, wa_ref, wb_ref, wc_ref, wo_ref, xo_ref):
    gates = cg_ref[...].astype(F32)
    dot = lambda a, w: jnp.dot(a, w, preferred_element_type=F32)
    m = (gates[:, 0:D_MODEL] * dot(oa_ref[...], wa_ref[...])
         + gates[:, D_MODEL:2 * D_MODEL] * dot(ob_ref[...], wb_ref[...])
         + gates[:, 2 * D_MODEL:3 * D_MODEL] * dot(oc_ref[...], wc_ref[...]))
    xo_ref[...] = x_ref[...] + dot(m.astype(BF16), wo_ref[...])


def _merge(x, oa, ob, oc, cg, wa, wb, wc, wo, *, tm):
    m = x.shape[0]
    rows = lambda n: pl.BlockSpec((tm, n), lambda i: (i, 0))
    full = lambda a: pl.BlockSpec(a.shape, lambda i: (0, 0))
    return pl.pallas_call(
        _merge_kernel,
        grid=(m // tm,),
        in_specs=[rows(D_MODEL), rows(D_A), rows(D_B), rows(D_C), rows(3 * D_MODEL),
                  full(wa), full(wb), full(wc), full(wo)],
        out_specs=rows(D_MODEL),
        out_shape=jax.ShapeDtypeStruct((m, D_MODEL), F32),
        compiler_params=_cparams("parallel"),
        name="merge",
    )(x, oa, ob, oc, cg, wa, wb, wc, wo)


def _ffn_kernel(x_ref, g_ref, wu_ref, wd_ref, gf_ref, o_ref, h_ref, acc_ref, *, final):
    j = pl.program_id(1)

    @pl.when(j == 0)
    def _():
        h_ref[...] = _rms(x_ref[...], g_ref[...], EPS).astype(BF16)
        acc_ref[...] = jnp.zeros_like(acc_ref)

    up = jnp.dot(h_ref[...], wu_ref[...], preferred_element_type=F32)
    act = jnp.square(jnp.maximum(up, 0.0)).astype(BF16)
    acc_ref[...] += jnp.dot(act, wd_ref[...], preferred_element_type=F32)

    @pl.when(j == pl.num_programs(1) - 1)
    def _():
        xn = x_ref[...] + acc_ref[...]
        o_ref[...] = _rms(xn, gf_ref[...], EPS) if final else xn


def _ffn(x, g, wu, wd, gf, *, final, tm, tf):
    m = x.shape[0]
    row = pl.BlockSpec((1, D_MODEL), lambda i, j: (0, 0))
    return pl.pallas_call(
        functools.partial(_ffn_kernel, final=final),
        grid=(m // tm, D_FF // tf),
        in_specs=[pl.BlockSpec((tm, D_MODEL), lambda i, j: (i, 0)), row,
                  pl.BlockSpec((D_MODEL, tf), lambda i, j: (0, j)),
                  pl.BlockSpec((tf, D_MODEL), lambda i, j: (j, 0)), row],
        out_specs=pl.BlockSpec((tm, D_MODEL), lambda i, j: (i, 0)),
        out_shape=jax.ShapeDtypeStruct((m, D_MODEL), F32),
        scratch_shapes=[pltpu.VMEM((tm, D_MODEL), BF16), pltpu.VMEM((tm, D_MODEL), F32)],
        compiler_params=_cparams("parallel", "arbitrary"),
        name="ffn",
    )(x, g, wu, wd, gf)


def _pad_b_cols(a):
    o = 3 * D_B
    z = lambda n: jnp.zeros(a.shape[:-1] + (n,), a.dtype)
    return jnp.concatenate([a[..., :o + LORA_W], z(LW_PAD - LORA_W),
                            a[..., o + LORA_W:o + LORA_W + LORA_A], z(LA_PAD - LORA_A),
                            a[..., o + LORA_W + LORA_A:], z(LG_PAD - LORA_G)], axis=-1)


def _unpad_b_cols(a):
    o = 3 * D_B
    return jnp.concatenate([a[..., :o + LORA_W], a[..., o + LW_PAD:o + LW_PAD + LORA_A],
                            a[..., o + LW_PAD + LA_PAD:o + LW_PAD + LA_PAD + LORA_G]], axis=-1)


def _pad_rows(a, n):
    return jnp.concatenate([a, jnp.zeros((n - a.shape[0],) + a.shape[1:], a.dtype)], axis=0)


def _ones_blockdiag(n, seg):
    i = np.arange(n) // seg
    return jnp.asarray((i[:, None] == i[None, :]).astype(np.float32), dtype=BF16)


def _layer_weights(l, p):
    w_in = p["w_in"][l]
    o_b = N_A_COLS
    o_q = o_b + N_B_COLS
    w = dict(
        norm_mix=p["norm_mix"][l].reshape(1, D_MODEL),
        w_a=w_in[:, :o_b].astype(BF16),
        w_b=_pad_b_cols(w_in[:, o_b:o_q]).astype(BF16),
        w_q=w_in[:, o_q:o_q + D_C].astype(BF16),
        w_k=w_in[:, o_q + D_C:o_q + 2 * D_C].astype(BF16),
        w_v=w_in[:, o_q + 2 * D_C:o_q + 3 * D_C].astype(BF16),
        w_g=w_in[:, o_q + 3 * D_C:].astype(BF16),
        lng=p["sgu_ln_g"][l].reshape(1, D_A), lnb=p["sgu_ln_b"][l].reshape(1, D_A),
        sgu_w=p["sgu_w"][l],
        sgu_bias_td=jnp.repeat(p["sgu_b"][l].T, D_A // G_A, axis=1),
        sgu_w0=jnp.repeat(p["sgu_w"][l][:, 0, 0], D_A // G_A).reshape(1, D_A),
        sgu_b0=jnp.repeat(p["sgu_b"][l][:, 0], D_A // G_A).reshape(1, D_A),
        rwkv=[_pad_b_cols(p["shift_mu"][l]).reshape(1, N_B_PAD),
              p["w0"][l].reshape(1, D_B), _pad_rows(p["w2"][l], LW_PAD).astype(BF16),
              p["a0"][l].reshape(1, D_B), _pad_rows(p["a2"][l], LA_PAD).astype(BF16),
              _pad_rows(p["g2"][l], LG_PAD).astype(BF16),
              p["k_k"][l].reshape(1, D_B), p["k_a"][l].reshape(1, D_B), p["r_k"][l].reshape(1, D_B),
              _ones_blockdiag(D_B, N_B)],
        lnx_g=p["lnx_g"][l].reshape(1, D_B), lnx_b=p["lnx_b"][l].reshape(1, D_B),
        lams=[p[n][l].reshape(1, D_HEAD) for n in ("lam_q1", "lam_k1", "lam_q2", "lam_k2")],
        subln_g=p["subln_g"][l].reshape(1, 2 * D_HEAD),
        w_br_a=p["w_br_a"][l].astype(BF16), w_br_b=p["w_br_b"][l].astype(BF16),
        w_br_c=p["w_br_c"][l].astype(BF16), w_out=p["w_out"][l].astype(BF16),
        norm_ffn=p["norm_ffn"][l].reshape(1, D_MODEL),
        w_up=p["w_up"][l].astype(BF16), w_down=p["w_down"][l].astype(BF16),
        norm_final=p["norm_final"].reshape(1, D_MODEL),
    )
    return w


def _project_all(x, w, rope_tabs, tm, t_seq=None, kv_stack=None):
    pj = functools.partial(_proj, x, w["norm_mix"], tm=tm, tn=D_MODEL)
    qscale = math.log2(math.e) / math.sqrt(D_HEAD)
    (ca,) = pj(w["w_a"], out_dtypes=(F32,), name="proj_a")
    (cb,) = pj(w["w_b"], out_dtypes=(F32,), name="proj_b")
    (cg,) = pj(w["w_g"], out_dtypes=(BF16,), gate=True, name="proj_g")
    if t_seq is None:
        (q,) = pj(w["w_q"], out_dtypes=(F32,), rope_tabs=rope_tabs, scale=qscale, name="proj_q")
        (kf,) = pj(w["w_k"], out_dtypes=(F32,), rope_tabs=rope_tabs, name="proj_k")
        (vf,) = pj(w["w_v"], out_dtypes=(F32,), name="proj_v")
        return ca, cb, cg, q, kf, vf
    (q,) = pj(w["w_q"], out_dtypes=(BF16,), rope_tabs=rope_tabs, scale=qscale, name="proj_q")
    layer, k_buf, v_buf = kv_stack
    kf, kb = pj(w["w_k"], out_dtypes=(F32, BF16), rope_tabs=rope_tabs, stack=(layer, k_buf), name="proj_k")
    vf, vt = _proj(x, w["norm_mix"], w["w_v"], tm=ATTN_BLOCK, tn=D_MODEL, out_dtypes=(F32, BF16), t_seq=t_seq,
                   stack=(layer, v_buf), name="proj_v")
    return ca, cb, cg, q, kf, vf, kb, vt


def _prompt_layer(x, l, w, rope_tabs, nb, t, final, kv_stack):
    lam_init = 0.8 - 0.6 * math.exp(-0.3 * l)
    ca, cb, cg, qb, kf, vf, kb, vt = _project_all(x, w, rope_tabs, 1024, t_seq=t, kv_stack=kv_stack)
    o_a = _sgu_prompt(ca, w["lng"], w["lnb"], w["sgu_w"], w["sgu_bias_td"], tm=512)
    prev0 = jnp.zeros((nb, 1, N_B_PAD), F32)
    *rw, last = _rwkv_pre(cb.reshape(nb, t, N_B_PAD), prev0, w["rwkv"], seq=True, tm=256)
    r, dec, k2, v, na, bb, g, rkv = rw
    y, s_fin = _rwkv_scan_seq(r, dec, k2, v, na, bb, tc=64)
    flat = lambda a: a.reshape(nb * t, D_B)
    o_b = _rwkv_post(flat(y), flat(rkv), flat(g), w["lnx_g"], w["lnx_b"], w["rwkv"][-1], tm=1024)
    b3 = lambda a: a.reshape(nb, t, D_C)
    o_c = _attn_prompt(b3(qb), b3(kb), vt, w["lams"], w["subln_g"], lam_init, tq=ATTN_BLOCK)
    x = _merge(x, o_a, o_b, o_c.reshape(nb * t, D_C), cg, w["w_br_a"], w["w_br_b"], w["w_br_c"], w["w_out"], tm=256)
    x = _ffn(x, w["norm_ffn"], w["w_up"], w["w_down"], w["norm_final"], final=final, tm=1024, tf=1024)
    return x, kf, vf, s_fin, _unpad_b_cols(last[:, 0, :])


def _sample_layer(x, l, w, rope_tabs, cache_k, cache_v, page_table, prev_shift, s0, final):
    nb = x.shape[0]
    lam_init = 0.8 - 0.6 * math.exp(-0.3 * l)
    ca, cb, cg, qb, kf, vf = _project_all(x, w, rope_tabs, nb)
    o_a, va = _sgu_sample(ca, w["lng"], w["lnb"], w["sgu_w0"], w["sgu_b0"])
    r, dec, k2, v, na, bb, g, rkv = _rwkv_pre(cb, _pad_b_cols(prev_shift), w["rwkv"], seq=False, tm=nb)
    y, s_new = _rwkv_scan_rows(r, dec, k2, v, na, bb, s0, tb=32)
    o_b = _rwkv_post(y, rkv, g, w["lnx_g"], w["lnx_b"], w["rwkv"][-1], tm=nb)
    h3 = lambda a: a.reshape(nb, H_C, LANES)
    o_c = _attn_sample(h3(qb), h3(kf), h3(vf), cache_k, cache_v, page_table, l,
                       w["lams"], w["subln_g"], lam_init, npg=8)
    x = _merge(x, o_a, o_b, o_c.reshape(nb, D_C), cg, w["w_br_a"], w["w_br_b"], w["w_br_c"], w["w_out"], tm=nb)
    x = _ffn(x, w["norm_ffn"], w["w_up"], w["w_down"], w["norm_final"], final=final, tm=nb, tf=1024)
    return x, kf, vf, s_new, _unpad_b_cols(cb), va


def kernel(x_prompt, x_sample, cache_k, cache_v, state_rwkv, state_shift, page_table, norm_mix, w_in, sgu_ln_g, sgu_ln_b, sgu_w, sgu_b, shift_mu, w0, w2, a0, a2, g2, k_k, k_a, r_k, lnx_g, lnx_b, lam_q1, lam_k1, lam_q2, lam_k2, subln_g, w_br_a, w_br_b, w_br_c, w_out, norm_ffn, w_up, w_down, norm_final):
    p = dict(norm_mix=norm_mix, w_in=w_in, sgu_ln_g=sgu_ln_g, sgu_ln_b=sgu_ln_b, sgu_w=sgu_w, sgu_b=sgu_b,
             shift_mu=shift_mu, w0=w0, w2=w2, a0=a0, a2=a2, g2=g2, k_k=k_k, k_a=k_a, r_k=r_k, lnx_g=lnx_g,
             lnx_b=lnx_b, lam_q1=lam_q1, lam_k1=lam_k1, lam_q2=lam_q2, lam_k2=lam_k2, subln_g=subln_g,
             w_br_a=w_br_a, w_br_b=w_br_b, w_br_c=w_br_c, w_out=w_out, norm_ffn=norm_ffn, w_up=w_up,
             w_down=w_down, norm_final=norm_final)
    depth = w_in.shape[0]
    bp, tp, _ = x_prompt.shape
    bs, ts, _ = x_sample.shape
    tabs_p = _rope_tables(jnp.arange(tp))
    tabs_s = _rope_tables(jnp.full((bs * ts,), PAST_LEN, jnp.int32))
    xp = x_prompt.reshape(bp * tp, D_MODEL)
    xs = x_sample.reshape(bs * ts, D_MODEL)
    outs_p, outs_s = [], []
    kp = jnp.zeros((depth * bp * tp, D_C), F32)
    vp = jnp.zeros((depth * bp * tp, D_C), F32)
    for l in range(depth):
        w = _layer_weights(l, p)
        final = l == depth - 1
        xp, kp, vp, sp, shp = _prompt_layer(xp, l, w, tabs_p, bp, tp, final, (l, kp, vp))
        xs, ks, vs, ss, shs, va = _sample_layer(xs, l, w, tabs_s, cache_k, cache_v, page_table,
                                                state_shift[l], state_rwkv[l], final)
        outs_p.append((sp, shp))
        outs_s.append((ks.reshape(bs, ts, H_C, 2 * D_HEAD), vs.reshape(bs, ts, H_C, 2 * D_HEAD), ss, shs,
                       va.reshape(bs, ts, D_A)))
    stack = lambda outs, i: jnp.stack([o[i] for o in outs])
    return (xp.reshape(bp, tp, D_MODEL), xs.reshape(bs, ts, D_MODEL),
            kp.reshape(depth, bp, tp, H_C, 2 * D_HEAD), vp.reshape(depth, bp, tp, H_C, 2 * D_HEAD),
            stack(outs_p, 0), stack(outs_p, 1),
            stack(outs_s, 0), stack(outs_s, 1), stack(outs_s, 2), stack(outs_s, 3), stack(outs_s, 4))
```

```python
import functools
import math

import numpy as np
import jax
import jax.numpy as jnp
from jax import lax
from jax.experimental import pallas as pl
from jax.experimental.pallas import tpu as pltpu

F32 = jnp.float32
BF16 = jnp.bfloat16

D_MODEL = 1024
PAST_LEN = 2048
PAGE_SIZE = 128
CHUNK = 128
D_A = 512
G_A = 8
H_B = 8
N_B = 64
D_B = H_B * N_B
LORA_W = 64
LORA_A = 64
LORA_G = 160
GN_EPS_B = 64e-5
H_C = 8
D_HEAD = 64
D_C = H_C * 2 * D_HEAD
ROPE_DIM = D_HEAD // 4
ROPE_THETA = 500000.0
SUBLN_EPS = 1e-5
D_FF = 4 * D_MODEL
EPS = 1e-6
N_A_COLS = 2 * D_A
N_B_COLS = 3 * D_B + LORA_W + LORA_A + LORA_G
LANES = 128
SUBLANES = 8
LW_PAD = 128
LA_PAD = 128
LG_PAD = 256
N_B_PAD = 3 * D_B + LW_PAD + LA_PAD + LG_PAD
VMEM_LIMIT = 52 * 1024 * 1024
ATTN_BLOCK = 512
ATTN_GROUP = 2
ATTN_STRIP = 256


def _cparams(*sem):
    return pltpu.CompilerParams(dimension_semantics=sem, vmem_limit_bytes=VMEM_LIMIT)


def _rms(x, g, eps):
    return x * lax.rsqrt(jnp.mean(x * x, axis=-1, keepdims=True) + eps) * g


def _proj_kernel(*refs, rope, scale, n_out, aliased, gate):
    x_ref, g_ref, w_ref = refs[:3]
    pos = 3
    if rope:
        c_ref, s1_ref, s2_ref = refs[3:6]
        pos = 6
    pos += aliased
    out_refs = refs[pos:pos + n_out]
    h_ref = refs[pos + n_out]

    @pl.when(pl.program_id(1) == 0)
    def _():
        h_ref[...] = _rms(x_ref[...], g_ref[...], EPS).astype(BF16)

    acc = jnp.dot(h_ref[...], w_ref[...], preferred_element_type=F32)
    if rope:
        c, s1, s2 = c_ref[...], s1_ref[...], s2_ref[...]
        parts = []
        for hh in range(acc.shape[1] // LANES):
            a = acc[:, hh * LANES:(hh + 1) * LANES]
            parts.append(a * c + pltpu.roll(a, ROPE_DIM // 2, 1) * s1
                         + pltpu.roll(a, LANES - ROPE_DIM // 2, 1) * s2)
        acc = jnp.concatenate(parts, axis=1)
    if scale != 1.0:
        acc = acc * scale
    if gate:
        acc = jax.nn.sigmoid(acc)
    for o_ref in out_refs:
        if len(o_ref.shape) == 4:
            o_ref[0, 0] = acc.T.astype(o_ref.dtype)
        else:
            o_ref[...] = acc.astype(o_ref.dtype)


def _proj(x, g, w, *, tm, tn, out_dtypes, rope_tabs=None, scale=1.0, t_seq=None, stack=None, gate=False,
          name="proj"):
    m, d = x.shape
    n = w.shape[1]
    in_specs = [pl.BlockSpec((tm, d), lambda i, j: (i, 0)),
                pl.BlockSpec((1, d), lambda i, j: (0, 0)),
                pl.BlockSpec((d, tn), lambda i, j: (0, j))]
    args = [x, g, w]
    if rope_tabs is not None:
        nt = rope_tabs[0].shape[0] // tm
        in_specs += [pl.BlockSpec((tm, LANES), lambda i, j: (i % nt, 0))] * 3
        args += list(rope_tabs)
    out_specs = [pl.BlockSpec((tm, tn), lambda i, j: (i, j)) for _ in out_dtypes]
    out_shape = [jax.ShapeDtypeStruct((m, n), dt) for dt in out_dtypes]
    if t_seq is not None:
        nts = t_seq // tm
        out_specs[-1] = pl.BlockSpec((1, 1, tn, tm), lambda i, j: (i // nts, i % nts, j, 0))
        out_shape[-1] = jax.ShapeDtypeStruct((m // t_seq, nts, n, tm), out_dtypes[-1])
    aliases = {}
    if stack is not None:
        layer, buf = stack
        off = layer * (m // tm)
        out_specs[0] = pl.BlockSpec((tm, tn), lambda i, j: (i + off, j))
        out_shape[0] = jax.ShapeDtypeStruct(buf.shape, out_dtypes[0])
        aliases = {len(args): 0}
        in_specs.append(pl.BlockSpec(memory_space=pl.ANY))
        args.append(buf)
    outs = pl.pallas_call(
        functools.partial(_proj_kernel, rope=rope_tabs is not None, scale=scale, n_out=len(out_dtypes),
                          aliased=len(aliases), gate=gate),
        grid=(m // tm, n // tn),
        in_specs=in_specs,
        out_specs=out_specs,
        out_shape=out_shape,
        input_output_aliases=aliases,
        scratch_shapes=[pltpu.VMEM((tm, d), BF16)],
        compiler_params=_cparams("parallel", "arbitrary"),
        name=name,
    )(*args)
    return outs


def _rope_kernel(pos_ref, invf_ref, c_ref, s1_ref, s2_ref):
    ang = pos_ref[...] * invf_ref[...]
    lane = lax.broadcasted_iota(jnp.int32, ang.shape, 1) % D_HEAD
    first = lane < ROPE_DIM // 2
    second = (lane >= ROPE_DIM // 2) & (lane < ROPE_DIM)
    cos, sin = jnp.cos(ang), jnp.sin(ang)
    c_ref[...] = jnp.where(first | second, cos, 1.0)
    s1_ref[...] = jnp.where(second, sin, 0.0)
    s2_ref[...] = jnp.where(first, -sin, 0.0)


def _rope_tables(pos):
    t = pos.shape[0]
    half = ROPE_DIM // 2
    inv_freq = ROPE_THETA ** (-jnp.arange(half, dtype=F32) / half)
    blk = jnp.concatenate([inv_freq, inv_freq, jnp.zeros((D_HEAD - ROPE_DIM,), F32)])
    invf = jnp.concatenate([blk, blk]).reshape(1, LANES)
    tm = min(t, 512)
    return pl.pallas_call(
        _rope_kernel,
        grid=(t // tm,),
        in_specs=[pl.BlockSpec((tm, 1), lambda i: (i, 0)), pl.BlockSpec((1, LANES), lambda i: (0, 0))],
        out_specs=[pl.BlockSpec((tm, LANES), lambda i: (i, 0))] * 3,
        out_shape=[jax.ShapeDtypeStruct((t, LANES), F32)] * 3,
        compiler_params=_cparams("parallel"),
        name="rope_tables",
    )(pos.astype(F32).reshape(t, 1), invf)


def _gelu_ln(ca, lng, lnb):
    gx = 0.5 * ca * (1.0 + lax.erf(ca * math.sqrt(0.5)))
    u, v = gx[:, :D_A], gx[:, D_A:]
    d = v - jnp.mean(v, axis=-1, keepdims=True)
    va = d * lax.rsqrt(jnp.mean(d * d, axis=-1, keepdims=True) + EPS) * lng + lnb
    return u, va


def _sgu_prompt_kernel(ca_ref, lng_ref, lnb_ref, w_ref, b_ref, oa_ref, *, nchunk):
    u, va = _gelu_ln(ca_ref[...], lng_ref[...], lnb_ref[...])
    vab = va.astype(BF16)
    row = lax.broadcasted_iota(jnp.int32, (CHUNK, CHUNK), 0)
    col = lax.broadcasted_iota(jnp.int32, (CHUNK, CHUNK), 1)
    wcat = jnp.concatenate([jnp.where(col <= row, w_ref[g], 0.0).astype(BF16) for g in range(G_A)], axis=1)
    grp = lax.broadcasted_iota(jnp.int32, (CHUNK, D_A), 1) // (D_A // G_A)
    for ci in range(nchunk):
        vc = vab[ci * CHUNK:(ci + 1) * CHUNK]
        vbig = jnp.concatenate([jnp.where(grp == g, vc, jnp.zeros_like(vc)) for g in range(G_A)], axis=0)
        s = jnp.dot(wcat, vbig, preferred_element_type=F32) + b_ref[...]
        oa_ref[ci * CHUNK:(ci + 1) * CHUNK, :] = (u[ci * CHUNK:(ci + 1) * CHUNK] * s).astype(oa_ref.dtype)


def _sgu_prompt(ca, lng, lnb, w, bias_td, *, tm):
    m = ca.shape[0]
    return pl.pallas_call(
        functools.partial(_sgu_prompt_kernel, nchunk=tm // CHUNK),
        grid=(m // tm,),
        in_specs=[pl.BlockSpec((tm, N_A_COLS), lambda i: (i, 0)),
                  pl.BlockSpec((1, D_A), lambda i: (0, 0)),
                  pl.BlockSpec((1, D_A), lambda i: (0, 0)),
                  pl.BlockSpec((G_A, CHUNK, CHUNK), lambda i: (0, 0, 0)),
                  pl.BlockSpec((CHUNK, D_A), lambda i: (0, 0))],
        out_specs=pl.BlockSpec((tm, D_A), lambda i: (i, 0)),
        out_shape=jax.ShapeDtypeStruct((m, D_A), BF16),
        compiler_params=_cparams("parallel"),
        name="sgu_prompt",
    )(ca, lng, lnb, w, bias_td)


def _sgu_sample_kernel(ca_ref, lng_ref, lnb_ref, w0_ref, b0_ref, oa_ref, va_ref):
    u, va = _gelu_ln(ca_ref[...], lng_ref[...], lnb_ref[...])
    va_ref[...] = va
    oa_ref[...] = (u * (va * w0_ref[...] + b0_ref[...])).astype(oa_ref.dtype)


def _sgu_sample(ca, lng, lnb, w0, b0):
    m = ca.shape[0]
    row = lambda n: pl.BlockSpec((1, n), lambda i: (0, 0))
    return pl.pallas_call(
        _sgu_sample_kernel,
        grid=(1,),
        in_specs=[pl.BlockSpec((m, N_A_COLS), lambda i: (0, 0)), row(D_A), row(D_A), row(D_A), row(D_A)],
        out_specs=[pl.BlockSpec((m, D_A), lambda i: (0, 0))] * 2,
        out_shape=[jax.ShapeDtypeStruct((m, D_A), BF16), jax.ShapeDtypeStruct((m, D_A), F32)],
        compiler_params=_cparams("arbitrary"),
        name="sgu_sample",
    )(ca, lng, lnb, w0, b0)


def _seg_sum(x, ones_bd):
    hi = x.astype(BF16)
    lo = (x - hi.astype(F32)).astype(BF16)
    return (jnp.dot(hi, ones_bd, preferred_element_type=F32)
            + jnp.dot(lo, ones_bd, preferred_element_type=F32))


def _rwkv_pre_kernel(*refs, seq):
    if seq:
        c_ref, prev_ref = refs[:2]
    else:
        c_ref, sh_ref = refs[:2]
    (mu_ref, w0_ref, w2_ref, a0_ref, a2_ref, g2_ref, kk_ref, ka_ref, rk_ref, bd_ref) = refs[2:12]
    (r_o, w_o, k_o, v_o, a_o, b_o, g_o, rkv_o) = refs[12:20]
    if seq:
        last_o, carry_ref = refs[20:22]
        cols = c_ref[0]
        tm = cols.shape[0]

        @pl.when(pl.program_id(1) == 0)
        def _():
            carry_ref[...] = prev_ref[0]

        rolled = pltpu.roll(cols, 1, 0)
        first = lax.broadcasted_iota(jnp.int32, cols.shape, 0) == 0
        shifted = jnp.where(first, carry_ref[...], rolled)
        carry_ref[...] = cols[tm - 1:tm, :]
        last_o[0] = cols[tm - 1:tm, :]
    else:
        cols = c_ref[...]
        shifted = sh_ref[...]
    xs = cols + (shifted - cols) * mu_ref[...]
    r = xs[:, 0:D_B]
    k = xs[:, D_B:2 * D_B]
    v = xs[:, 2 * D_B:3 * D_B]
    o = 3 * D_B
    wl = xs[:, o:o + LW_PAD]
    al = xs[:, o + LW_PAD:o + LW_PAD + LA_PAD]
    gl = xs[:, o + LW_PAD + LA_PAD:o + LW_PAD + LA_PAD + LG_PAD]
    z = -(w0_ref[...] + jnp.dot(jnp.tanh(wl).astype(BF16), w2_ref[...], preferred_element_type=F32))
    softplus = jnp.maximum(z, 0.0) + jnp.log1p(jnp.exp(-jnp.abs(z)))
    log_decay = -jnp.exp(-softplus - 0.5)
    decay = log_decay if seq else jnp.exp(log_decay)
    a = jax.nn.sigmoid(a0_ref[...] + jnp.dot(al.astype(BF16), a2_ref[...], preferred_element_type=F32))
    g = jnp.dot(jax.nn.sigmoid(gl).astype(BF16), g2_ref[...], preferred_element_type=F32)
    bd = bd_ref[...]
    kk = k * kk_ref[...]
    kk = kk / jnp.maximum(jnp.sqrt(_seg_sum(kk * kk, bd)), 1e-12)
    k2 = k * (1.0 + (a - 1.0) * ka_ref[...])
    if seq:
        r_o[0], w_o[0], k_o[0], v_o[0], a_o[0], b_o[0], g_o[0] = r, decay, k2, v, -kk, kk * a, g
        rkv_o[0] = _seg_sum(r * k2 * rk_ref[...], bd) * v
    else:
        r_o[...], w_o[...], k_o[...], v_o[...], a_o[...], b_o[...], g_o[...] = r, decay, k2, v, -kk, kk * a, g
        rkv_o[...] = _seg_sum(r * k2 * rk_ref[...], bd) * v


def _rwkv_pre(cols, shift_src, params, *, seq, tm):
    prm_specs = []
    for p in params:
        prm_specs.append(pl.BlockSpec(p.shape, (lambda b, i: (0, 0)) if seq else (lambda i: (0, 0))))
    if seq:
        nb, t, _ = cols.shape
        grid = (nb, t // tm)
        in_specs = [pl.BlockSpec((1, tm, N_B_PAD), lambda b, i: (b, i, 0)),
                    pl.BlockSpec((1, 1, N_B_PAD), lambda b, i: (b, 0, 0))] + prm_specs
        ospec = pl.BlockSpec((1, tm, D_B), lambda b, i: (b, i, 0))
        oshape = jax.ShapeDtypeStruct((nb, t, D_B), F32)
        out_specs = [ospec] * 8 + [pl.BlockSpec((1, 1, N_B_PAD), lambda b, i: (b, 0, 0))]
        out_shape = [oshape] * 8 + [jax.ShapeDtypeStruct((nb, 1, N_B_PAD), F32)]
        scratch = [pltpu.VMEM((1, N_B_PAD), F32)]
        sem = ("parallel", "arbitrary")
    else:
        m = cols.shape[0]
        grid = (m // tm,)
        in_specs = [pl.BlockSpec((tm, N_B_PAD), lambda i: (i, 0))] * 2 + prm_specs
        out_specs = [pl.BlockSpec((tm, D_B), lambda i: (i, 0))] * 8
        out_shape = [jax.ShapeDtypeStruct((m, D_B), F32)] * 8
        scratch = []
        sem = ("parallel",)
    return pl.pallas_call(
        functools.partial(_rwkv_pre_kernel, seq=seq),
        grid=grid, in_specs=in_specs, out_specs=out_specs, out_shape=out_shape,
        scratch_shapes=scratch, compiler_params=_cparams(*sem),
        name="rwkv_pre_seq" if seq else "rwkv_pre_rows",
    )(cols, shift_src, *params)


def _scan_pair_step(S, a, w, b, k, v, r):
    lane = lax.broadcasted_iota(jnp.int32, (N_B, LANES), 1)
    row = lax.broadcasted_iota(jnp.int32, (N_B, LANES), 0)
    lo = lane < N_B
    e1 = lane == row
    e2 = lane == row + N_B

    def half_sums(p):
        t1 = jnp.sum(jnp.where(lo, p, 0.0), axis=1, keepdims=True)
        t2 = jnp.sum(jnp.where(lo, 0.0, p), axis=1, keepdims=True)
        return t1, t2

    t1, t2 = half_sums(S * a)
    sa = jnp.where(lo, t1, t2)
    vb = jnp.where(lo, jnp.sum(jnp.where(e1, v, 0.0), axis=1, keepdims=True),
                   jnp.sum(jnp.where(e2, v, 0.0), axis=1, keepdims=True))
    S = S * w + sa * b + vb * k
    y1, y2 = half_sums(S * r)
    y = jnp.sum(jnp.where(e1, y1, 0.0) + jnp.where(e2, y2, 0.0), axis=0, keepdims=True)
    return S, y


def _split(x):
    hi = x.astype(BF16)
    return hi, (x - hi.astype(F32)).astype(BF16)


def _mm3(xs, ys, nt=False):
    lhs = jnp.concatenate([xs[0], xs[0], xs[1]], axis=1)
    if nt:
        rhs = jnp.concatenate([ys[0], ys[1], ys[0]], axis=1)
        return lax.dot_general(lhs, rhs, (((1,), (1,)), ((), ())), preferred_element_type=F32)
    rhs = jnp.concatenate([ys[0], ys[1], ys[0]], axis=0)
    return jnp.dot(lhs, rhs, preferred_element_type=F32)


def _mm1(x, y, nt=False):
    if nt:
        return lax.dot_general(x, y, (((1,), (1,)), ((), ())), preferred_element_type=F32)
    return jnp.dot(x, y, preferred_element_type=F32)


def _block_diag(x):
    lo = lax.broadcasted_iota(jnp.int32, x.shape, 1) < N_B
    zero = jnp.zeros_like(x)
    return jnp.concatenate([jnp.where(lo, x, zero), jnp.where(lo, zero, x)], axis=0)


def _each(fn, *lists):
    return [fn(*args) for args in zip(*lists)]


def _rwkv_chunk_pairs(zs, lws, a_s, bs, ks, vs, rs):
    c = lws[0].shape[0]
    tri = (lax.broadcasted_iota(jnp.int32, (c, c), 0) >= lax.broadcasted_iota(jnp.int32, (c, c), 1)).astype(BF16)
    tri3 = jnp.concatenate([tri, tri, tri], axis=1)

    def cumsum(lw):
        l1 = lw.astype(BF16)
        r1 = lw - l1.astype(F32)
        l2 = r1.astype(BF16)
        l3 = (r1 - l2.astype(F32)).astype(BF16)
        return jnp.dot(tri3, jnp.concatenate([l1, l2, l3], axis=0), preferred_element_type=F32)

    lam = _each(cumsum, lws)
    lam_c = [x[c - 1:c, :] for x in lam]
    w_inv = [jnp.exp(-x) for x in lam]
    w_rem = _each(lambda lc, x: jnp.exp(lc - x), lam_c, lam)
    r_h = _each(lambda r, x: _block_diag(r * jnp.exp(x)), rs, lam)
    s_a = _each(lambda a, x, lw: _split(_block_diag(a * jnp.exp(x - lw))), a_s, lam, lws)
    h_r = [x.astype(BF16) for x in r_h]
    s_b = _each(lambda b, wi: _split(_block_diag(b * wi)), bs, w_inv)
    s_k = _each(lambda k, wi: _split(_block_diag(k * wi)), ks, w_inv)
    s_v = [_split(_block_diag(v)) for v in vs]
    n = 2 * c
    ri = lax.broadcasted_iota(jnp.int32, (n, n), 0)
    ci = lax.broadcasted_iota(jnp.int32, (n, n), 1)
    strict, incl, eye = ci < ri, ci <= ri, ci == ri
    gram = lambda x, y: jnp.where(strict, _mm3(x, y, nt=True), 0.0)
    lab = _each(gram, s_a, s_b)
    lak = _each(gram, s_a, s_k)
    h_lrb = _each(lambda x, y: jnp.where(incl, _mm1(x, y[0], nt=True), 0.0).astype(BF16), h_r, s_b)
    h_lrk = _each(lambda x, y: jnp.where(incl, _mm1(x, y[0], nt=True), 0.0).astype(BF16), h_r, s_k)
    t = [jnp.where(eye, 1.0, 0.0) + x for x in lab]
    s_p = _each(_split, lab)
    for _ in range(int(math.log2(c)) - 1):
        s_p = _each(lambda p: _split(_mm3(p, p)), s_p)
        t = _each(lambda tt, p: tt + _mm3(_split(tt), p), t, s_p)
    s_t = _each(_split, t)
    s_at = _each(lambda tt, a: _split(_mm3(tt, a)), s_t, s_a)
    x2 = _each(lambda l, v: _split(_mm3(_split(l), v)), lak, s_v)
    s_u0 = _each(lambda tt, x: _split(_mm3(tt, x)), s_t, x2)
    rt = _each(lambda rh, l, at: rh + _mm1(l, at[0]), r_h, h_lrb, s_at)
    y0 = _each(lambda l, u, lk, v: _mm1(l, u[0]) + _mm1(lk, v[0]), h_lrb, s_u0, h_lrk, s_v)
    s_bt = _each(lambda b, wr: _split(_block_diag(b * wr).T), bs, w_rem)
    s_kt = _each(lambda k, wr: _split(_block_diag(k * wr).T), ks, w_rem)
    m = _each(lambda lc, bt, at: jnp.where(eye, jnp.exp(lc), 0.0) + _mm3(bt, at), lam_c, s_bt, s_at)
    nn = _each(lambda bt, u, kt, v: _mm3(bt, u) + _mm3(kt, v), s_bt, s_u0, s_kt, s_v)
    s_z = _each(_split, zs)
    ybd = _each(lambda r, z, y: _mm1(r.astype(BF16), z[0]) + y, rt, s_z, y0)
    z_new = _each(lambda mm, z, x: _mm3(_split(mm), z) + x, m, s_z, nn)
    return z_new, [y[:c] + y[c:] for y in ybd]


def _rwkv_scan_seq_kernel(r_ref, w_ref, k_ref, v_ref, a_ref, b_ref, y_ref, sf_ref, z_ref, *, nb):
    npair = H_B // 2
    ci = pl.program_id(0)

    @pl.when(ci == 0)
    def _():
        z_ref[...] = jnp.zeros_like(z_ref)

    sls = [(bi, slice(None), slice(p * LANES, (p + 1) * LANES)) for bi in range(nb) for p in range(npair)]
    rows = lambda ref: [ref[sl] for sl in sls]
    zs, ys = _rwkv_chunk_pairs([z_ref[c] for c in range(len(sls))], rows(w_ref), rows(a_ref),
                               rows(b_ref), rows(k_ref), rows(v_ref), rows(r_ref))
    for c, sl in enumerate(sls):
        z_ref[c] = zs[c]
        y_ref[sl] = ys[c]

    @pl.when(ci == pl.num_programs(0) - 1)
    def _():
        for c, (bi, _, _) in enumerate(sls):
            st = z_ref[c].T
            sf_ref[bi, 2 * (c % npair)] = st[:N_B, :N_B]
            sf_ref[bi, 2 * (c % npair) + 1] = st[N_B:, N_B:]


def _rwkv_scan_seq(r, lw, k, v, a, b, *, tc):
    nb, t, _ = r.shape
    spec = pl.BlockSpec((nb, tc, D_B), lambda c: (0, c, 0))
    return pl.pallas_call(
        functools.partial(_rwkv_scan_seq_kernel, nb=nb),
        grid=(t // tc,),
        in_specs=[spec] * 6,
        out_specs=[spec, pl.BlockSpec((nb, H_B, N_B, N_B), lambda c: (0, 0, 0, 0))],
        out_shape=[jax.ShapeDtypeStruct((nb, t, D_B), F32), jax.ShapeDtypeStruct((nb, H_B, N_B, N_B), F32)],
        scratch_shapes=[pltpu.VMEM((nb * H_B // 2, LANES, LANES), F32)],
        compiler_params=_cparams("arbitrary"),
        name="rwkv_scan_seq",
    )(r, lw, k, v, a, b)


def _rwkv_scan_rows_kernel(r_ref, w_ref, k_ref, v_ref, a_ref, b_ref, s0_ref, y_ref, sf_ref, *, tb):
    def group(gi, carry):
        base = pl.multiple_of(gi * SUBLANES, SUBLANES)
        for p in range(H_B // 2):
            sl = (pl.ds(base, SUBLANES), slice(p * LANES, (p + 1) * LANES))
            a, w, b, k, v, r = (ref[sl] for ref in (a_ref, w_ref, b_ref, k_ref, v_ref, r_ref))
            ys = []
            for j in range(SUBLANES):
                row = slice(j, j + 1)
                S0 = jnp.concatenate([s0_ref[base + j, 2 * p], s0_ref[base + j, 2 * p + 1]], axis=1)
                S, y = _scan_pair_step(S0, a[row], w[row], b[row], k[row], v[row], r[row])
                sf_ref[base + j, 2 * p] = S[:, :N_B]
                sf_ref[base + j, 2 * p + 1] = S[:, N_B:]
                ys.append(y)
            y_ref[sl] = jnp.concatenate(ys, axis=0)
        return carry

    lax.fori_loop(0, tb // SUBLANES, group, 0)


def _rwkv_scan_rows(r, w, k, v, a, b, s0, *, tb):
    m = r.shape[0]
    spec = pl.BlockSpec((tb, D_B), lambda i: (i, 0))
    sspec = pl.BlockSpec((tb, H_B, N_B, N_B), lambda i: (i, 0, 0, 0))
    return pl.pallas_call(
        functools.partial(_rwkv_scan_rows_kernel, tb=tb),
        grid=(m // tb,),
        in_specs=[spec] * 6 + [sspec],
        out_specs=[spec, sspec],
        out_shape=[jax.ShapeDtypeStruct((m, D_B), F32), jax.ShapeDtypeStruct((m, H_B, N_B, N_B), F32)],
        compiler_params=_cparams("parallel"),
        name="rwkv_scan_rows",
    )(r, w, k, v, a, b, s0)


def _rwkv_post_kernel(y_ref, rkv_ref, g_ref, lg_ref, lb_ref, bd_ref, o_ref):
    y = y_ref[...]
    bd = bd_ref[...]
    d = y - _seg_sum(y, bd) * (1.0 / N_B)
    var = _seg_sum(d * d, bd) * (1.0 / N_B)
    yn = d * lax.rsqrt(var + GN_EPS_B) * lg_ref[...] + lb_ref[...]
    o_ref[...] = ((yn + rkv_ref[...]) * g_ref[...]).astype(o_ref.dtype)


def _rwkv_post(y, rkv, g, lg, lb, bd, *, tm):
    m = y.shape[0]
    spec = pl.BlockSpec((tm, D_B), lambda i: (i, 0))
    row = pl.BlockSpec((1, D_B), lambda i: (0, 0))
    return pl.pallas_call(
        _rwkv_post_kernel,
        grid=(m // tm,),
        in_specs=[spec, spec, spec, row, row, pl.BlockSpec((D_B, D_B), lambda i: (0, 0))],
        out_specs=spec,
        out_shape=jax.ShapeDtypeStruct((m, D_B), BF16),
        compiler_params=_cparams("parallel"),
        name="rwkv_post",
    )(y, rkv, g, lg, lb, bd)


def _lam_full(lq1, lk1, lq2, lk2, lam_init):
    return (jnp.exp(jnp.sum(lq1 * lk1, axis=-1, keepdims=True))
            - jnp.exp(jnp.sum(lq2 * lk2, axis=-1, keepdims=True)) + lam_init)


ONES_ROWS = 16


def _attn_prompt_kernel(q_ref, k_ref, vt_ref, lq1, lk1, lq2, lk2, sg_ref, o_ref, m1, a1, m2, a2, s_a, s_b, *,
                        tk, lam_init):
    qi = pl.program_id(2)
    for m_ref, a_ref in ((m1, a1), (m2, a2)):
        m_ref[...] = jnp.full_like(m_ref, -jnp.inf)
        a_ref[...] = jnp.zeros_like(a_ref)

    q = q_ref[0]
    lane = lax.broadcasted_iota(jnp.int32, q.shape, 1)
    zero = jnp.zeros_like(q)
    q_sub = (jnp.where(lane < D_HEAD, q, zero), jnp.where(lane < D_HEAD, zero, q))
    ones = jnp.ones((ONES_ROWS, tk), BF16)

    units = [(qs[c0:c0 + ATTN_STRIP], m_ref, a_ref, c0)
             for qs, m_ref, a_ref in ((q_sub[0], m1, a1), (q_sub[1], m2, a2))
             for c0 in range(0, q.shape[0], ATTN_STRIP)]

    n_units = len(units)

    def scores(u, kis):
        return [lax.dot_general(k_ref[0, pl.ds(pl.multiple_of(ki * tk, tk), tk), :], units[u][0],
                                (((1,), (1,)), ((), ())), preferred_element_type=F32) for ki in kis]

    def stash(s_ref, u, blocks):
        for i, st in enumerate(blocks):
            s_ref[u, i] = st

    def unstash(s_ref, u):
        return [s_ref[u, i] for i in range(ATTN_GROUP)]

    def softmax_pv(u, kis, st_blocks, masked):
        _, m_ref, a_ref, c0 = units[u]
        cols = slice(c0, c0 + ATTN_STRIP)
        if masked:
            kr = lax.broadcasted_iota(jnp.int32, st_blocks[0].shape, 0)
            qc = lax.broadcasted_iota(jnp.int32, st_blocks[0].shape, 1) + c0
            st_blocks = [jnp.where(kr <= qc, st, -jnp.inf) for st in st_blocks]
        m_old = m_ref[:, cols]
        m_new = m_old
        for st in st_blocks:
            m_new = jnp.maximum(m_new, jnp.max(st, axis=0, keepdims=True))
        acc = jnp.exp2(m_old - m_new) * a_ref[:, cols]
        for st, ki in zip(st_blocks, kis):
            vt1 = jnp.concatenate([vt_ref[0, ki], ones], axis=0)
            acc = acc + jnp.dot(vt1, jnp.exp2(st - m_new).astype(BF16), preferred_element_type=F32)
        a_ref[:, cols] = acc
        m_ref[:, cols] = m_new

    def update(kis, masked):
        sts = [scores(u, kis) for u in range(n_units)]
        for u in range(n_units):
            softmax_pv(u, kis, sts[u], masked)

    def pipelined(cur_kis, cur_ref, next_kis, next_ref):
        for u in range(n_units):
            stash(next_ref, u, scores(u, next_kis))
            softmax_pv(u, cur_kis, unstash(cur_ref, u), False)

    group = lambda g: [ATTN_GROUP * g + i for i in range(ATTN_GROUP)]
    n_groups = lax.shift_right_logical(qi, ATTN_GROUP.bit_length() - 1)

    @pl.when(n_groups > 0)
    def _():
        for u in range(n_units):
            stash(s_a, u, scores(u, group(0)))

    def pair_body(jj, carry):
        g0 = 2 * jj
        pipelined(group(g0), s_a, group(g0 + 1), s_b)
        pipelined(group(g0 + 1), s_b, [jnp.minimum(ki, qi) for ki in group(g0 + 2)], s_a)
        return carry

    lax.fori_loop(0, lax.shift_right_logical(n_groups, 1), pair_body, 0)

    @pl.when((n_groups & 1) == 1)
    def _():
        for u in range(n_units):
            softmax_pv(u, group(n_groups - 1), unstash(s_a, u), False)

    def single_body(ki, carry):
        update([ki], False)
        return carry

    lax.fori_loop(n_groups * ATTN_GROUP, qi, single_body, 0)
    update([qi], True)
    lam = _lam_full(lq1[...], lk1[...], lq2[...], lk2[...], lam_init)
    d = 2 * D_HEAD
    ot = a1[:d, :] / a1[d:d + 1, :] - lam * (a2[:d, :] / a2[d:d + 1, :])
    on = ot * lax.rsqrt(jnp.mean(ot * ot, axis=0, keepdims=True) + SUBLN_EPS)
    o_ref[0] = (on.T * sg_ref[...] * (1.0 - lam_init)).astype(o_ref.dtype)


def _attn_prompt(q, k, vt, lams, subln_g, lam_init, *, tq):
    nb, t, _ = q.shape
    tk = vt.shape[3]
    assert tk == tq
    qspec = pl.BlockSpec((1, tq, LANES), lambda b, h, i: (b, i, h))
    kspec = pl.BlockSpec((1, t, LANES), lambda b, h, i: (b, 0, h))
    vspec = pl.BlockSpec((1, t // tk, LANES, tk), lambda b, h, i: (b, 0, h, 0))
    row = lambda n: pl.BlockSpec((1, n), lambda b, h, i: (0, 0))
    acc = 2 * D_HEAD + ONES_ROWS
    return pl.pallas_call(
        functools.partial(_attn_prompt_kernel, tk=tk, lam_init=lam_init),
        grid=(nb, H_C, t // tq),
        in_specs=[qspec, kspec, vspec, row(D_HEAD), row(D_HEAD), row(D_HEAD), row(D_HEAD), row(LANES)],
        out_specs=qspec,
        out_shape=jax.ShapeDtypeStruct(q.shape, BF16),
        scratch_shapes=[pltpu.VMEM((1, tq), F32), pltpu.VMEM((acc, tq), F32),
                        pltpu.VMEM((1, tq), F32), pltpu.VMEM((acc, tq), F32)]
                       + [pltpu.VMEM((2 * tq // ATTN_STRIP, ATTN_GROUP, tk, ATTN_STRIP), F32)] * 2,
        compiler_params=_cparams("parallel", "parallel", "arbitrary"),
        name="attn_prompt",
    )(q, k, vt, *lams, subln_g)


def _attn_sample_kernel(pt_ref, q_ref, kn_ref, vn_ref, *refs, npg, lam_init):
    k_refs = refs[:npg]
    v_refs = refs[npg:2 * npg]
    lq1, lk1, lq2, lk2, sg_ref, o_ref, m_ref, l_ref, acc_ref = refs[2 * npg:]
    g = pl.program_id(1)
    nrow, ncol = 2 * H_C, PAGE_SIZE * H_C
    q = q_ref[0]
    lo = lax.broadcasted_iota(jnp.int32, q.shape, 1) < D_HEAD
    qz = jnp.concatenate([jnp.where(lo, q, 0.0), jnp.where(lo, 0.0, q)], axis=0).astype(BF16)
    valid = ((lax.broadcasted_iota(jnp.int32, (nrow, ncol), 1) & (H_C - 1))
             == (lax.broadcasted_iota(jnp.int32, (nrow, ncol), 0) & (H_C - 1)))

    @pl.when(g == 0)
    def _():
        as_mxu = lambda ref: jnp.concatenate([ref[0], ref[0]], axis=0).astype(BF16).astype(F32)
        m_ref[...] = jnp.sum(qz.astype(F32) * as_mxu(kn_ref), axis=1, keepdims=True)
        l_ref[...] = jnp.ones_like(l_ref)
        acc_ref[...] = as_mxu(vn_ref)

    sts = []
    for r in range(npg):
        kmat = k_refs[r][...].reshape(ncol, LANES).astype(BF16)
        st = lax.dot_general(qz, kmat, (((1,), (1,)), ((), ())), preferred_element_type=F32)
        sts.append(jnp.where(valid, st, -jnp.inf))
    m_old = m_ref[...]
    m_new = m_old
    for st in sts:
        m_new = jnp.maximum(m_new, jnp.max(st, axis=1, keepdims=True))
    alpha = jnp.exp2(m_old - m_new)
    l = alpha * l_ref[...]
    acc = alpha * acc_ref[...]
    for st, v_ref in zip(sts, v_refs):
        p = jnp.exp2(st - m_new)
        l = l + jnp.sum(p, axis=1, keepdims=True)
        acc = acc + jnp.dot(p.astype(BF16), v_ref[...].reshape(ncol, LANES).astype(BF16),
                            preferred_element_type=F32)
    m_ref[...] = m_new
    l_ref[...] = l
    acc_ref[...] = acc

    @pl.when(g == pl.num_programs(1) - 1)
    def _():
        o = acc_ref[...] / l_ref[...]
        lam = _lam_full(lq1[...], lk1[...], lq2[...], lk2[...], lam_init)
        o_ref[0] = (_rms(o[:H_C] - lam * o[H_C:], sg_ref[...], SUBLN_EPS) * (1.0 - lam_init)).astype(o_ref.dtype)


def _attn_sample(q, k_new, v_new, cache_k, cache_v, page_table, layer, lams, subln_g, lam_init, *, npg):
    nb = q.shape[0]
    n_pages = page_table.shape[1]
    hspec = pl.BlockSpec((1, H_C, LANES), lambda b, g, pt: (b, 0, 0))
    row = lambda n: pl.BlockSpec((1, n), lambda b, g, pt: (0, 0))

    def page_spec(r):
        return pl.BlockSpec((None, None, PAGE_SIZE, H_C, LANES),
                            lambda b, g, pt: (layer, pt[b, g * npg + r], 0, 0, 0))

    grid_spec = pltpu.PrefetchScalarGridSpec(
        num_scalar_prefetch=1,
        grid=(nb, n_pages // npg),
        in_specs=[hspec, hspec, hspec] + [page_spec(r) for r in range(npg)] * 2
                 + [row(D_HEAD), row(D_HEAD), row(D_HEAD), row(D_HEAD), row(LANES)],
        out_specs=hspec,
        scratch_shapes=[pltpu.VMEM((2 * H_C, 1), F32), pltpu.VMEM((2 * H_C, 1), F32),
                        pltpu.VMEM((2 * H_C, LANES), F32)],
    )
    return pl.pallas_call(
        functools.partial(_attn_sample_kernel, npg=npg, lam_init=lam_init),
        grid_spec=grid_spec,
        out_shape=jax.ShapeDtypeStruct((nb, H_C, LANES), BF16),
        compiler_params=_cparams("parallel", "arbitrary"),
        name="attn_sample",
    )(page_table, q, k_new, v_new, *([cache_k] * npg), *([cache_v] * npg), *lams, subln_g)


def _merge_kernel(x_ref, oa_ref, ob_ref, oc_ref, cg_ref, wa_ref, wb_ref, wc_ref, wo_ref, xo_ref):
    gates = cg_ref[...].astype(F32)
    dot = lambda a, w: jnp.dot(a, w, preferred_element_type=F32)
    m = (gates[:, 0:D_MODEL] * dot(oa_ref[...], wa_ref[...])
         + gates[:, D_MODEL:2 * D_MODEL] * dot(ob_ref[...], wb_ref[...])
         + gates[:, 2 * D_MODEL:3 * D_MODEL] * dot(oc_ref[...], wc_ref[...]))
    xo_ref[...] = x_ref[...] + dot(m.astype(BF16), wo_ref[...])


def _merge(x, oa, ob, oc, cg, wa, wb, wc, wo, *, tm):
    m = x.shape[0]
    rows = lambda n: pl.BlockSpec((tm, n), lambda i: (i, 0))
    full = lambda a: pl.BlockSpec(a.shape, lambda i: (0, 0))
    return pl.pallas_call(
        _merge_kernel,
        grid=(m // tm,),
        in_specs=[rows(D_MODEL), rows(D_A), rows(D_B), rows(D_C), rows(3 * D_MODEL),
                  full(wa), full(wb), full(wc), full(wo)],
        out_specs=rows(D_MODEL),
        out_shape=jax.ShapeDtypeStruct((m, D_MODEL), F32),
        compiler_params=_cparams("parallel"),
        name="merge",
    )(x, oa, ob, oc, cg, wa, wb, wc, wo)


def _ffn_kernel(x_ref, g_ref, wu_ref, wd_ref, gf_ref, o_ref, h_ref, acc_ref, *, final):
    j = pl.program_id(1)

    @pl.when(j == 0)
    def _():
        h_ref[...] = _rms(x_ref[...], g_ref[...], EPS).astype(BF16)
        acc_ref[...] = jnp.zeros_like(acc_ref)

    up = jnp.dot(h_ref[...], wu_ref[...], preferred_element_type=F32)
    act = jnp.square(jnp.maximum(up, 0.0)).astype(BF16)
    acc_ref[...] += jnp.dot(act, wd_ref[...], preferred_element_type=F32)

    @pl.when(j == pl.num_programs(1) - 1)
    def _():
        xn = x_ref[...] + acc_ref[...]
        o_ref[...] = _rms(xn, gf_ref[...], EPS) if final else xn


def _ffn(x, g, wu, wd, gf, *, final, tm, tf):
    m = x.shape[0]
    row = pl.BlockSpec((1, D_MODEL), lambda i, j: (0, 0))
    return pl.pallas_call(
        functools.partial(_ffn_kernel, final=final),
        grid=(m // tm, D_FF // tf),
        in_specs=[pl.BlockSpec((tm, D_MODEL), lambda i, j: (i, 0)), row,
                  pl.BlockSpec((D_MODEL, tf), lambda i, j: (0, j)),
                  pl.BlockSpec((tf, D_MODEL), lambda i, j: (j, 0)), row],
        out_specs=pl.BlockSpec((tm, D_MODEL), lambda i, j: (i, 0)),
        out_shape=jax.ShapeDtypeStruct((m, D_MODEL), F32),
        scratch_shapes=[pltpu.VMEM((tm, D_MODEL), BF16), pltpu.VMEM((tm, D_MODEL), F32)],
        compiler_params=_cparams("parallel", "arbitrary"),
        name="ffn",
    )(x, g, wu, wd, gf)


def _pad_b_cols(a):
    o = 3 * D_B
    z = lambda n: jnp.zeros(a.shape[:-1] + (n,), a.dtype)
    return jnp.concatenate([a[..., :o + LORA_W], z(LW_PAD - LORA_W),
                            a[..., o + LORA_W:o + LORA_W + LORA_A], z(LA_PAD - LORA_A),
                            a[..., o + LORA_W + LORA_A:], z(LG_PAD - LORA_G)], axis=-1)


def _unpad_b_cols(a):
    o = 3 * D_B
    return jnp.concatenate([a[..., :o + LORA_W], a[..., o + LW_PAD:o + LW_PAD + LORA_A],
                            a[..., o + LW_PAD + LA_PAD:o + LW_PAD + LA_PAD + LORA_G]], axis=-1)


def _pad_rows(a, n):
    return jnp.concatenate([a, jnp.zeros((n - a.shape[0],) + a.shape[1:], a.dtype)], axis=0)


def _ones_blockdiag(n, seg):
    i = np.arange(n) // seg
    return jnp.asarray((i[:, None] == i[None, :]).astype(np.float32), dtype=BF16)


def _layer_weights(l, p):
    w_in = p["w_in"][l]
    o_b = N_A_COLS
    o_q = o_b + N_B_COLS
    w = dict(
        norm_mix=p["norm_mix"][l].reshape(1, D_MODEL),
        w_a=w_in[:, :o_b].astype(BF16),
        w_b=_pad_b_cols(w_in[:, o_b:o_q]).astype(BF16),
        w_q=w_in[:, o_q:o_q + D_C].astype(BF16),
        w_k=w_in[:, o_q + D_C:o_q + 2 * D_C].astype(BF16),
        w_v=w_in[:, o_q + 2 * D_C:o_q + 3 * D_C].astype(BF16),
        w_g=w_in[:, o_q + 3 * D_C:].astype(BF16),
        lng=p["sgu_ln_g"][l].reshape(1, D_A), lnb=p["sgu_ln_b"][l].reshape(1, D_A),
        sgu_w=p["sgu_w"][l],
        sgu_bias_td=jnp.repeat(p["sgu_b"][l].T, D_A // G_A, axis=1),
        sgu_w0=jnp.repeat(p["sgu_w"][l][:, 0, 0], D_A // G_A).reshape(1, D_A),
        sgu_b0=jnp.repeat(p["sgu_b"][l][:, 0], D_A // G_A).reshape(1, D_A),
        rwkv=[_pad_b_cols(p["shift_mu"][l]).reshape(1, N_B_PAD),
              p["w0"][l].reshape(1, D_B), _pad_rows(p["w2"][l], LW_PAD).astype(BF16),
              p["a0"][l].reshape(1, D_B), _pad_rows(p["a2"][l], LA_PAD).astype(BF16),
              _pad_rows(p["g2"][l], LG_PAD).astype(BF16),
              p["k_k"][l].reshape(1, D_B), p["k_a"][l].reshape(1, D_B), p["r_k"][l].reshape(1, D_B),
              _ones_blockdiag(D_B, N_B)],
        lnx_g=p["lnx_g"][l].reshape(1, D_B), lnx_b=p["lnx_b"][l].reshape(1, D_B),
        lams=[p[n][l].reshape(1, D_HEAD) for n in ("lam_q1", "lam_k1", "lam_q2", "lam_k2")],
        subln_g=p["subln_g"][l].reshape(1, 2 * D_HEAD),
        w_br_a=p["w_br_a"][l].astype(BF16), w_br_b=p["w_br_b"][l].astype(BF16),
        w_br_c=p["w_br_c"][l].astype(BF16), w_out=p["w_out"][l].astype(BF16),
        norm_ffn=p["norm_ffn"][l].reshape(1, D_MODEL),
        w_up=p["w_up"][l].astype(BF16), w_down=p["w_down"][l].astype(BF16),
        norm_final=p["norm_final"].reshape(1, D_MODEL),
    )
    return w


def _project_all(x, w, rope_tabs, tm, t_seq=None, kv_stack=None):
    pj = functools.partial(_proj, x, w["norm_mix"], tm=tm, tn=D_MODEL)
    qscale = math.log2(math.e) / math.sqrt(D_HEAD)
    (ca,) = pj(w["w_a"], out_dtypes=(F32,), name="proj_a")
    (cb,) = pj(w["w_b"], out_dtypes=(F32,), name="proj_b")
    (cg,) = pj(w["w_g"], out_dtypes=(BF16,), gate=True, name="proj_g")
    if t_seq is None:
        (q,) = pj(w["w_q"], out_dtypes=(F32,), rope_tabs=rope_tabs, scale=qscale, name="proj_q")
        (kf,) = pj(w["w_k"], out_dtypes=(F32,), rope_tabs=rope_tabs, name="proj_k")
        (vf,) = pj(w["w_v"], out_dtypes=(F32,), name="proj_v")
        return ca, cb, cg, q, kf, vf
    (q,) = pj(w["w_q"], out_dtypes=(BF16,), rope_tabs=rope_tabs, scale=qscale, name="proj_q")
    layer, k_buf, v_buf = kv_stack
    kf, kb = pj(w["w_k"], out_dtypes=(F32, BF16), rope_tabs=rope_tabs, stack=(layer, k_buf), name="proj_k")
    vf, vt = _proj(x, w["norm_mix"], w["w_v"], tm=ATTN_BLOCK, tn=D_MODEL, out_dtypes=(F32, BF16), t_seq=t_seq,
                   stack=(layer, v_buf), name="proj_v")
    return ca, cb, cg, q, kf, vf, kb, vt


def _prompt_layer(x, l, w, rope_tabs, nb, t, final, kv_stack):
    lam_init = 0.8 - 0.6 * math.exp(-0.3 * l)
    ca, cb, cg, qb, kf, vf, kb, vt = _project_all(x, w, rope_tabs, 1024, t_seq=t, kv_stack=kv_stack)
    o_a = _sgu_prompt(ca, w["lng"], w["lnb"], w["sgu_w"], w["sgu_bias_td"], tm=512)
    prev0 = jnp.zeros((nb, 1, N_B_PAD), F32)
    *rw, last = _rwkv_pre(cb.reshape(nb, t, N_B_PAD), prev0, w["rwkv"], seq=True, tm=256)
    r, dec, k2, v, na, bb, g, rkv = rw
    y, s_fin = _rwkv_scan_seq(r, dec, k2, v, na, bb, tc=64)
    flat = lambda a: a.reshape(nb * t, D_B)
    o_b = _rwkv_post(flat(y), flat(rkv), flat(g), w["lnx_g"], w["lnx_b"], w["rwkv"][-1], tm=1024)
    b3 = lambda a: a.reshape(nb, t, D_C)
    o_c = _attn_prompt(b3(qb), b3(kb), vt, w["lams"], w["subln_g"], lam_init, tq=ATTN_BLOCK)
    x = _merge(x, o_a, o_b, o_c.reshape(nb * t, D_C), cg, w["w_br_a"], w["w_br_b"], w["w_br_c"], w["w_out"], tm=512)
    x = _ffn(x, w["norm_ffn"], w["w_up"], w["w_down"], w["norm_final"], final=final, tm=1024, tf=1024)
    return x, kf, vf, s_fin, _unpad_b_cols(last[:, 0, :])


def _sample_layer(x, l, w, rope_tabs, cache_k, cache_v, page_table, prev_shift, s0, final):
    nb = x.shape[0]
    lam_init = 0.8 - 0.6 * math.exp(-0.3 * l)
    ca, cb, cg, qb, kf, vf = _project_all(x, w, rope_tabs, nb)
    o_a, va = _sgu_sample(ca, w["lng"], w["lnb"], w["sgu_w0"], w["sgu_b0"])
    r, dec, k2, v, na, bb, g, rkv = _rwkv_pre(cb, _pad_b_cols(prev_shift), w["rwkv"], seq=False, tm=nb)
    y, s_new = _rwkv_scan_rows(r, dec, k2, v, na, bb, s0, tb=32)
    o_b = _rwkv_post(y, rkv, g, w["lnx_g"], w["lnx_b"], w["rwkv"][-1], tm=nb)
    h3 = lambda a: a.reshape(nb, H_C, LANES)
    o_c = _attn_sample(h3(qb), h3(kf), h3(vf), cache_k, cache_v, page_table, l,
                       w["lams"], w["subln_g"], lam_init, npg=8)
    x = _merge(x, o_a, o_b, o_c.reshape(nb, D_C), cg, w["w_br_a"], w["w_br_b"], w["w_br_c"], w["w_out"], tm=nb)
    x = _ffn(x, w["norm_ffn"], w["w_up"], w["w_down"], w["norm_final"], final=final, tm=nb, tf=1024)
    return x, kf, vf, s_new, _unpad_b_cols(cb), va


def kernel(x_prompt, x_sample, cache_k, cache_v, state_rwkv, state_shift, page_table, norm_mix, w_in, sgu_ln_g, sgu_ln_b, sgu_w, sgu_b, shift_mu, w0, w2, a0, a2, g2, k_k, k_a, r_k, lnx_g, lnx_b, lam_q1, lam_k1, lam_q2, lam_k2, subln_g, w_br_a, w_br_b, w_br_c, w_out, norm_ffn, w_up, w_down, norm_final):
    p = dict(norm_mix=norm_mix, w_in=w_in, sgu_ln_g=sgu_ln_g, sgu_ln_b=sgu_ln_b, sgu_w=sgu_w, sgu_b=sgu_b,
             shift_mu=shift_mu, w0=w0, w2=w2, a0=a0, a2=a2, g2=g2, k_k=k_k, k_a=k_a, r_k=r_k, lnx_g=lnx_g,
             lnx_b=lnx_b, lam_q1=lam_q1, lam_k1=lam_k1, lam_q2=lam_q2, lam_k2=lam_k2, subln_g=subln_g,
             w_br_a=w_br_a, w_br_b=w_br_b, w_br_c=w_br_c, w_out=w_out, norm_ffn=norm_ffn, w_up=w_up,
             w_down=w_down, norm_final=norm_final)
    depth = w_in.shape[0]
    bp, tp, _ = x_prompt.shape
    bs, ts, _ = x_sample.shape
    tabs_p = _rope_tables(jnp.arange(tp))
    tabs_s = _rope_tables(jnp.full((bs * ts,), PAST_LEN, jnp.int32))
    xp = x_prompt.reshape(bp * tp, D_MODEL)
    xs = x_sample.reshape(bs * ts, D_MODEL)
    outs_p, outs_s = [], []
    kp = jnp.zeros((depth * bp * tp, D_C), F32)
    vp = jnp.zeros((depth * bp * tp, D_C), F32)
    for l in range(depth):
        w = _layer_weights(l, p)
        final = l == depth - 1
        xp, kp, vp, sp, shp = _prompt_layer(xp, l, w, tabs_p, bp, tp, final, (l, kp, vp))
        xs, ks, vs, ss, shs, va = _sample_layer(xs, l, w, tabs_s, cache_k, cache_v, page_table,
                                                state_shift[l], state_rwkv[l], final)
        outs_p.append((sp, shp))
        outs_s.append((ks.reshape(bs, ts, H_C, 2 * D_HEAD), vs.reshape(bs, ts, H_C, 2 * D_HEAD), ss, shs,
                       va.reshape(bs, ts, D_A)))
    stack = lambda outs, i: jnp.stack([o[i] for o in outs])
    return (xp.reshape(bp, tp, D_MODEL), xs.reshape(bs, ts, D_MODEL),
            kp.reshape(depth, bp, tp, H_C, 2 * D_HEAD), vp.reshape(depth, bp, tp, H_C, 2 * D_HEAD),
            stack(outs_p, 0), stack(outs_p, 1),
            stack(outs_s, 0), stack(outs_s, 1), stack(outs_s, 2), stack(outs_s, 3), stack(outs_s, 4))
```

```python
import functools
import math

import numpy as np
import jax
import jax.numpy as jnp
from jax import lax
from jax.experimental import pallas as pl
from jax.experimental.pallas import tpu as pltpu

F32 = jnp.float32
BF16 = jnp.bfloat16

D_MODEL = 1024
PAST_LEN = 2048
PAGE_SIZE = 128
CHUNK = 128
D_A = 512
G_A = 8
H_B = 8
N_B = 64
D_B = H_B * N_B
LORA_W = 64
LORA_A = 64
LORA_G = 160
GN_EPS_B = 64e-5
H_C = 8
D_HEAD = 64
D_C = H_C * 2 * D_HEAD
ROPE_DIM = D_HEAD // 4
ROPE_THETA = 500000.0
SUBLN_EPS = 1e-5
D_FF = 4 * D_MODEL
EPS = 1e-6
N_A_COLS = 2 * D_A
N_B_COLS = 3 * D_B + LORA_W + LORA_A + LORA_G
LANES = 128
SUBLANES = 8
LW_PAD = 128
LA_PAD = 128
LG_PAD = 256
N_B_PAD = 3 * D_B + LW_PAD + LA_PAD + LG_PAD
VMEM_LIMIT = 52 * 1024 * 1024
ATTN_BLOCK = 512
ATTN_GROUP = 2
ATTN_STRIP = 256


def _cparams(*sem):
    return pltpu.CompilerParams(dimension_semantics=sem, vmem_limit_bytes=VMEM_LIMIT)


def _rms(x, g, eps):
    return x * lax.rsqrt(jnp.mean(x * x, axis=-1, keepdims=True) + eps) * g


def _proj_kernel(*refs, rope, scale, n_out, aliased, gate):
    x_ref, g_ref, w_ref = refs[:3]
    pos = 3
    if rope:
        c_ref, s1_ref, s2_ref = refs[3:6]
        pos = 6
    pos += aliased
    out_refs = refs[pos:pos + n_out]
    h_ref = refs[pos + n_out]

    @pl.when(pl.program_id(1) == 0)
    def _():
        h_ref[...] = _rms(x_ref[...], g_ref[...], EPS).astype(BF16)

    acc = jnp.dot(h_ref[...], w_ref[...], preferred_element_type=F32)
    if rope:
        c, s1, s2 = c_ref[...], s1_ref[...], s2_ref[...]
        parts = []
        for hh in range(acc.shape[1] // LANES):
            a = acc[:, hh * LANES:(hh + 1) * LANES]
            parts.append(a * c + pltpu.roll(a, ROPE_DIM // 2, 1) * s1
                         + pltpu.roll(a, LANES - ROPE_DIM // 2, 1) * s2)
        acc = jnp.concatenate(parts, axis=1)
    if scale != 1.0:
        acc = acc * scale
    if gate:
        acc = jax.nn.sigmoid(acc)
    for o_ref in out_refs:
        if len(o_ref.shape) == 4:
            o_ref[0, 0] = acc.T.astype(o_ref.dtype)
        else:
            o_ref[...] = acc.astype(o_ref.dtype)


def _proj(x, g, w, *, tm, tn, out_dtypes, rope_tabs=None, scale=1.0, t_seq=None, stack=None, gate=False,
          name="proj"):
    m, d = x.shape
    n = w.shape[1]
    in_specs = [pl.BlockSpec((tm, d), lambda i, j: (i, 0)),
                pl.BlockSpec((1, d), lambda i, j: (0, 0)),
                pl.BlockSpec((d, tn), lambda i, j: (0, j))]
    args = [x, g, w]
    if rope_tabs is not None:
        nt = rope_tabs[0].shape[0] // tm
        in_specs += [pl.BlockSpec((tm, LANES), lambda i, j: (i % nt, 0))] * 3
        args += list(rope_tabs)
    out_specs = [pl.BlockSpec((tm, tn), lambda i, j: (i, j)) for _ in out_dtypes]
    out_shape = [jax.ShapeDtypeStruct((m, n), dt) for dt in out_dtypes]
    if t_seq is not None:
        nts = t_seq // tm
        out_specs[-1] = pl.BlockSpec((1, 1, tn, tm), lambda i, j: (i // nts, i % nts, j, 0))
        out_shape[-1] = jax.ShapeDtypeStruct((m // t_seq, nts, n, tm), out_dtypes[-1])
    aliases = {}
    if stack is not None:
        layer, buf = stack
        off = layer * (m // tm)
        out_specs[0] = pl.BlockSpec((tm, tn), lambda i, j: (i + off, j))
        out_shape[0] = jax.ShapeDtypeStruct(buf.shape, out_dtypes[0])
        aliases = {len(args): 0}
        in_specs.append(pl.BlockSpec(memory_space=pl.ANY))
        args.append(buf)
    outs = pl.pallas_call(
        functools.partial(_proj_kernel, rope=rope_tabs is not None, scale=scale, n_out=len(out_dtypes),
                          aliased=len(aliases), gate=gate),
        grid=(m // tm, n // tn),
        in_specs=in_specs,
        out_specs=out_specs,
        out_shape=out_shape,
        input_output_aliases=aliases,
        scratch_shapes=[pltpu.VMEM((tm, d), BF16)],
        compiler_params=_cparams("parallel", "arbitrary"),
        name=name,
    )(*args)
    return outs


def _rope_kernel(pos_ref, invf_ref, c_ref, s1_ref, s2_ref):
    ang = pos_ref[...] * invf_ref[...]
    lane = lax.broadcasted_iota(jnp.int32, ang.shape, 1) % D_HEAD
    first = lane < ROPE_DIM // 2
    second = (lane >= ROPE_DIM // 2) & (lane < ROPE_DIM)
    cos, sin = jnp.cos(ang), jnp.sin(ang)
    c_ref[...] = jnp.where(first | second, cos, 1.0)
    s1_ref[...] = jnp.where(second, sin, 0.0)
    s2_ref[...] = jnp.where(first, -sin, 0.0)


def _rope_tables(pos):
    t = pos.shape[0]
    half = ROPE_DIM // 2
    inv_freq = ROPE_THETA ** (-jnp.arange(half, dtype=F32) / half)
    blk = jnp.concatenate([inv_freq, inv_freq, jnp.zeros((D_HEAD - ROPE_DIM,), F32)])
    invf = jnp.concatenate([blk, blk]).reshape(1, LANES)
    tm = min(t, 512)
    return pl.pallas_call(
        _rope_kernel,
        grid=(t // tm,),
        in_specs=[pl.BlockSpec((tm, 1), lambda i: (i, 0)), pl.BlockSpec((1, LANES), lambda i: (0, 0))],
        out_specs=[pl.BlockSpec((tm, LANES), lambda i: (i, 0))] * 3,
        out_shape=[jax.ShapeDtypeStruct((t, LANES), F32)] * 3,
        compiler_params=_cparams("parallel"),
        name="rope_tables",
    )(pos.astype(F32).reshape(t, 1), invf)


def _gelu_ln(ca, lng, lnb):
    gx = 0.5 * ca * (1.0 + lax.erf(ca * math.sqrt(0.5)))
    u, v = gx[:, :D_A], gx[:, D_A:]
    d = v - jnp.mean(v, axis=-1, keepdims=True)
    va = d * lax.rsqrt(jnp.mean(d * d, axis=-1, keepdims=True) + EPS) * lng + lnb
    return u, va


def _sgu_prompt_kernel(ca_ref, lng_ref, lnb_ref, w_ref, b_ref, oa_ref, *, nchunk):
    u, va = _gelu_ln(ca_ref[...], lng_ref[...], lnb_ref[...])
    vab = va.astype(BF16)
    row = lax.broadcasted_iota(jnp.int32, (CHUNK, CHUNK), 0)
    col = lax.broadcasted_iota(jnp.int32, (CHUNK, CHUNK), 1)
    wcat = jnp.concatenate([jnp.where(col <= row, w_ref[g], 0.0).astype(BF16) for g in range(G_A)], axis=1)
    grp = lax.broadcasted_iota(jnp.int32, (CHUNK, D_A), 1) // (D_A // G_A)
    for ci in range(nchunk):
        vc = vab[ci * CHUNK:(ci + 1) * CHUNK]
        vbig = jnp.concatenate([jnp.where(grp == g, vc, jnp.zeros_like(vc)) for g in range(G_A)], axis=0)
        s = jnp.dot(wcat, vbig, preferred_element_type=F32) + b_ref[...]
        oa_ref[ci * CHUNK:(ci + 1) * CHUNK, :] = (u[ci * CHUNK:(ci + 1) * CHUNK] * s).astype(oa_ref.dtype)


def _sgu_prompt(ca, lng, lnb, w, bias_td, *, tm):
    m = ca.shape[0]
    return pl.pallas_call(
        functools.partial(_sgu_prompt_kernel, nchunk=tm // CHUNK),
        grid=(m // tm,),
        in_specs=[pl.BlockSpec((tm, N_A_COLS), lambda i: (i, 0)),
                  pl.BlockSpec((1, D_A), lambda i: (0, 0)),
                  pl.BlockSpec((1, D_A), lambda i: (0, 0)),
                  pl.BlockSpec((G_A, CHUNK, CHUNK), lambda i: (0, 0, 0)),
                  pl.BlockSpec((CHUNK, D_A), lambda i: (0, 0))],
        out_specs=pl.BlockSpec((tm, D_A), lambda i: (i, 0)),
        out_shape=jax.ShapeDtypeStruct((m, D_A), BF16),
        compiler_params=_cparams("parallel"),
        name="sgu_prompt",
    )(ca, lng, lnb, w, bias_td)


def _sgu_sample_kernel(ca_ref, lng_ref, lnb_ref, w0_ref, b0_ref, oa_ref, va_ref):
    u, va = _gelu_ln(ca_ref[...], lng_ref[...], lnb_ref[...])
    va_ref[...] = va
    oa_ref[...] = (u * (va * w0_ref[...] + b0_ref[...])).astype(oa_ref.dtype)


def _sgu_sample(ca, lng, lnb, w0, b0):
    m = ca.shape[0]
    row = lambda n: pl.BlockSpec((1, n), lambda i: (0, 0))
    return pl.pallas_call(
        _sgu_sample_kernel,
        grid=(1,),
        in_specs=[pl.BlockSpec((m, N_A_COLS), lambda i: (0, 0)), row(D_A), row(D_A), row(D_A), row(D_A)],
        out_specs=[pl.BlockSpec((m, D_A), lambda i: (0, 0))] * 2,
        out_shape=[jax.ShapeDtypeStruct((m, D_A), BF16), jax.ShapeDtypeStruct((m, D_A), F32)],
        compiler_params=_cparams("arbitrary"),
        name="sgu_sample",
    )(ca, lng, lnb, w0, b0)


def _seg_sum(x, ones_bd):
    hi = x.astype(BF16)
    lo = (x - hi.astype(F32)).astype(BF16)
    return (jnp.dot(hi, ones_bd, preferred_element_type=F32)
            + jnp.dot(lo, ones_bd, preferred_element_type=F32))


def _rwkv_pre_kernel(*refs, seq):
    if seq:
        c_ref, prev_ref = refs[:2]
    else:
        c_ref, sh_ref = refs[:2]
    (mu_ref, w0_ref, w2_ref, a0_ref, a2_ref, g2_ref, kk_ref, ka_ref, rk_ref, bd_ref) = refs[2:12]
    (r_o, w_o, k_o, v_o, a_o, b_o, g_o, rkv_o) = refs[12:20]
    if seq:
        last_o, carry_ref = refs[20:22]
        cols = c_ref[0]
        tm = cols.shape[0]

        @pl.when(pl.program_id(1) == 0)
        def _():
            carry_ref[...] = prev_ref[0]

        rolled = pltpu.roll(cols, 1, 0)
        first = lax.broadcasted_iota(jnp.int32, cols.shape, 0) == 0
        shifted = jnp.where(first, carry_ref[...], rolled)
        carry_ref[...] = cols[tm - 1:tm, :]
        last_o[0] = cols[tm - 1:tm, :]
    else:
        cols = c_ref[...]
        shifted = sh_ref[...]
    xs = cols + (shifted - cols) * mu_ref[...]
    r = xs[:, 0:D_B]
    k = xs[:, D_B:2 * D_B]
    v = xs[:, 2 * D_B:3 * D_B]
    o = 3 * D_B
    wl = xs[:, o:o + LW_PAD]
    al = xs[:, o + LW_PAD:o + LW_PAD + LA_PAD]
    gl = xs[:, o + LW_PAD + LA_PAD:o + LW_PAD + LA_PAD + LG_PAD]
    z = -(w0_ref[...] + jnp.dot(jnp.tanh(wl).astype(BF16), w2_ref[...], preferred_element_type=F32))
    softplus = jnp.maximum(z, 0.0) + jnp.log1p(jnp.exp(-jnp.abs(z)))
    log_decay = -jnp.exp(-softplus - 0.5)
    decay = log_decay if seq else jnp.exp(log_decay)
    a = jax.nn.sigmoid(a0_ref[...] + jnp.dot(al.astype(BF16), a2_ref[...], preferred_element_type=F32))
    g = jnp.dot(jax.nn.sigmoid(gl).astype(BF16), g2_ref[...], preferred_element_type=F32)
    bd = bd_ref[...]
    kk = k * kk_ref[...]
    kk = kk / jnp.maximum(jnp.sqrt(_seg_sum(kk * kk, bd)), 1e-12)
    k2 = k * (1.0 + (a - 1.0) * ka_ref[...])
    if seq:
        r_o[0], w_o[0], k_o[0], v_o[0], a_o[0], b_o[0], g_o[0] = r, decay, k2, v, -kk, kk * a, g
        rkv_o[0] = _seg_sum(r * k2 * rk_ref[...], bd) * v
    else:
        r_o[...], w_o[...], k_o[...], v_o[...], a_o[...], b_o[...], g_o[...] = r, decay, k2, v, -kk, kk * a, g
        rkv_o[...] = _seg_sum(r * k2 * rk_ref[...], bd) * v


def _rwkv_pre(cols, shift_src, params, *, seq, tm):
    prm_specs = []
    for p in params:
        prm_specs.append(pl.BlockSpec(p.shape, (lambda b, i: (0, 0)) if seq else (lambda i: (0, 0))))
    if seq:
        nb, t, _ = cols.shape
        grid = (nb, t // tm)
        in_specs = [pl.BlockSpec((1, tm, N_B_PAD), lambda b, i: (b, i, 0)),
                    pl.BlockSpec((1, 1, N_B_PAD), lambda b, i: (b, 0, 0))] + prm_specs
        ospec = pl.BlockSpec((1, tm, D_B), lambda b, i: (b, i, 0))
        oshape = jax.ShapeDtypeStruct((nb, t, D_B), F32)
        out_specs = [ospec] * 8 + [pl.BlockSpec((1, 1, N_B_PAD), lambda b, i: (b, 0, 0))]
        out_shape = [oshape] * 8 + [jax.ShapeDtypeStruct((nb, 1, N_B_PAD), F32)]
        scratch = [pltpu.VMEM((1, N_B_PAD), F32)]
        sem = ("parallel", "arbitrary")
    else:
        m = cols.shape[0]
        grid = (m // tm,)
        in_specs = [pl.BlockSpec((tm, N_B_PAD), lambda i: (i, 0))] * 2 + prm_specs
        out_specs = [pl.BlockSpec((tm, D_B), lambda i: (i, 0))] * 8
        out_shape = [jax.ShapeDtypeStruct((m, D_B), F32)] * 8
        scratch = []
        sem = ("parallel",)
    return pl.pallas_call(
        functools.partial(_rwkv_pre_kernel, seq=seq),
        grid=grid, in_specs=in_specs, out_specs=out_specs, out_shape=out_shape,
        scratch_shapes=scratch, compiler_params=_cparams(*sem),
        name="rwkv_pre_seq" if seq else "rwkv_pre_rows",
    )(cols, shift_src, *params)


def _scan_pair_step(S, a, w, b, k, v, r):
    lane = lax.broadcasted_iota(jnp.int32, (N_B, LANES), 1)
    row = lax.broadcasted_iota(jnp.int32, (N_B, LANES), 0)
    lo = lane < N_B
    e1 = lane == row
    e2 = lane == row + N_B

    def half_sums(p):
        t1 = jnp.sum(jnp.where(lo, p, 0.0), axis=1, keepdims=True)
        t2 = jnp.sum(jnp.where(lo, 0.0, p), axis=1, keepdims=True)
        return t1, t2

    t1, t2 = half_sums(S * a)
    sa = jnp.where(lo, t1, t2)
    vb = jnp.where(lo, jnp.sum(jnp.where(e1, v, 0.0), axis=1, keepdims=True),
                   jnp.sum(jnp.where(e2, v, 0.0), axis=1, keepdims=True))
    S = S * w + sa * b + vb * k
    y1, y2 = half_sums(S * r)
    y = jnp.sum(jnp.where(e1, y1, 0.0) + jnp.where(e2, y2, 0.0), axis=0, keepdims=True)
    return S, y


def _split(x):
    hi = x.astype(BF16)
    return hi, (x - hi.astype(F32)).astype(BF16)


def _mm3(xs, ys, nt=False):
    lhs = jnp.concatenate([xs[0], xs[0], xs[1]], axis=1)
    if nt:
        rhs = jnp.concatenate([ys[0], ys[1], ys[0]], axis=1)
        return lax.dot_general(lhs, rhs, (((1,), (1,)), ((), ())), preferred_element_type=F32)
    rhs = jnp.concatenate([ys[0], ys[1], ys[0]], axis=0)
    return jnp.dot(lhs, rhs, preferred_element_type=F32)


def _mm1(x, y, nt=False):
    if nt:
        return lax.dot_general(x, y, (((1,), (1,)), ((), ())), preferred_element_type=F32)
    return jnp.dot(x, y, preferred_element_type=F32)


def _block_diag(x):
    lo = lax.broadcasted_iota(jnp.int32, x.shape, 1) < N_B
    zero = jnp.zeros_like(x)
    return jnp.concatenate([jnp.where(lo, x, zero), jnp.where(lo, zero, x)], axis=0)


def _each(fn, *lists):
    return [fn(*args) for args in zip(*lists)]


def _rwkv_chunk_pairs(zs, lws, a_s, bs, ks, vs, rs):
    c = lws[0].shape[0]
    tri = (lax.broadcasted_iota(jnp.int32, (c, c), 0) >= lax.broadcasted_iota(jnp.int32, (c, c), 1)).astype(BF16)
    tri3 = jnp.concatenate([tri, tri, tri], axis=1)

    def cumsum(lw):
        l1 = lw.astype(BF16)
        r1 = lw - l1.astype(F32)
        l2 = r1.astype(BF16)
        l3 = (r1 - l2.astype(F32)).astype(BF16)
        return jnp.dot(tri3, jnp.concatenate([l1, l2, l3], axis=0), preferred_element_type=F32)

    lam = _each(cumsum, lws)
    lam_c = [x[c - 1:c, :] for x in lam]
    w_inv = [jnp.exp(-x) for x in lam]
    w_rem = _each(lambda lc, x: jnp.exp(lc - x), lam_c, lam)
    r_h = _each(lambda r, x: _block_diag(r * jnp.exp(x)), rs, lam)
    s_a = _each(lambda a, x, lw: _split(_block_diag(a * jnp.exp(x - lw))), a_s, lam, lws)
    h_r = [x.astype(BF16) for x in r_h]
    s_b = _each(lambda b, wi: _split(_block_diag(b * wi)), bs, w_inv)
    s_k = _each(lambda k, wi: _split(_block_diag(k * wi)), ks, w_inv)
    s_v = [_split(_block_diag(v)) for v in vs]
    n = 2 * c
    ri = lax.broadcasted_iota(jnp.int32, (n, n), 0)
    ci = lax.broadcasted_iota(jnp.int32, (n, n), 1)
    strict, incl, eye = ci < ri, ci <= ri, ci == ri
    gram = lambda x, y: jnp.where(strict, _mm3(x, y, nt=True), 0.0)
    lab = _each(gram, s_a, s_b)
    lak = _each(gram, s_a, s_k)
    h_lrb = _each(lambda x, y: jnp.where(incl, _mm1(x, y[0], nt=True), 0.0).astype(BF16), h_r, s_b)
    h_lrk = _each(lambda x, y: jnp.where(incl, _mm1(x, y[0], nt=True), 0.0).astype(BF16), h_r, s_k)
    t = [jnp.where(eye, 1.0, 0.0) + x for x in lab]
    s_p = _each(_split, lab)
    for _ in range(int(math.log2(c)) - 1):
        s_p = _each(lambda p: _split(_mm3(p, p)), s_p)
        t = _each(lambda tt, p: tt + _mm3(_split(tt), p), t, s_p)
    s_t = _each(_split, t)
    s_at = _each(lambda tt, a: _split(_mm3(tt, a)), s_t, s_a)
    x2 = _each(lambda l, v: _split(_mm3(_split(l), v)), lak, s_v)
    s_u0 = _each(lambda tt, x: _split(_mm3(tt, x)), s_t, x2)
    rt = _each(lambda rh, l, at: rh + _mm1(l, at[0]), r_h, h_lrb, s_at)
    y0 = _each(lambda l, u, lk, v: _mm1(l, u[0]) + _mm1(lk, v[0]), h_lrb, s_u0, h_lrk, s_v)
    s_bt = _each(lambda b, wr: _split(_block_diag(b * wr).T), bs, w_rem)
    s_kt = _each(lambda k, wr: _split(_block_diag(k * wr).T), ks, w_rem)
    m = _each(lambda lc, bt, at: jnp.where(eye, jnp.exp(lc), 0.0) + _mm3(bt, at), lam_c, s_bt, s_at)
    nn = _each(lambda bt, u, kt, v: _mm3(bt, u) + _mm3(kt, v), s_bt, s_u0, s_kt, s_v)
    s_z = _each(_split, zs)
    ybd = _each(lambda r, z, y: _mm1(r.astype(BF16), z[0]) + y, rt, s_z, y0)
    z_new = _each(lambda mm, z, x: _mm3(_split(mm), z) + x, m, s_z, nn)
    return z_new, [y[:c] + y[c:] for y in ybd]


def _rwkv_scan_seq_kernel(r_ref, w_ref, k_ref, v_ref, a_ref, b_ref, y_ref, sf_ref, z_ref, *, nb):
    npair = H_B // 2
    ci = pl.program_id(0)

    @pl.when(ci == 0)
    def _():
        z_ref[...] = jnp.zeros_like(z_ref)

    sls = [(bi, slice(None), slice(p * LANES, (p + 1) * LANES)) for bi in range(nb) for p in range(npair)]
    rows = lambda ref: [ref[sl] for sl in sls]
    zs, ys = _rwkv_chunk_pairs([z_ref[c] for c in range(len(sls))], rows(w_ref), rows(a_ref),
                               rows(b_ref), rows(k_ref), rows(v_ref), rows(r_ref))
    for c, sl in enumerate(sls):
        z_ref[c] = zs[c]
        y_ref[sl] = ys[c]

    @pl.when(ci == pl.num_programs(0) - 1)
    def _():
        for c, (bi, _, _) in enumerate(sls):
            st = z_ref[c].T
            sf_ref[bi, 2 * (c % npair)] = st[:N_B, :N_B]
            sf_ref[bi, 2 * (c % npair) + 1] = st[N_B:, N_B:]


def _rwkv_scan_seq(r, lw, k, v, a, b, *, tc):
    nb, t, _ = r.shape
    spec = pl.BlockSpec((nb, tc, D_B), lambda c: (0, c, 0))
    return pl.pallas_call(
        functools.partial(_rwkv_scan_seq_kernel, nb=nb),
        grid=(t // tc,),
        in_specs=[spec] * 6,
        out_specs=[spec, pl.BlockSpec((nb, H_B, N_B, N_B), lambda c: (0, 0, 0, 0))],
        out_shape=[jax.ShapeDtypeStruct((nb, t, D_B), F32), jax.ShapeDtypeStruct((nb, H_B, N_B, N_B), F32)],
        scratch_shapes=[pltpu.VMEM((nb * H_B // 2, LANES, LANES), F32)],
        compiler_params=_cparams("arbitrary"),
        name="rwkv_scan_seq",
    )(r, lw, k, v, a, b)


def _rwkv_scan_rows_kernel(r_ref, w_ref, k_ref, v_ref, a_ref, b_ref, s0_ref, y_ref, sf_ref, *, tb):
    def group(gi, carry):
        base = pl.multiple_of(gi * SUBLANES, SUBLANES)
        for p in range(H_B // 2):
            sl = (pl.ds(base, SUBLANES), slice(p * LANES, (p + 1) * LANES))
            a, w, b, k, v, r = (ref[sl] for ref in (a_ref, w_ref, b_ref, k_ref, v_ref, r_ref))
            ys = []
            for j in range(SUBLANES):
                row = slice(j, j + 1)
                S0 = jnp.concatenate([s0_ref[base + j, 2 * p], s0_ref[base + j, 2 * p + 1]], axis=1)
                S, y = _scan_pair_step(S0, a[row], w[row], b[row], k[row], v[row], r[row])
                sf_ref[base + j, 2 * p] = S[:, :N_B]
                sf_ref[base + j, 2 * p + 1] = S[:, N_B:]
                ys.append(y)
            y_ref[sl] = jnp.concatenate(ys, axis=0)
        return carry

    lax.fori_loop(0, tb // SUBLANES, group, 0)


def _rwkv_scan_rows(r, w, k, v, a, b, s0, *, tb):
    m = r.shape[0]
    spec = pl.BlockSpec((tb, D_B), lambda i: (i, 0))
    sspec = pl.BlockSpec((tb, H_B, N_B, N_B), lambda i: (i, 0, 0, 0))
    return pl.pallas_call(
        functools.partial(_rwkv_scan_rows_kernel, tb=tb),
        grid=(m // tb,),
        in_specs=[spec] * 6 + [sspec],
        out_specs=[spec, sspec],
        out_shape=[jax.ShapeDtypeStruct((m, D_B), F32), jax.ShapeDtypeStruct((m, H_B, N_B, N_B), F32)],
        compiler_params=_cparams("parallel"),
        name="rwkv_scan_rows",
    )(r, w, k, v, a, b, s0)


def _rwkv_post_kernel(y_ref, rkv_ref, g_ref, lg_ref, lb_ref, bd_ref, o_ref):
    y = y_ref[...]
    bd = bd_ref[...]
    d = y - _seg_sum(y, bd) * (1.0 / N_B)
    var = _seg_sum(d * d, bd) * (1.0 / N_B)
    yn = d * lax.rsqrt(var + GN_EPS_B) * lg_ref[...] + lb_ref[...]
    o_ref[...] = ((yn + rkv_ref[...]) * g_ref[...]).astype(o_ref.dtype)


def _rwkv_post(y, rkv, g, lg, lb, bd, *, tm):
    m = y.shape[0]
    spec = pl.BlockSpec((tm, D_B), lambda i: (i, 0))
    row = pl.BlockSpec((1, D_B), lambda i: (0, 0))
    return pl.pallas_call(
        _rwkv_post_kernel,
        grid=(m // tm,),
        in_specs=[spec, spec, spec, row, row, pl.BlockSpec((D_B, D_B), lambda i: (0, 0))],
        out_specs=spec,
        out_shape=jax.ShapeDtypeStruct((m, D_B), BF16),
        compiler_params=_cparams("parallel"),
        name="rwkv_post",
    )(y, rkv, g, lg, lb, bd)


def _lam_full(lq1, lk1, lq2, lk2, lam_init):
    return (jnp.exp(jnp.sum(lq1 * lk1, axis=-1, keepdims=True))
            - jnp.exp(jnp.sum(lq2 * lk2, axis=-1, keepdims=True)) + lam_init)


ONES_ROWS = 16


def _attn_prompt_kernel(q_ref, k_ref, vt_ref, lq1, lk1, lq2, lk2, sg_ref, o_ref, m1, a1, m2, a2, s_a, s_b, *,
                        tk, lam_init):
    qi = pl.program_id(2)
    for m_ref, a_ref in ((m1, a1), (m2, a2)):
        m_ref[...] = jnp.full_like(m_ref, -jnp.inf)
        a_ref[...] = jnp.zeros_like(a_ref)

    q = q_ref[0]
    lane = lax.broadcasted_iota(jnp.int32, q.shape, 1)
    zero = jnp.zeros_like(q)
    q_sub = (jnp.where(lane < D_HEAD, q, zero), jnp.where(lane < D_HEAD, zero, q))
    ones = jnp.ones((ONES_ROWS, tk), BF16)

    units = [(qs[c0:c0 + ATTN_STRIP], m_ref, a_ref, c0)
             for qs, m_ref, a_ref in ((q_sub[0], m1, a1), (q_sub[1], m2, a2))
             for c0 in range(0, q.shape[0], ATTN_STRIP)]

    n_units = len(units)

    def scores(u, kis):
        return [lax.dot_general(k_ref[0, pl.ds(pl.multiple_of(ki * tk, tk), tk), :], units[u][0],
                                (((1,), (1,)), ((), ())), preferred_element_type=F32) for ki in kis]

    def stash(s_ref, u, blocks):
        for i, st in enumerate(blocks):
            s_ref[u, i] = st

    def unstash(s_ref, u):
        return [s_ref[u, i] for i in range(ATTN_GROUP)]

    def softmax_pv(u, kis, st_blocks, masked):
        _, m_ref, a_ref, c0 = units[u]
        cols = slice(c0, c0 + ATTN_STRIP)
        if masked:
            kr = lax.broadcasted_iota(jnp.int32, st_blocks[0].shape, 0)
            qc = lax.broadcasted_iota(jnp.int32, st_blocks[0].shape, 1) + c0
            st_blocks = [jnp.where(kr <= qc, st, -jnp.inf) for st in st_blocks]
        m_old = m_ref[:, cols]
        m_new = m_old
        for st in st_blocks:
            m_new = jnp.maximum(m_new, jnp.max(st, axis=0, keepdims=True))
        acc = jnp.exp2(m_old - m_new) * a_ref[:, cols]
        for st, ki in zip(st_blocks, kis):
            vt1 = jnp.concatenate([vt_ref[0, ki], ones], axis=0)
            acc = acc + jnp.dot(vt1, jnp.exp2(st - m_new).astype(BF16), preferred_element_type=F32)
        a_ref[:, cols] = acc
        m_ref[:, cols] = m_new

    def update(kis, masked):
        sts = [scores(u, kis) for u in range(n_units)]
        for u in range(n_units):
            softmax_pv(u, kis, sts[u], masked)

    def pipelined(cur_kis, cur_ref, next_kis, next_ref):
        for u in range(n_units):
            stash(next_ref, u, scores(u, next_kis))
            softmax_pv(u, cur_kis, unstash(cur_ref, u), False)

    group = lambda g: [ATTN_GROUP * g + i for i in range(ATTN_GROUP)]
    n_groups = lax.shift_right_logical(qi, ATTN_GROUP.bit_length() - 1)

    @pl.when(n_groups > 0)
    def _():
        for u in range(n_units):
            stash(s_a, u, scores(u, group(0)))

    def pair_body(jj, carry):
        g0 = 2 * jj
        pipelined(group(g0), s_a, group(g0 + 1), s_b)
        pipelined(group(g0 + 1), s_b, [jnp.minimum(ki, qi) for ki in group(g0 + 2)], s_a)
        return carry

    lax.fori_loop(0, lax.shift_right_logical(n_groups, 1), pair_body, 0)

    @pl.when((n_groups & 1) == 1)
    def _():
        for u in range(n_units):
            softmax_pv(u, group(n_groups - 1), unstash(s_a, u), False)

    def single_body(ki, carry):
        update([ki], False)
        return carry

    lax.fori_loop(n_groups * ATTN_GROUP, qi, single_body, 0)
    update([qi], True)
    lam = _lam_full(lq1[...], lk1[...], lq2[...], lk2[...], lam_init)
    d = 2 * D_HEAD
    ot = a1[:d, :] / a1[d:d + 1, :] - lam * (a2[:d, :] / a2[d:d + 1, :])
    on = ot * lax.rsqrt(jnp.mean(ot * ot, axis=0, keepdims=True) + SUBLN_EPS)
    o_ref[0] = (on.T * sg_ref[...] * (1.0 - lam_init)).astype(o_ref.dtype)


def _attn_prompt(q, k, vt, lams, subln_g, lam_init, *, tq):
    nb, t, _ = q.shape
    tk = vt.shape[3]
    assert tk == tq
    qspec = pl.BlockSpec((1, tq, LANES), lambda b, h, i: (b, i, h))
    kspec = pl.BlockSpec((1, t, LANES), lambda b, h, i: (b, 0, h))
    vspec = pl.BlockSpec((1, t // tk, LANES, tk), lambda b, h, i: (b, 0, h, 0))
    row = lambda n: pl.BlockSpec((1, n), lambda b, h, i: (0, 0))
    acc = 2 * D_HEAD + ONES_ROWS
    return pl.pallas_call(
        functools.partial(_attn_prompt_kernel, tk=tk, lam_init=lam_init),
        grid=(nb, H_C, t // tq),
        in_specs=[qspec, kspec, vspec, row(D_HEAD), row(D_HEAD), row(D_HEAD), row(D_HEAD), row(LANES)],
        out_specs=qspec,
        out_shape=jax.ShapeDtypeStruct(q.shape, BF16),
        scratch_shapes=[pltpu.VMEM((1, tq), F32), pltpu.VMEM((acc, tq), F32),
                        pltpu.VMEM((1, tq), F32), pltpu.VMEM((acc, tq), F32)]
                       + [pltpu.VMEM((2 * tq // ATTN_STRIP, ATTN_GROUP, tk, ATTN_STRIP), F32)] * 2,
        compiler_params=_cparams("parallel", "parallel", "arbitrary"),
        name="attn_prompt",
    )(q, k, vt, *lams, subln_g)


def _attn_sample_kernel(pt_ref, q_ref, kn_ref, vn_ref, *refs, npg, lam_init):
    k_refs = refs[:npg]
    v_refs = refs[npg:2 * npg]
    lq1, lk1, lq2, lk2, sg_ref, o_ref, m_ref, l_ref, acc_ref = refs[2 * npg:]
    g = pl.program_id(1)
    nrow, ncol = 2 * H_C, PAGE_SIZE * H_C
    q = q_ref[0]
    lo = lax.broadcasted_iota(jnp.int32, q.shape, 1) < D_HEAD
    qz = jnp.concatenate([jnp.where(lo, q, 0.0), jnp.where(lo, 0.0, q)], axis=0).astype(BF16)
    valid = ((lax.broadcasted_iota(jnp.int32, (nrow, ncol), 1) & (H_C - 1))
             == (lax.broadcasted_iota(jnp.int32, (nrow, ncol), 0) & (H_C - 1)))

    @pl.when(g == 0)
    def _():
        as_mxu = lambda ref: jnp.concatenate([ref[0], ref[0]], axis=0).astype(BF16).astype(F32)
        m_ref[...] = jnp.sum(qz.astype(F32) * as_mxu(kn_ref), axis=1, keepdims=True)
        l_ref[...] = jnp.ones_like(l_ref)
        acc_ref[...] = as_mxu(vn_ref)

    sts = []
    for r in range(npg):
        kmat = k_refs[r][...].reshape(ncol, LANES).astype(BF16)
        st = lax.dot_general(qz, kmat, (((1,), (1,)), ((), ())), preferred_element_type=F32)
        sts.append(jnp.where(valid, st, -jnp.inf))
    m_old = m_ref[...]
    m_new = m_old
    for st in sts:
        m_new = jnp.maximum(m_new, jnp.max(st, axis=1, keepdims=True))
    alpha = jnp.exp2(m_old - m_new)
    l = alpha * l_ref[...]
    acc = alpha * acc_ref[...]
    for st, v_ref in zip(sts, v_refs):
        p = jnp.exp2(st - m_new)
        l = l + jnp.sum(p, axis=1, keepdims=True)
        acc = acc + jnp.dot(p.astype(BF16), v_ref[...].reshape(ncol, LANES).astype(BF16),
                            preferred_element_type=F32)
    m_ref[...] = m_new
    l_ref[...] = l
    acc_ref[...] = acc

    @pl.when(g == pl.num_programs(1) - 1)
    def _():
        o = acc_ref[...] / l_ref[...]
        lam = _lam_full(lq1[...], lk1[...], lq2[...], lk2[...], lam_init)
        o_ref[0] = (_rms(o[:H_C] - lam * o[H_C:], sg_ref[...], SUBLN_EPS) * (1.0 - lam_init)).astype(o_ref.dtype)


def _attn_sample(q, k_new, v_new, cache_k, cache_v, page_table, layer, lams, subln_g, lam_init, *, npg):
    nb = q.shape[0]
    n_pages = page_table.shape[1]
    hspec = pl.BlockSpec((1, H_C, LANES), lambda b, g, pt: (b, 0, 0))
    row = lambda n: pl.BlockSpec((1, n), lambda b, g, pt: (0, 0))

    def page_spec(r):
        return pl.BlockSpec((None, None, PAGE_SIZE, H_C, LANES),
                            lambda b, g, pt: (layer, pt[b, g * npg + r], 0, 0, 0))

    grid_spec = pltpu.PrefetchScalarGridSpec(
        num_scalar_prefetch=1,
        grid=(nb, n_pages // npg),
        in_specs=[hspec, hspec, hspec] + [page_spec(r) for r in range(npg)] * 2
                 + [row(D_HEAD), row(D_HEAD), row(D_HEAD), row(D_HEAD), row(LANES)],
        out_specs=hspec,
        scratch_shapes=[pltpu.VMEM((2 * H_C, 1), F32), pltpu.VMEM((2 * H_C, 1), F32),
                        pltpu.VMEM((2 * H_C, LANES), F32)],
    )
    return pl.pallas_call(
        functools.partial(_attn_sample_kernel, npg=npg, lam_init=lam_init),
        grid_spec=grid_spec,
        out_shape=jax.ShapeDtypeStruct((nb, H_C, LANES), BF16),
        compiler_params=_cparams("parallel", "arbitrary"),
        name="attn_sample",
    )(page_table, q, k_new, v_new, *([cache_k] * npg), *([cache_v] * npg), *lams, subln_g)


def _merge_kernel(x_ref, oa_ref, ob_ref, oc_ref, cg_ref, wa_ref, wb_ref, wc_ref, wo_ref, xo_ref):
    gates = cg_ref[...].astype(F32)
    dot = lambda a, w: jnp.dot(a, w, preferred_element_type=F32)
    m = (gates[:, 0:D_MODEL] * dot(oa_ref[...], wa_ref[...])
         + gates[:, D_MODEL:2 * D_MODEL] * dot(ob_ref[...], wb_ref[...])
         + gates[:, 2 * D_MODEL:3 * D_MODEL] * dot(oc_ref[...], wc_ref[...]))
    xo_ref[...] = x_ref[...] + dot(m.astype(BF16), wo_ref[...])


def _merge(x, oa, ob, oc, cg, wa, wb, wc, wo, *, tm):
    m = x.shape[0]
    rows = lambda n: pl.BlockSpec((tm, n), lambda i: (i, 0))
    full = lambda a: pl.BlockSpec(a.shape, lambda i: (0, 0))
    return pl.pallas_call(
        _merge_kernel,
        grid=(m // tm,),
        in_specs=[rows(D_MODEL), rows(D_A), rows(D_B), rows(D_C), rows(3 * D_MODEL),
                  full(wa), full(wb), full(wc), full(wo)],
        out_specs=rows(D_MODEL),
        out_shape=jax.ShapeDtypeStruct((m, D_MODEL), F32),
        compiler_params=_cparams("parallel"),
        name="merge",
    )(x, oa, ob, oc, cg, wa, wb, wc, wo)


def _ffn_kernel(x_ref, g_ref, wu_ref, wd_ref, gf_ref, o_ref, h_ref, acc_ref, *, final):
    j = pl.program_id(1)

    @pl.when(j == 0)
    def _():
        h_ref[...] = _rms(x_ref[...], g_ref[...], EPS).astype(BF16)
        acc_ref[...] = jnp.zeros_like(acc_ref)

    up = jnp.dot(h_ref[...], wu_ref[...], preferred_element_type=F32)
    act = jnp.square(jnp.maximum(up, 0.0)).astype(BF16)
    acc_ref[...] += jnp.dot(act, wd_ref[...], preferred_element_type=F32)

    @pl.when(j == pl.num_programs(1) - 1)
    def _():
        xn = x_ref[...] + acc_ref[...]
        o_ref[...] = _rms(xn, gf_ref[...], EPS) if final else xn


def _ffn(x, g, wu, wd, gf, *, final, tm, tf):
    m = x.shape[0]
    row = pl.BlockSpec((1, D_MODEL), lambda i, j: (0, 0))
    return pl.pallas_call(
        functools.partial(_ffn_kernel, final=final),
        grid=(m // tm, D_FF // tf),
        in_specs=[pl.BlockSpec((tm, D_MODEL), lambda i, j: (i, 0)), row,
                  pl.BlockSpec((D_MODEL, tf), lambda i, j: (0, j)),
                  pl.BlockSpec((tf, D_MODEL), lambda i, j: (j, 0)), row],
        out_specs=pl.BlockSpec((tm, D_MODEL), lambda i, j: (i, 0)),
        out_shape=jax.ShapeDtypeStruct((m, D_MODEL), F32),
        scratch_shapes=[pltpu.VMEM((tm, D_MODEL), BF16), pltpu.VMEM((tm, D_MODEL), F32)],
        compiler_params=_cparams("parallel", "arbitrary"),
        name="ffn",
    )(x, g, wu, wd, gf)


def _pad_b_cols(a):
    o = 3 * D_B
    z = lambda n: jnp.zeros(a.shape[:-1] + (n,), a.dtype)
    return jnp.concatenate([a[..., :o + LORA_W], z(LW_PAD - LORA_W),
                            a[..., o + LORA_W:o + LORA_W + LORA_A], z(LA_PAD - LORA_A),
                            a[..., o + LORA_W + LORA_A:], z(LG_PAD - LORA_G)], axis=-1)


def _unpad_b_cols(a):
    o = 3 * D_B
    return jnp.concatenate([a[..., :o + LORA_W], a[..., o + LW_PAD:o + LW_PAD + LORA_A],
                            a[..., o + LW_PAD + LA_PAD:o + LW_PAD + LA_PAD + LORA_G]], axis=-1)


def _pad_rows(a, n):
    return jnp.concatenate([a, jnp.zeros((n - a.shape[0],) + a.shape[1:], a.dtype)], axis=0)


def _ones_blockdiag(n, seg):
    i = np.arange(n) // seg
    return jnp.asarray((i[:, None] == i[None, :]).astype(np.float32), dtype=BF16)


def _layer_weights(l, p):
    w_in = p["w_in"][l]
    o_b = N_A_COLS
    o_q = o_b + N_B_COLS
    w = dict(
        norm_mix=p["norm_mix"][l].reshape(1, D_MODEL),
        w_a=w_in[:, :o_b].astype(BF16),
        w_b=_pad_b_cols(w_in[:, o_b:o_q]).astype(BF16),
        w_q=w_in[:, o_q:o_q + D_C].astype(BF16),
        w_k=w_in[:, o_q + D_C:o_q + 2 * D_C].astype(BF16),
        w_v=w_in[:, o_q + 2 * D_C:o_q + 3 * D_C].astype(BF16),
        w_g=w_in[:, o_q + 3 * D_C:].astype(BF16),
        lng=p["sgu_ln_g"][l].reshape(1, D_A), lnb=p["sgu_ln_b"][l].reshape(1, D_A),
        sgu_w=p["sgu_w"][l],
        sgu_bias_td=jnp.repeat(p["sgu_b"][l].T, D_A // G_A, axis=1),
        sgu_w0=jnp.repeat(p["sgu_w"][l][:, 0, 0], D_A // G_A).reshape(1, D_A),
        sgu_b0=jnp.repeat(p["sgu_b"][l][:, 0], D_A // G_A).reshape(1, D_A),
        rwkv=[_pad_b_cols(p["shift_mu"][l]).reshape(1, N_B_PAD),
              p["w0"][l].reshape(1, D_B), _pad_rows(p["w2"][l], LW_PAD).astype(BF16),
              p["a0"][l].reshape(1, D_B), _pad_rows(p["a2"][l], LA_PAD).astype(BF16),
              _pad_rows(p["g2"][l], LG_PAD).astype(BF16),
              p["k_k"][l].reshape(1, D_B), p["k_a"][l].reshape(1, D_B), p["r_k"][l].reshape(1, D_B),
              _ones_blockdiag(D_B, N_B)],
        lnx_g=p["lnx_g"][l].reshape(1, D_B), lnx_b=p["lnx_b"][l].reshape(1, D_B),
        lams=[p[n][l].reshape(1, D_HEAD) for n in ("lam_q1", "lam_k1", "lam_q2", "lam_k2")],
        subln_g=p["subln_g"][l].reshape(1, 2 * D_HEAD),
        w_br_a=p["w_br_a"][l].astype(BF16), w_br_b=p["w_br_b"][l].astype(BF16),
        w_br_c=p["w_br_c"][l].astype(BF16), w_out=p["w_out"][l].astype(BF16),
        norm_ffn=p["norm_ffn"][l].reshape(1, D_MODEL),
        w_up=p["w_up"][l].astype(BF16), w_down=p["w_down"][l].astype(BF16),
        norm_final=p["norm_final"].reshape(1, D_MODEL),
    )
    return w


def _project_all(x, w, rope_tabs, tm, t_seq=None, kv_stack=None):
    pj = functools.partial(_proj, x, w["norm_mix"], tm=tm, tn=D_MODEL)
    qscale = math.log2(math.e) / math.sqrt(D_HEAD)
    (ca,) = pj(w["w_a"], out_dtypes=(F32,), name="proj_a")
    (cb,) = pj(w["w_b"], out_dtypes=(F32,), name="proj_b")
    (cg,) = pj(w["w_g"], out_dtypes=(BF16,), gate=True, name="proj_g")
    if t_seq is None:
        (q,) = pj(w["w_q"], out_dtypes=(F32,), rope_tabs=rope_tabs, scale=qscale, name="proj_q")
        (kf,) = pj(w["w_k"], out_dtypes=(F32,), rope_tabs=rope_tabs, name="proj_k")
        (vf,) = pj(w["w_v"], out_dtypes=(F32,), name="proj_v")
        return ca, cb, cg, q, kf, vf
    (q,) = pj(w["w_q"], out_dtypes=(BF16,), rope_tabs=rope_tabs, scale=qscale, name="proj_q")
    layer, k_buf, v_buf = kv_stack
    kf, kb = pj(w["w_k"], out_dtypes=(F32, BF16), rope_tabs=rope_tabs, stack=(layer, k_buf), name="proj_k")
    vf, vt = _proj(x, w["norm_mix"], w["w_v"], tm=ATTN_BLOCK, tn=D_MODEL, out_dtypes=(F32, BF16), t_seq=t_seq,
                   stack=(layer, v_buf), name="proj_v")
    return ca, cb, cg, q, kf, vf, kb, vt


def _prompt_layer(x, l, w, rope_tabs, nb, t, final, kv_stack):
    lam_init = 0.8 - 0.6 * math.exp(-0.3 * l)
    ca, cb, cg, qb, kf, vf, kb, vt = _project_all(x, w, rope_tabs, 1024, t_seq=t, kv_stack=kv_stack)
    o_a = _sgu_prompt(ca, w["lng"], w["lnb"], w["sgu_w"], w["sgu_bias_td"], tm=512)
    prev0 = jnp.zeros((nb, 1, N_B_PAD), F32)
    *rw, last = _rwkv_pre(cb.reshape(nb, t, N_B_PAD), prev0, w["rwkv"], seq=True, tm=256)
    r, dec, k2, v, na, bb, g, rkv = rw
    y, s_fin = _rwkv_scan_seq(r, dec, k2, v, na, bb, tc=64)
    flat = lambda a: a.reshape(nb * t, D_B)
    o_b = _rwkv_post(flat(y), flat(rkv), flat(g), w["lnx_g"], w["lnx_b"], w["rwkv"][-1], tm=1024)
    b3 = lambda a: a.reshape(nb, t, D_C)
    o_c = _attn_prompt(b3(qb), b3(kb), vt, w["lams"], w["subln_g"], lam_init, tq=ATTN_BLOCK)
    x = _merge(x, o_a, o_b, o_c.reshape(nb * t, D_C), cg, w["w_br_a"], w["w_br_b"], w["w_br_c"], w["w_out"], tm=512)
    x = _ffn(x, w["norm_ffn"], w["w_up"], w["w_down"], w["norm_final"], final=final, tm=1024, tf=1024)
    return x, kf, vf, s_fin, _unpad_b_cols(last[:, 0, :])


def _sample_layer(x, l, w, rope_tabs, cache_k, cache_v, page_table, prev_shift, s0, final):
    nb = x.shape[0]
    lam_init = 0.8 - 0.6 * math.exp(-0.3 * l)
    ca, cb, cg, qb, kf, vf = _project_all(x, w, rope_tabs, nb)
    o_a, va = _sgu_sample(ca, w["lng"], w["lnb"], w["sgu_w0"], w["sgu_b0"])
    r, dec, k2, v, na, bb, g, rkv = _rwkv_pre(cb, _pad_b_cols(prev_shift), w["rwkv"], seq=False, tm=nb)
    y, s_new = _rwkv_scan_rows(r, dec, k2, v, na, bb, s0, tb=32)
    o_b = _rwkv_post(y, rkv, g, w["lnx_g"], w["lnx_b"], w["rwkv"][-1], tm=nb)
    h3 = lambda a: a.reshape(nb, H_C, LANES)
    o_c = _attn_sample(h3(qb), h3(kf), h3(vf), cache_k, cache_v, page_table, l,
                       w["lams"], w["subln_g"], lam_init, npg=page_table.shape[1])
    x = _merge(x, o_a, o_b, o_c.reshape(nb, D_C), cg, w["w_br_a"], w["w_br_b"], w["w_br_c"], w["w_out"], tm=nb)
    x = _ffn(x, w["norm_ffn"], w["w_up"], w["w_down"], w["norm_final"], final=final, tm=nb, tf=1024)
    return x, kf, vf, s_new, _unpad_b_cols(cb), va


def kernel(x_prompt, x_sample, cache_k, cache_v, state_rwkv, state_shift, page_table, norm_mix, w_in, sgu_ln_g, sgu_ln_b, sgu_w, sgu_b, shift_mu, w0, w2, a0, a2, g2, k_k, k_a, r_k, lnx_g, lnx_b, lam_q1, lam_k1, lam_q2, lam_k2, subln_g, w_br_a, w_br_b, w_br_c, w_out, norm_ffn, w_up, w_down, norm_final):
    p = dict(norm_mix=norm_mix, w_in=w_in, sgu_ln_g=sgu_ln_g, sgu_ln_b=sgu_ln_b, sgu_w=sgu_w, sgu_b=sgu_b,
             shift_mu=shift_mu, w0=w0, w2=w2, a0=a0, a2=a2, g2=g2, k_k=k_k, k_a=k_a, r_k=r_k, lnx_g=lnx_g,
             lnx_b=lnx_b, lam_q1=lam_q1, lam_k1=lam_k1, lam_q2=lam_q2, lam_k2=lam_k2, subln_g=subln_g,
             w_br_a=w_br_a, w_br_b=w_br_b, w_br_c=w_br_c, w_out=w_out, norm_ffn=norm_ffn, w_up=w_up,
             w_down=w_down, norm_final=norm_final)
    depth = w_in.shape[0]
    bp, tp, _ = x_prompt.shape
    bs, ts, _ = x_sample.shape
    tabs_p = _rope_tables(jnp.arange(tp))
    tabs_s = _rope_tables(jnp.full((bs * ts,), PAST_LEN, jnp.int32))
    xp = x_prompt.reshape(bp * tp, D_MODEL)
    xs = x_sample.reshape(bs * ts, D_MODEL)
    outs_p, outs_s = [], []
    kp = jnp.zeros((depth * bp * tp, D_C), F32)
    vp = jnp.zeros((depth * bp * tp, D_C), F32)
    for l in range(depth):
        w = _layer_weights(l, p)
        final = l == depth - 1
        xp, kp, vp, sp, shp = _prompt_layer(xp, l, w, tabs_p, bp, tp, final, (l, kp, vp))
        xs, ks, vs, ss, shs, va = _sample_layer(xs, l, w, tabs_s, cache_k, cache_v, page_table,
                                                state_shift[l], state_rwkv[l], final)
        outs_p.append((sp, shp))
        outs_s.append((ks.reshape(bs, ts, H_C, 2 * D_HEAD), vs.reshape(bs, ts, H_C, 2 * D_HEAD), ss, shs,
                       va.reshape(bs, ts, D_A)))
    stack = lambda outs, i: jnp.stack([o[i] for o in outs])
    return (xp.reshape(bp, tp, D_MODEL), xs.reshape(bs, ts, D_MODEL),
            kp.reshape(depth, bp, tp, H_C, 2 * D_HEAD), vp.reshape(depth, bp, tp, H_C, 2 * D_HEAD),
            stack(outs_p, 0), stack(outs_p, 1),
            stack(outs_s, 0), stack(outs_s, 1), stack(outs_s, 2), stack(outs_s, 3), stack(outs_s, 4))
```
